```python
import math
import jax
import jax.numpy as jnp
from jax import lax
import numpy as np

D_MODEL = 1024
BATCH = 1
SEQ = 16384
DEPTH = 4

GRID_W = 64
CTX_LEN = 256
N_MIXERS = 4
NORM_EPS = 1e-6
ROPE_THETA = 10000.0
HEAD_DIM = 64
BLOCK = 128
NEG_INF = -1e30

DIFF_HEADS = D_MODEL // (2 * HEAD_DIM)
DIFF_V_DIM = 2 * HEAD_DIM
DIFF_QK_WIDTH = 2 * DIFF_HEADS * HEAD_DIM
DIFF_V_WIDTH = DIFF_HEADS * DIFF_V_DIM
SHORT_CONV_W = 3
WIN_Q_HEADS = D_MODEL // HEAD_DIM
WIN_KV_HEADS = 4
WIN_GROUP = WIN_Q_HEADS // WIN_KV_HEADS
WINDOW = 128
SPAN = BLOCK + 2 * WINDOW
CONF_CONV_W = 31
D_FF = -(-(8 * D_MODEL) // (3 * 256)) * 256

kernel_name = "hybrid_interleaved_diffusion_block"


def _n_layers_of(kind):
    return (DEPTH - kind + N_MIXERS - 1) // N_MIXERS


def rmsnorm(x, g):
    xf = x.astype(jnp.float32)
    y = xf * lax.rsqrt(jnp.mean(xf * xf, axis=-1, keepdims=True) + NORM_EPS)
    return (y * g.astype(jnp.float32)).astype(x.dtype)


def layernorm(x, g, b):
    xf = x.astype(jnp.float32)
    mu = jnp.mean(xf, axis=-1, keepdims=True)
    var = jnp.mean(jnp.square(xf - mu), axis=-1, keepdims=True)
    y = (xf - mu) * lax.rsqrt(var + NORM_EPS)
    return (y * g.astype(jnp.float32) + b.astype(jnp.float32)).astype(x.dtype)


def rope_tables(n):
    n_rows = n // GRID_W
    rows = jnp.broadcast_to(jnp.arange(n_rows, dtype=jnp.float32)[:, None], (n_rows, GRID_W)).reshape(-1)
    cols = jnp.broadcast_to(jnp.arange(GRID_W, dtype=jnp.float32)[None, :], (n_rows, GRID_W)).reshape(-1)
    n_freq = HEAD_DIM // 4
    inv_freq = ROPE_THETA ** (-jnp.arange(n_freq, dtype=jnp.float32) / n_freq)
    ang_r = rows[:, None] * inv_freq[None, :]
    ang_c = cols[:, None] * inv_freq[None, :]
    ang = jnp.concatenate([ang_r, ang_r, ang_c, ang_c], axis=-1)
    return jnp.cos(ang), jnp.sin(ang)


def apply_rope(x, cos, sin):
    x1, x2, x3, x4 = jnp.split(x, 4, axis=-1)
    rot = jnp.concatenate([-x2, x1, -x4, x3], axis=-1)
    shape = (1, x.shape[1]) + (1,) * (x.ndim - 3) + (x.shape[-1],)
    y = x.astype(jnp.float32) * cos.reshape(shape) + rot.astype(jnp.float32) * sin.reshape(shape)
    return y.astype(x.dtype)


def dwconv(x, w):
    k = w.shape[0]
    return lax.conv_general_dilated(
        x, w[:, None, :].astype(x.dtype), window_strides=(1,), padding=[(k // 2, k // 2)],
        dimension_numbers=("NWC", "WIO", "NWC"), feature_group_count=x.shape[-1])


def _diff_attend(q, k, v, lam):
    s = jnp.einsum("bqmhd,bkmhd->bmhqk", q, k, preferred_element_type=jnp.float32) * (HEAD_DIM ** -0.5)
    p = jax.nn.softmax(s, axis=-1)
    a = (p[:, 0] - lam * p[:, 1]).astype(v.dtype)
    return jnp.einsum("bhqk,bkhd->bqhd", a, v, preferred_element_type=jnp.float32)


def diff_attention(h_lat, h_ctx, w_qkv, w_o, q_norm, k_norm, lam_q1, lam_k1, lam_q2, lam_k2,
                   subln, lam_init, cos, sin, with_ctx):
    bsz, n, _ = h_lat.shape

    def project(h, rotary):
        q, k, v = jnp.split(h @ w_qkv, [DIFF_QK_WIDTH, 2 * DIFF_QK_WIDTH], axis=-1)
        q = rmsnorm(q.reshape(bsz, -1, 2, DIFF_HEADS, HEAD_DIM), q_norm)
        k = rmsnorm(k.reshape(bsz, -1, 2, DIFF_HEADS, HEAD_DIM), k_norm)
        if rotary:
            q, k = apply_rope(q, cos, sin), apply_rope(k, cos, sin)
        return q, k, v.reshape(bsz, -1, DIFF_HEADS, DIFF_V_DIM)

    lam = (jnp.exp(jnp.sum(lam_q1.astype(jnp.float32) * lam_k1.astype(jnp.float32)))
           - jnp.exp(jnp.sum(lam_q2.astype(jnp.float32) * lam_k2.astype(jnp.float32))) + lam_init)
    q_l, k_l, v_l = project(h_lat, True)
    q_c, k_c, v_c = project(h_ctx, False)
    k_all = jnp.concatenate([k_l, k_c], axis=1)
    v_all = jnp.concatenate([v_l, v_c], axis=1)
    nb = n // BLOCK
    q_blocks = jnp.moveaxis(q_l.reshape(bsz, nb, BLOCK, 2, DIFF_HEADS, HEAD_DIM), 1, 0)
    o_blocks = lax.map(lambda qb: _diff_attend(qb, k_all, v_all, lam), q_blocks)
    o_l = jnp.moveaxis(o_blocks, 0, 1).reshape(bsz, n, DIFF_HEADS, DIFF_V_DIM)

    def out(o):
        o = rmsnorm(o, subln) * (1.0 - lam_init)
        return o.reshape(bsz, -1, DIFF_V_WIDTH).astype(h_lat.dtype) @ w_o

    y_c = out(_diff_attend(q_c, k_c, v_c, lam)) if with_ctx else None
    return out(o_l), y_c


def short_conv(h, w_in, conv_w, w_out):
    b_gate, c_gate, u = jnp.split(h @ w_in, 3, axis=-1)
    return (b_gate * dwconv(c_gate * u, conv_w)) @ w_out


def _sink_attend(scores, values, sink):
    b, kv, g, q, _ = scores[0].shape
    sink_col = jnp.broadcast_to(sink.astype(jnp.float32)[None, :, :, None, None], (b, kv, g, q, 1))
    p = jax.nn.softmax(jnp.concatenate(scores + [sink_col], axis=-1), axis=-1)
    out = 0.0
    start = 0
    for s, v in zip(scores, values):
        kn = s.shape[-1]
        out = out + jnp.einsum("bkgqj,bjkd->bqkgd", p[..., start:start + kn].astype(v.dtype), v,
                               preferred_element_type=jnp.float32)
        start += kn
    return out


def window_gqa(h_lat, h_ctx, w_qkv, w_o, q_norm, k_norm, sink, cos, sin, with_ctx):
    bsz, n, _ = h_lat.shape
    qw = WIN_Q_HEADS * HEAD_DIM
    kw = WIN_KV_HEADS * HEAD_DIM
    scale = HEAD_DIM ** -0.5

    def project(h, rotary):
        q, k, v = jnp.split(h @ w_qkv, [qw, qw + kw], axis=-1)
        q = rmsnorm(q.reshape(bsz, -1, WIN_KV_HEADS, WIN_GROUP, HEAD_DIM), q_norm)
        k = rmsnorm(k.reshape(bsz, -1, WIN_KV_HEADS, HEAD_DIM), k_norm)
        if rotary:
            q, k = apply_rope(q, cos, sin), apply_rope(k, cos, sin)
        return q, k, v.reshape(bsz, -1, WIN_KV_HEADS, HEAD_DIM)

    sink_kg = sink.reshape(WIN_KV_HEADS, WIN_GROUP)
    q_l, k_l, v_l = project(h_lat, True)
    q_c, k_c, v_c = project(h_ctx, False)
    pad = ((0, 0), (WINDOW, WINDOW), (0, 0), (0, 0))
    k_pad, v_pad = jnp.pad(k_l, pad), jnp.pad(v_l, pad)
    nb = n // BLOCK

    def block(args):
        qb, bi = args
        start = bi * BLOCK
        kb = lax.dynamic_slice_in_dim(k_pad, start, SPAN, axis=1)
        vb = lax.dynamic_slice_in_dim(v_pad, start, SPAN, axis=1)
        s_win = jnp.einsum("bqkgd,bjkd->bkgqj", qb, kb, preferred_element_type=jnp.float32) * scale
        kpos = start - WINDOW + jnp.arange(SPAN)
        qpos = start + jnp.arange(BLOCK)
        valid = ((jnp.abs(kpos[None, :] - qpos[:, None]) <= WINDOW)
                 & (kpos[None, :] >= 0) & (kpos[None, :] < n))
        s_win = jnp.where(valid, s_win, NEG_INF)
        s_ctx = jnp.einsum("bqkgd,bjkd->bkgqj", qb, k_c, preferred_element_type=jnp.float32) * scale
        return _sink_attend([s_win, s_ctx], [vb, v_c], sink_kg)

    q_blocks = jnp.moveaxis(q_l.reshape(bsz, nb, BLOCK, WIN_KV_HEADS, WIN_GROUP, HEAD_DIM), 1, 0)
    o_blocks = lax.map(block, (q_blocks, jnp.arange(nb)))
    o_l = jnp.moveaxis(o_blocks, 0, 1)

    def out(o):
        return o.reshape(bsz, -1, D_MODEL).astype(h_lat.dtype) @ w_o

    if with_ctx:
        s_cc = jnp.einsum("bqkgd,bjkd->bkgqj", q_c, k_c, preferred_element_type=jnp.float32) * scale
        y_c = out(_sink_attend([s_cc], [v_c], sink_kg))
    else:
        y_c = None
    return out(o_l), y_c


def conformer_conv(h, w_pw1, b_pw1, dw_w, dw_b, ln_g, ln_b, w_pw2, b_pw2):
    a, g = jnp.split(h @ w_pw1 + b_pw1, 2, axis=-1)
    u = dwconv(a * jax.nn.sigmoid(g), dw_w) + dw_b
    u = jax.nn.silu(layernorm(u, ln_g, ln_b))
    return u @ w_pw2 + b_pw2


def swiglu(h, w_gate_up, w_down):
    g, u = jnp.split(h @ w_gate_up, 2, axis=-1)
    return (jax.nn.silu(g) * u) @ w_down


def setup_inputs(seed: int = 0) -> dict:
    key = jax.random.key(seed)
    keys = iter(jax.random.split(key, 48))

    def nrm(shape, std):
        return jax.random.normal(next(keys), shape, jnp.float32) * std

    def gain(shape):
        return 1.0 + nrm(shape, 0.05)

    d = D_MODEL
    n_a, n_b, n_c, n_d = (_n_layers_of(k) for k in range(N_MIXERS))
    return {
        "x": nrm((BATCH, SEQ, d), 1.0),
        "c": nrm((BATCH, d), 1.0),
        "ctx": nrm((BATCH, CTX_LEN, d), 1.0),
        "c_ctx": nrm((d,), 1.0),
        "ada_w": nrm((DEPTH, d, 6 * d), 0.5 * d ** -0.5),
        "ada_b": nrm((DEPTH, 6 * d), 0.02),
        "norm1": gain((DEPTH, d)),
        "norm2": gain((DEPTH, d)),
        "ffn_w_gate_up": nrm((DEPTH, d, 2 * D_FF), d ** -0.5),
        "ffn_w_down": nrm((DEPTH, D_FF, d), D_FF ** -0.5),
        "diff_w_qkv": nrm((n_a, d, 2 * DIFF_QK_WIDTH + DIFF_V_WIDTH), d ** -0.5),
        "diff_w_o": nrm((n_a, DIFF_V_WIDTH, d), DIFF_V_WIDTH ** -0.5),
        "diff_q_norm": gain((n_a, HEAD_DIM)),
        "diff_k_norm": gain((n_a, HEAD_DIM)),
        "diff_lam_q1": nrm((n_a, HEAD_DIM), 0.1),
        "diff_lam_k1": nrm((n_a, HEAD_DIM), 0.1),
        "diff_lam_q2": nrm((n_a, HEAD_DIM), 0.1),
        "diff_lam_k2": nrm((n_a, HEAD_DIM), 0.1),
        "diff_subln": gain((n_a, DIFF_V_DIM)),
        "sc_w_in": nrm((n_b, d, 3 * d), d ** -0.5),
        "sc_conv_w": nrm((n_b, SHORT_CONV_W, d), SHORT_CONV_W ** -0.5),
        "sc_w_out": nrm((n_b, d, d), d ** -0.5),
        "win_w_qkv": nrm((n_c, d, (WIN_Q_HEADS + 2 * WIN_KV_HEADS) * HEAD_DIM), d ** -0.5),
        "win_w_o": nrm((n_c, d, d), d ** -0.5),
        "win_q_norm": gain((n_c, HEAD_DIM)),
        "win_k_norm": gain((n_c, HEAD_DIM)),
        "win_sink": nrm((n_c, WIN_Q_HEADS), 0.5),
        "cf_w_pw1": nrm((n_d, d, 2 * d), d ** -0.5),
        "cf_b_pw1": nrm((n_d, 2 * d), 0.01),
        "cf_dw_w": nrm((n_d, CONF_CONV_W, d), CONF_CONV_W ** -0.5),
        "cf_dw_b": nrm((n_d, d), 0.01),
        "cf_ln_g": gain((n_d, d)),
        "cf_ln_b": nrm((n_d, d), 0.01),
        "cf_w_pw2": nrm((n_d, d, d), d ** -0.5),
        "cf_b_pw2": nrm((n_d, d), 0.01),
    }


def reference(x, c, ctx, c_ctx, ada_w, ada_b, norm1, norm2, ffn_w_gate_up, ffn_w_down,
              diff_w_qkv, diff_w_o, diff_q_norm, diff_k_norm, diff_lam_q1, diff_lam_k1,
              diff_lam_q2, diff_lam_k2, diff_subln,
              sc_w_in, sc_conv_w, sc_w_out,
              win_w_qkv, win_w_o, win_q_norm, win_k_norm, win_sink,
              cf_w_pw1, cf_b_pw1, cf_dw_w, cf_dw_b, cf_ln_g, cf_ln_b, cf_w_pw2, cf_b_pw2):
    n = x.shape[1]
    cos, sin = rope_tables(n)
    c_act = jax.nn.silu(c)
    cc_act = jax.nn.silu(c_ctx)
    h, hc = x, ctx
    for i in range(DEPTH):
        kind, j = i % N_MIXERS, i // N_MIXERS
        with_ctx = i < DEPTH - 1
        mod = (c_act @ ada_w[i] + ada_b[i])[:, None, :]
        mod_c = (cc_act @ ada_w[i] + ada_b[i])[None, None, :]
        sh1, sc1, g1, sh2, sc2, g2 = jnp.split(mod, 6, axis=-1)
        csh1, csc1, cg1, csh2, csc2, cg2 = jnp.split(mod_c, 6, axis=-1)
        a = rmsnorm(h, norm1[i]) * (1.0 + sc1) + sh1
        if with_ctx or kind in (0, 2):
            ac = rmsnorm(hc, norm1[i]) * (1.0 + csc1) + csh1
        if kind == 0:
            lam_init = 0.8 - 0.6 * math.exp(-0.3 * i)
            y, yc = diff_attention(a, ac, diff_w_qkv[j], diff_w_o[j], diff_q_norm[j], diff_k_norm[j],
                                   diff_lam_q1[j], diff_lam_k1[j], diff_lam_q2[j], diff_lam_k2[j],
                                   diff_subln[j], lam_init, cos, sin, with_ctx)
        elif kind == 1:
            y = short_conv(a, sc_w_in[j], sc_conv_w[j], sc_w_out[j])
            yc = short_conv(ac, sc_w_in[j], sc_conv_w[j], sc_w_out[j]) if with_ctx else None
        elif kind == 2:
            y, yc = window_gqa(a, ac, win_w_qkv[j], win_w_o[j], win_q_norm[j], win_k_norm[j],
                               win_sink[j], cos, sin, with_ctx)
        else:
            cf = (cf_w_pw1[j], cf_b_pw1[j], cf_dw_w[j], cf_dw_b[j], cf_ln_g[j], cf_ln_b[j],
                  cf_w_pw2[j], cf_b_pw2[j])
            y = conformer_conv(a, *cf)
            yc = conformer_conv(ac, *cf) if with_ctx else None
        h = h + g1 * y
        h = h + g2 * swiglu(rmsnorm(h, norm2[i]) * (1.0 + sc2) + sh2, ffn_w_gate_up[i], ffn_w_down[i])
        if with_ctx:
            hc = hc + cg1 * yc
            hc = hc + cg2 * swiglu(rmsnorm(hc, norm2[i]) * (1.0 + csc2) + csh2,
                                   ffn_w_gate_up[i], ffn_w_down[i])
    return h
```

```python
import functools
import math

import jax
import jax.numpy as jnp
import numpy as np
from jax import lax
from jax.experimental import pallas as pl
from jax.experimental.pallas import tpu as pltpu

F32 = jnp.float32
BF16 = jnp.bfloat16

D_MODEL = 1024
DEPTH = 4
N_MIXERS = 4
GRID_W = 64
NORM_EPS = 1e-6
ROPE_THETA = 10000.0
HEAD_DIM = 64
NEG_INF = -1e30
DIFF_HEADS = D_MODEL // (2 * HEAD_DIM)
SHORT_CONV_W = 3
WIN_Q_HEADS = D_MODEL // HEAD_DIM
WIN_KV_HEADS = 4
WIN_GROUP = WIN_Q_HEADS // WIN_KV_HEADS
WINDOW = 128
CONF_CONV_W = 31
D_FF = -(-(8 * D_MODEL) // (3 * 256)) * 256

LANES = 128
CONV_HALO = 16
VMEM_LIMIT_BYTES = 56 * 1024 * 1024

LOG2E = math.log2(math.e)
SH1, SC1, G1, SH2, SC2, G2 = range(6)


def _cparams(n_axes):
    return pltpu.CompilerParams(dimension_semantics=("parallel",) * n_axes,
                                vmem_limit_bytes=VMEM_LIMIT_BYTES)


def _const_spec(shape):
    nd = len(shape)
    return pl.BlockSpec(shape, lambda *_: (0,) * nd, pipeline_mode=pl.Buffered(1))


def _mod(mod_ref, row, idx):
    return mod_ref[row:row + 1, idx * D_MODEL:(idx + 1) * D_MODEL]


def _mod_norm(h, gain, scale, shift):
    ms = jnp.mean(h * h, axis=-1, keepdims=True)
    return (h * lax.rsqrt(ms + NORM_EPS)) * (gain * (1.0 + scale)) + shift


def _silu(x):
    return x * (1.0 / (1.0 + jnp.exp(-x)))


def _row_tile(n):
    return 512 if n % 512 == 0 else n


def _ada_kernel(act_ref, w_ref, b_ref, o_ref):
    a = _silu(act_ref[...])
    o_ref[...] = jnp.dot(a.astype(BF16), w_ref[...].astype(BF16),
                         preferred_element_type=F32) + b_ref[...]


def _ada_mod(c, c_ctx, ada_w, ada_b):
    act = jnp.zeros((8, D_MODEL), F32).at[0].set(c[0]).at[1].set(c_ctx)
    cols = 1536
    nc = 6 * D_MODEL // cols
    return pl.pallas_call(
        _ada_kernel,
        grid=(DEPTH, nc),
        in_specs=[pl.BlockSpec((8, D_MODEL), lambda i, j: (0, 0)),
                  pl.BlockSpec((None, D_MODEL, cols), lambda i, j: (i, 0, j)),
                  pl.BlockSpec((None, 1, cols), lambda i, j: (i, 0, j))],
        out_specs=pl.BlockSpec((None, 8, cols), lambda i, j: (i, 0, j)),
        out_shape=jax.ShapeDtypeStruct((DEPTH, 8, 6 * D_MODEL), F32),
        compiler_params=_cparams(2),
        name="ada_mod",
    )(act, ada_w, ada_b.reshape(DEPTH, 1, 6 * D_MODEL))


def _group_mean_matrix(width):
    g = np.arange(width) // HEAD_DIM
    return jnp.asarray((g[:, None] == g[None, :]).astype(np.float32) / HEAD_DIM, BF16)


def _rope_tables(n):
    n_rows = n // GRID_W
    rows = jnp.broadcast_to(jnp.arange(n_rows, dtype=F32)[:, None], (n_rows, GRID_W)).reshape(-1)
    cols = jnp.broadcast_to(jnp.arange(GRID_W, dtype=F32)[None, :], (n_rows, GRID_W)).reshape(-1)
    n_freq = HEAD_DIM // 4
    inv_freq = ROPE_THETA ** (-jnp.arange(n_freq, dtype=F32) / n_freq)
    ang_r = rows[:, None] * inv_freq[None, :]
    ang_c = cols[:, None] * inv_freq[None, :]
    ang = jnp.concatenate([ang_r, ang_r, ang_c, ang_c], axis=-1)
    cos, sin = jnp.cos(ang), jnp.sin(ang)
    first = (np.arange(HEAD_DIM) % (2 * n_freq)) < n_freq
    sin_up = jnp.where(first[None, :], -sin, 0.0)
    sin_dn = jnp.where(first[None, :], 0.0, sin)
    reps = LANES // HEAD_DIM
    return jnp.concatenate([jnp.tile(cos, (1, reps)), jnp.tile(sin_up, (1, reps)),
                            jnp.tile(sin_dn, (1, reps))], axis=-1)


def _rope(x, tab):
    w = x.shape[-1]
    reps = w // LANES
    cos = jnp.tile(tab[:, 0:LANES], (1, reps))
    sin_up = jnp.tile(tab[:, LANES:2 * LANES], (1, reps))
    sin_dn = jnp.tile(tab[:, 2 * LANES:3 * LANES], (1, reps))
    quarter = HEAD_DIM // 4
    return (x * cos + pltpu.roll(x, w - quarter, 1) * sin_up
            + pltpu.roll(x, quarter, 1) * sin_dn)


def _qkv_kernel(row, rotary, kw, q_scale, *refs):
    if rotary:
        h_ref, mod_ref, n1_ref, w_ref, gm_ref, qg_ref, kg_ref, tab_ref, q_ref, k_ref, v_ref = refs
    else:
        h_ref, mod_ref, n1_ref, w_ref, gm_ref, qg_ref, kg_ref, q_ref, k_ref, v_ref = refs
    a = _mod_norm(h_ref[...], n1_ref[...], _mod(mod_ref, row, SC1), _mod(mod_ref, row, SH1))
    y = jnp.dot(a.astype(BF16), w_ref[...], preferred_element_type=F32)

    def head_norm(x, gain, width):
        ms = jnp.dot((x * x).astype(BF16), gm_ref[0:width, 0:width], preferred_element_type=F32)
        return x * lax.rsqrt(ms + NORM_EPS) * gain

    q = head_norm(y[:, :D_MODEL], qg_ref[...], D_MODEL)
    k = head_norm(y[:, D_MODEL:D_MODEL + kw], kg_ref[...], kw)
    if rotary:
        tab = tab_ref[...]
        q, k = _rope(q, tab), _rope(k, tab)
    q_ref[...] = (q * q_scale).astype(BF16)
    k_ref[...] = k.astype(BF16)
    v_ref[...] = y[:, D_MODEL + kw:].astype(BF16)


def _qkv_proj(h, mods, layer, row, n1, w, q_gain, k_gain, kw, vw, q_scale, tab):
    n = h.shape[0]
    t = _row_tile(n)
    rotary = tab is not None
    wtot = D_MODEL + kw + vw
    in_specs = [pl.BlockSpec((t, D_MODEL), lambda i: (i, 0)),
                pl.BlockSpec((None, 8, 6 * D_MODEL), lambda i: (layer, 0, 0)),
                _const_spec((1, D_MODEL)),
                _const_spec((D_MODEL, wtot)),
                _const_spec((D_MODEL, D_MODEL)),
                _const_spec((1, D_MODEL)),
                _const_spec((1, kw))]
    args = [h, mods, n1.reshape(1, D_MODEL), w, _group_mean_matrix(D_MODEL),
            jnp.tile(q_gain, D_MODEL // HEAD_DIM).reshape(1, D_MODEL),
            jnp.tile(k_gain, kw // HEAD_DIM).reshape(1, kw)]
    if rotary:
        in_specs.append(pl.BlockSpec((t, 3 * LANES), lambda i: (i, 0)))
        args.append(tab)
    return pl.pallas_call(
        functools.partial(_qkv_kernel, row, rotary, kw, q_scale),
        grid=(n // t,),
        in_specs=in_specs,
        out_specs=[pl.BlockSpec((t, D_MODEL), lambda i: (i, 0)),
                   pl.BlockSpec((t, kw), lambda i: (i, 0)),
                   pl.BlockSpec((t, vw), lambda i: (i, 0))],
        out_shape=[jax.ShapeDtypeStruct((n, D_MODEL), BF16),
                   jax.ShapeDtypeStruct((n, kw), BF16),
                   jax.ShapeDtypeStruct((n, vw), BF16)],
        compiler_params=_cparams(1),
        name="qkv_proj",
    )(*args)


def _diff_attn_kernel(tq, chunks, lam_init, *refs):
    n_src = len(chunks)
    q_ref, lamv_ref, sub_ref = refs[:3]
    kv_refs = refs[3:3 + 2 * n_src]
    o_ref, m_ref, acc_ref = refs[3 + 2 * n_src:]

    q = q_ref[...]
    lane = lax.broadcasted_iota(jnp.int32, q.shape, 1)
    zero = jnp.zeros_like(q)
    qq = jnp.concatenate([jnp.where(lane < HEAD_DIM, q, zero),
                          jnp.where(lane >= HEAD_DIM, q, zero)], axis=0)
    m_ref[...] = jnp.full(m_ref.shape, NEG_INF, F32)
    acc_ref[...] = jnp.zeros(acc_ref.shape, F32)

    def step(kc, vc):
        tk = kc.shape[0]
        s = lax.dot_general(qq, kc, (((1,), (1,)), ((), ())), preferred_element_type=F32)
        m_prev = m_ref[...]
        m_new = jnp.maximum(m_prev, jnp.max(s, axis=-1, keepdims=True))
        alpha = jnp.exp2(m_prev - m_new)
        p = jnp.exp2(s - jnp.tile(m_new, (1, tk // LANES)))
        v_aug = jnp.concatenate([vc, jnp.ones((tk, LANES), BF16)], axis=1)
        pv = jnp.dot(p.astype(BF16), v_aug, preferred_element_type=F32)
        acc_ref[...] = acc_ref[...] * jnp.tile(alpha, (1, 2)) + pv
        m_ref[...] = m_new

    for s_idx, (tk, n_chunks) in enumerate(chunks):
        k_ref, v_ref = kv_refs[2 * s_idx], kv_refs[2 * s_idx + 1]
        if n_chunks == 1:
            step(k_ref[...], v_ref[...])
        else:
            def body(c, carry, k_ref=k_ref, v_ref=v_ref, tk=tk):
                start = pl.multiple_of(c * tk, tk)
                step(k_ref[pl.ds(start, tk), :], v_ref[pl.ds(start, tk), :])
                return carry
            lax.fori_loop(0, n_chunks, body, 0)

    acc = acc_ref[...]
    o = acc[:, :LANES] / acc[:, LANES:LANES + 1]
    lv = lamv_ref[...]
    lam = (jnp.exp(jnp.sum(lv[0:1] * lv[1:2], axis=-1, keepdims=True))
           - jnp.exp(jnp.sum(lv[2:3] * lv[3:4], axis=-1, keepdims=True)) + lam_init)
    d = o[:tq] - lam * o[tq:]
    ms = jnp.mean(d * d, axis=-1, keepdims=True)
    o_ref[...] = (d * lax.rsqrt(ms + NORM_EPS) * (sub_ref[...] * (1.0 - lam_init))).astype(BF16)


def _diff_attn(q, kv_sources, lamv, subln, lam_init, tk=512):
    nq = q.shape[0]
    tq = min(256, nq)
    chunks = []
    in_specs = [pl.BlockSpec((tq, LANES), lambda h, i: (i, h)),
                _const_spec((8, LANES)),
                _const_spec((1, LANES))]
    args = [q, lamv, subln.reshape(1, LANES)]
    for k, v in kv_sources:
        nk = k.shape[0]
        ck = min(tk, nk)
        chunks.append((ck, nk // ck))
        in_specs += [pl.BlockSpec((nk, LANES), lambda h, i: (0, h)),
                     pl.BlockSpec((nk, LANES), lambda h, i: (0, h))]
        args += [k, v]
    return pl.pallas_call(
        functools.partial(_diff_attn_kernel, tq, tuple(chunks), lam_init),
        grid=(DIFF_HEADS, nq // tq),
        in_specs=in_specs,
        out_specs=pl.BlockSpec((tq, LANES), lambda h, i: (i, h)),
        out_shape=jax.ShapeDtypeStruct((nq, D_MODEL), BF16),
        scratch_shapes=[pltpu.VMEM((2 * tq, LANES), F32), pltpu.VMEM((2 * tq, 2 * LANES), F32)],
        compiler_params=_cparams(2),
        name="diff_attn",
    )(*args)


def _win_attn_kernel(tq, n_lat, with_window, *refs):
    if with_window:
        (q_ref, kp_ref, kc_ref, kn_ref, vp_ref, vc_ref, vn_ref,
         kx_ref, vx_ref, sink_ref, o_ref) = refs
    else:
        q_ref, kx_ref, vx_ref, sink_ref, o_ref = refs
    i = pl.program_id(0)
    n_ctx = kx_ref.shape[0]
    if with_window:
        span = tq + 2 * WINDOW
        qpos = i * tq + lax.broadcasted_iota(jnp.int32, (tq, span), 0)
        kpos = i * tq - WINDOW + lax.broadcasted_iota(jnp.int32, (tq, span), 1)
        valid = (jnp.abs(kpos - qpos) <= WINDOW) & (kpos >= 0) & (kpos < n_lat)
    lane = lax.broadcasted_iota(jnp.int32, (tq, LANES), 1)
    low = lane < HEAD_DIM
    for pair in range(WIN_Q_HEADS // 2):
        kvh = pair // (WIN_GROUP // 2)
        qs = slice(pair * LANES, (pair + 1) * LANES)
        ks = slice(kvh * LANES, (kvh + 1) * LANES)
        qp = q_ref[:, qs]
        zero = jnp.zeros_like(qp)
        kx, vx = kx_ref[:, ks], vx_ref[:, ks]
        if with_window:
            kw = jnp.concatenate([kp_ref[:, ks], kc_ref[:, ks], kn_ref[:, ks]], axis=0)
            vw = jnp.concatenate([vp_ref[:, ks], vc_ref[:, ks], vn_ref[:, ks]], axis=0)
        outs = []
        for half in range(2):
            head = 2 * pair + half
            qh = jnp.where(low if half == 0 else ~low, qp, zero)
            sink = sink_ref[head:head + 1, 0:1]
            s_x = lax.dot_general(qh, kx, (((1,), (1,)), ((), ())), preferred_element_type=F32)
            m = jnp.maximum(jnp.max(s_x, axis=-1, keepdims=True), sink)
            if with_window:
                s_w = lax.dot_general(qh, kw, (((1,), (1,)), ((), ())),
                                      preferred_element_type=F32)
                s_w = jnp.where(valid, s_w, NEG_INF)
                m = jnp.maximum(m, jnp.max(s_w, axis=-1, keepdims=True))
            p_x = jnp.exp(s_x - m)
            den = jnp.sum(p_x, axis=-1, keepdims=True) + jnp.exp(sink - m)
            o = jnp.dot(p_x.astype(BF16), vx, preferred_element_type=F32)
            if with_window:
                p_w = jnp.exp(s_w - m)
                den = den + jnp.sum(p_w, axis=-1, keepdims=True)
                o = o + jnp.dot(p_w.astype(BF16), vw, preferred_element_type=F32)
            outs.append(o / den)
        o_ref[:, qs] = jnp.where(low, outs[0], outs[1]).astype(BF16)


def _win_attn(q, k, v, kx, vx, sink_rows, with_window):
    nq = q.shape[0]
    tq = min(256, nq)
    n_ctx = kx.shape[0]
    kvw = WIN_KV_HEADS * LANES
    in_specs = [pl.BlockSpec((tq, D_MODEL), lambda i: (i, 0))]
    args = [q]
    if with_window:
        r = tq // WINDOW
        last = nq // WINDOW - 1
        prev_spec = pl.BlockSpec((WINDOW, kvw), lambda i: (jnp.maximum(i * r - 1, 0), 0))
        cur_spec = pl.BlockSpec((tq, kvw), lambda i: (i, 0))
        next_spec = pl.BlockSpec((WINDOW, kvw), lambda i: (jnp.minimum((i + 1) * r, last), 0))
        in_specs += [prev_spec, cur_spec, next_spec] * 2
        args += [k, k, k, v, v, v]
    in_specs += [_const_spec((n_ctx, kvw)), _const_spec((n_ctx, kvw)),
                 _const_spec((WIN_Q_HEADS, LANES))]
    args += [kx, vx, sink_rows]
    return pl.pallas_call(
        functools.partial(_win_attn_kernel, tq, nq, with_window),
        grid=(nq // tq,),
        in_specs=in_specs,
        out_specs=pl.BlockSpec((tq, D_MODEL), lambda i: (i, 0)),
        out_shape=jax.ShapeDtypeStruct((nq, D_MODEL), BF16),
        compiler_params=_cparams(1),
        name="win_attn",
    )(*args)


def _out_res_kernel(row, has_bias, *refs):
    if has_bias:
        x_ref, h_ref, mod_ref, w_ref, b_ref, o_ref = refs
    else:
        x_ref, h_ref, mod_ref, w_ref, o_ref = refs
    y = jnp.dot(x_ref[...], w_ref[...], preferred_element_type=F32)
    if has_bias:
        y = y + b_ref[...]
    o_ref[...] = h_ref[...] + _mod(mod_ref, row, G1) * y


def _out_res(x, h, mods, layer, row, w, bias=None):
    n = h.shape[0]
    t = _row_tile(n)
    in_specs = [pl.BlockSpec((t, D_MODEL), lambda i: (i, 0)),
                pl.BlockSpec((t, D_MODEL), lambda i: (i, 0)),
                pl.BlockSpec((None, 8, 6 * D_MODEL), lambda i: (layer, 0, 0)),
                _const_spec((D_MODEL, D_MODEL))]
    args = [x, h, mods, w]
    if bias is not None:
        in_specs.append(_const_spec((1, D_MODEL)))
        args.append(bias.reshape(1, D_MODEL))
    return pl.pallas_call(
        functools.partial(_out_res_kernel, row, bias is not None),
        grid=(n // t,),
        in_specs=in_specs,
        out_specs=pl.BlockSpec((t, D_MODEL), lambda i: (i, 0)),
        out_shape=jax.ShapeDtypeStruct((n, D_MODEL), F32),
        compiler_params=_cparams(1),
        name="out_res",
    )(*args)


FFN_CHUNK = D_FF // 2


def _ffn_kernel(row, h_ref, mod_ref, n2_ref, wgu_ref, wd_ref, o_ref):
    h = h_ref[...]
    a = _mod_norm(h, n2_ref[...], _mod(mod_ref, row, SC2), _mod(mod_ref, row, SH2)).astype(BF16)
    acc = None
    for c in range(D_FF // FFN_CHUNK):
        lo = c * FFN_CHUNK
        g = jnp.dot(a, wgu_ref[:, lo:lo + FFN_CHUNK], preferred_element_type=F32)
        u = jnp.dot(a, wgu_ref[:, D_FF + lo:D_FF + lo + FFN_CHUNK], preferred_element_type=F32)
        act = (_silu(g) * u).astype(BF16)
        part = jnp.dot(act, wd_ref[lo:lo + FFN_CHUNK, :], preferred_element_type=F32)
        acc = part if acc is None else acc + part
    o_ref[...] = h + _mod(mod_ref, row, G2) * acc


def _ffn(h, mods, layer, row, n2, wgu, wd):
    n = h.shape[0]
    t = _row_tile(n)
    return pl.pallas_call(
        functools.partial(_ffn_kernel, row),
        grid=(n // t,),
        in_specs=[pl.BlockSpec((t, D_MODEL), lambda i: (i, 0)),
                  pl.BlockSpec((None, 8, 6 * D_MODEL), lambda i: (layer, 0, 0)),
                  _const_spec((1, D_MODEL)),
                  _const_spec((D_MODEL, 2 * D_FF)),
                  _const_spec((D_FF, D_MODEL))],
        out_specs=pl.BlockSpec((t, D_MODEL), lambda i: (i, 0)),
        out_shape=jax.ShapeDtypeStruct((n, D_MODEL), F32),
        compiler_params=_cparams(1),
        name="ffn",
    )(h, mods, n2.reshape(1, D_MODEL), wgu, wd)


def _in_proj_kernel(row, mode, *refs):
    if mode == "short":
        h_ref, mod_ref, n1_ref, w_ref, b_out, cu_out = refs
    else:
        h_ref, mod_ref, n1_ref, w_ref, bias_ref, glu_out = refs
    a = _mod_norm(h_ref[...], n1_ref[...], _mod(mod_ref, row, SC1), _mod(mod_ref, row, SH1))
    y = jnp.dot(a.astype(BF16), w_ref[...], preferred_element_type=F32)
    if mode == "short":
        b_out[...] = y[:, :D_MODEL]
        cu_out[...] = y[:, D_MODEL:2 * D_MODEL] * y[:, 2 * D_MODEL:]
    else:
        y = y + bias_ref[...]
        g = y[:, D_MODEL:]
        glu_out[...] = y[:, :D_MODEL] * (1.0 / (1.0 + jnp.exp(-g)))


def _in_proj(h, mods, layer, row, n1, w, mode, bias=None):
    n = h.shape[0]
    t = _row_tile(n)
    wtot = w.shape[1]
    in_specs = [pl.BlockSpec((t, D_MODEL), lambda i: (i, 0)),
                pl.BlockSpec((None, 8, 6 * D_MODEL), lambda i: (layer, 0, 0)),
                _const_spec((1, D_MODEL)),
                _const_spec((D_MODEL, wtot))]
    args = [h, mods, n1.reshape(1, D_MODEL), w]
    row_spec = pl.BlockSpec((t, D_MODEL), lambda i: (i, 0))
    row_shape = jax.ShapeDtypeStruct((n, D_MODEL), F32)
    if mode == "short":
        out_specs, out_shape = [row_spec, row_spec], [row_shape, row_shape]
    else:
        in_specs.append(_const_spec((1, wtot)))
        args.append(bias.reshape(1, wtot))
        out_specs, out_shape = row_spec, row_shape
    return pl.pallas_call(
        functools.partial(_in_proj_kernel, row, mode),
        grid=(n // t,),
        in_specs=in_specs,
        out_specs=out_specs,
        out_shape=out_shape,
        compiler_params=_cparams(1),
        name="in_proj_" + mode,
    )(*args)


def _dwconv_tile(xs_ref, prev_ref, cur_ref, next_ref, w_ref, taps, t):
    i = pl.program_id(0)
    last = pl.num_programs(0) - 1
    xs_ref[0:CONV_HALO, :] = jnp.where(i > 0, prev_ref[...], 0.0)
    xs_ref[CONV_HALO:CONV_HALO + t, :] = cur_ref[...]
    xs_ref[CONV_HALO + t:, :] = jnp.where(i < last, next_ref[...], 0.0)
    base = CONV_HALO - taps // 2
    acc = None
    for k in range(taps):
        term = xs_ref[base + k:base + k + t, :] * w_ref[k:k + 1, :]
        acc = term if acc is None else acc + term
    return acc


def _conv_out_kernel(row, mode, taps, t, *refs):
    if mode == "short":
        (prev_ref, cur_ref, next_ref, cw_ref, b_ref, h_ref, mod_ref, w_ref, o_ref, xs_ref) = refs
    else:
        (prev_ref, cur_ref, next_ref, cw_ref, dwb_ref, lng_ref, lnb_ref, h_ref, mod_ref,
         w_ref, pb_ref, o_ref, xs_ref) = refs
    u = _dwconv_tile(xs_ref, prev_ref, cur_ref, next_ref, cw_ref, taps, t)
    if mode == "short":
        y = jnp.dot((b_ref[...] * u).astype(BF16), w_ref[...], preferred_element_type=F32)
    else:
        u = u + dwb_ref[...]
        mu = jnp.mean(u, axis=-1, keepdims=True)
        uc = u - mu
        var = jnp.mean(uc * uc, axis=-1, keepdims=True)
        z = _silu(uc * lax.rsqrt(var + NORM_EPS) * lng_ref[...] + lnb_ref[...])
        y = jnp.dot(z.astype(BF16), w_ref[...], preferred_element_type=F32) + pb_ref[...]
    o_ref[...] = h_ref[...] + _mod(mod_ref, row, G1) * y


def _conv_out(x, h, mods, layer, row, conv_w, w, mode, extra):
    n = h.shape[0]
    t = _row_tile(n)
    taps = conv_w.shape[0]
    r = t // CONV_HALO
    last = n // CONV_HALO - 1
    taps_pad = -(-taps // 8) * 8
    cw = jnp.zeros((taps_pad, D_MODEL), F32).at[:taps].set(conv_w)
    row_spec = pl.BlockSpec((t, D_MODEL), lambda i: (i, 0))
    vec_spec = _const_spec((1, D_MODEL))
    in_specs = [pl.BlockSpec((CONV_HALO, D_MODEL), lambda i: (jnp.maximum(i * r - 1, 0), 0)),
                row_spec,
                pl.BlockSpec((CONV_HALO, D_MODEL), lambda i: (jnp.minimum((i + 1) * r, last), 0)),
                _const_spec((taps_pad, D_MODEL))]
    args = [x, x, x, cw]
    mod_spec = pl.BlockSpec((None, 8, 6 * D_MODEL), lambda i: (layer, 0, 0))
    if mode == "short":
        (b_gate,) = extra
        in_specs += [row_spec, row_spec, mod_spec, _const_spec((D_MODEL, D_MODEL))]
        args += [b_gate, h, mods, w]
    else:
        dw_b, ln_g, ln_b, pw_b = extra
        in_specs += [vec_spec, vec_spec, vec_spec, row_spec, mod_spec,
                     _const_spec((D_MODEL, D_MODEL)), vec_spec]
        args += [dw_b.reshape(1, D_MODEL), ln_g.reshape(1, D_MODEL), ln_b.reshape(1, D_MODEL),
                 h, mods, w, pw_b.reshape(1, D_MODEL)]
    return pl.pallas_call(
        functools.partial(_conv_out_kernel, row, mode, taps, t),
        grid=(n // t,),
        in_specs=in_specs,
        out_specs=row_spec,
        out_shape=jax.ShapeDtypeStruct((n, D_MODEL), F32),
        scratch_shapes=[pltpu.VMEM((t + 2 * CONV_HALO, D_MODEL), F32)],
        compiler_params=_cparams(1),
        name="conv_out_" + mode,
    )(*args)


def _diff_qkv_weight(w_qkv):
    idx = np.arange(D_MODEL).reshape(2, DIFF_HEADS, HEAD_DIM).transpose(1, 0, 2).reshape(-1)
    cols = np.concatenate([idx, D_MODEL + idx, 2 * D_MODEL + np.arange(D_MODEL)])
    return w_qkv[:, cols].astype(BF16)


def _win_qkv_weight(w_qkv):
    qw = WIN_Q_HEADS * HEAD_DIM
    kvw = WIN_KV_HEADS * HEAD_DIM
    dup = np.repeat(np.arange(kvw).reshape(WIN_KV_HEADS, 1, HEAD_DIM), 2, axis=1).reshape(-1)
    cols = np.concatenate([np.arange(qw), qw + dup, qw + kvw + dup])
    return w_qkv[:, cols].astype(BF16)


def kernel(x, c, ctx, c_ctx, ada_w, ada_b, norm1, norm2, ffn_w_gate_up, ffn_w_down, diff_w_qkv, diff_w_o, diff_q_norm, diff_k_norm, diff_lam_q1, diff_lam_k1, diff_lam_q2, diff_lam_k2, diff_subln, sc_w_in, sc_conv_w, sc_w_out, win_w_qkv, win_w_o, win_q_norm, win_k_norm, win_sink, cf_w_pw1, cf_b_pw1, cf_dw_w, cf_dw_b, cf_ln_g, cf_ln_b, cf_w_pw2, cf_b_pw2):
    n = x.shape[1]
    h, hc = x[0], ctx[0]
    mods = _ada_mod(c, c_ctx, ada_w, ada_b)
    tab = _rope_tables(n)
    for i in range(DEPTH):
        kind, j = i % N_MIXERS, i // N_MIXERS
        with_ctx = i < DEPTH - 1
        if kind == 0:
            lam_init = 0.8 - 0.6 * math.exp(-0.3 * i)
            w = _diff_qkv_weight(diff_w_qkv[j])
            w_o = diff_w_o[j].astype(BF16)
            lamv = jnp.zeros((8, LANES), F32).at[0:4, 0:HEAD_DIM].set(
                jnp.stack([diff_lam_q1[j], diff_lam_k1[j], diff_lam_q2[j], diff_lam_k2[j]]))
            q_scale = HEAD_DIM ** -0.5 * LOG2E
            proj = functools.partial(_qkv_proj, mods=mods, layer=i, n1=norm1[i], w=w,
                                     q_gain=diff_q_norm[j], k_gain=diff_k_norm[j],
                                     kw=D_MODEL, vw=D_MODEL, q_scale=q_scale)
            q_l, k_l, v_l = proj(h, row=0, tab=tab)
            q_c, k_c, v_c = proj(hc, row=1, tab=None)
            o_l = _diff_attn(q_l, [(k_l, v_l), (k_c, v_c)], lamv, diff_subln[j], lam_init)
            h = _out_res(o_l, h, mods, i, 0, w_o)
            if with_ctx:
                o_c = _diff_attn(q_c, [(k_c, v_c)], lamv, diff_subln[j], lam_init)
                hc = _out_res(o_c, hc, mods, i, 1, w_o)
        elif kind == 1:
            w_in = sc_w_in[j].astype(BF16)
            w_out = sc_w_out[j].astype(BF16)
            b_l, cu_l = _in_proj(h, mods, i, 0, norm1[i], w_in, "short")
            h = _conv_out(cu_l, h, mods, i, 0, sc_conv_w[j], w_out, "short", (b_l,))
            if with_ctx:
                b_c, cu_c = _in_proj(hc, mods, i, 1, norm1[i], w_in, "short")
                hc = _conv_out(cu_c, hc, mods, i, 1, sc_conv_w[j], w_out, "short", (b_c,))
        elif kind == 2:
            w = _win_qkv_weight(win_w_qkv[j])
            w_o = win_w_o[j].astype(BF16)
            kvw = WIN_KV_HEADS * LANES
            sink_rows = jnp.broadcast_to(win_sink[j][:, None], (WIN_Q_HEADS, LANES))
            proj = functools.partial(_qkv_proj, mods=mods, layer=i, n1=norm1[i], w=w,
                                     q_gain=win_q_norm[j], k_gain=win_k_norm[j],
                                     kw=kvw, vw=kvw, q_scale=HEAD_DIM ** -0.5)
            q_l, k_l, v_l = proj(h, row=0, tab=tab)
            q_c, k_c, v_c = proj(hc, row=1, tab=None)
            o_l = _win_attn(q_l, k_l, v_l, k_c, v_c, sink_rows, True)
            h = _out_res(o_l, h, mods, i, 0, w_o)
            if with_ctx:
                o_c = _win_attn(q_c, None, None, k_c, v_c, sink_rows, False)
                hc = _out_res(o_c, hc, mods, i, 1, w_o)
        else:
            w1 = cf_w_pw1[j].astype(BF16)
            w2 = cf_w_pw2[j].astype(BF16)
            extra = (cf_dw_b[j], cf_ln_g[j], cf_ln_b[j], cf_b_pw2[j])
            glu_l = _in_proj(h, mods, i, 0, norm1[i], w1, "glu", cf_b_pw1[j])
            h = _conv_out(glu_l, h, mods, i, 0, cf_dw_w[j], w2, "conf", extra)
            if with_ctx:
                glu_c = _in_proj(hc, mods, i, 1, norm1[i], w1, "glu", cf_b_pw1[j])
                hc = _conv_out(glu_c, hc, mods, i, 1, cf_dw_w[j], w2, "conf", extra)
        wgu = ffn_w_gate_up[i].astype(BF16)
        wd = ffn_w_down[i].astype(BF16)
        h = _ffn(h, mods, i, 0, norm2[i], wgu, wd)
        if with_ctx:
            hc = _ffn(hc, mods, i, 1, norm2[i], wgu, wd)
    return h[None]
```

```python
import functools
import math

import jax
import jax.numpy as jnp
import numpy as np
from jax import lax
from jax.experimental import pallas as pl
from jax.experimental.pallas import tpu as pltpu

F32 = jnp.float32
BF16 = jnp.bfloat16

D_MODEL = 1024
DEPTH = 4
N_MIXERS = 4
GRID_W = 64
NORM_EPS = 1e-6
ROPE_THETA = 10000.0
HEAD_DIM = 64
NEG_INF = -1e30
DIFF_HEADS = D_MODEL // (2 * HEAD_DIM)
SHORT_CONV_W = 3
WIN_Q_HEADS = D_MODEL // HEAD_DIM
WIN_KV_HEADS = 4
WIN_GROUP = WIN_Q_HEADS // WIN_KV_HEADS
WINDOW = 128
CONF_CONV_W = 31
D_FF = -(-(8 * D_MODEL) // (3 * 256)) * 256

LANES = 128
CONV_HALO = 16
VMEM_LIMIT_BYTES = 56 * 1024 * 1024

LOG2E = math.log2(math.e)
SH1, SC1, G1, SH2, SC2, G2 = range(6)


def _cparams(n_axes):
    return pltpu.CompilerParams(dimension_semantics=("parallel",) * n_axes,
                                vmem_limit_bytes=VMEM_LIMIT_BYTES)


def _const_spec(shape):
    nd = len(shape)
    return pl.BlockSpec(shape, lambda *_: (0,) * nd, pipeline_mode=pl.Buffered(1))


def _mod(mod_ref, row, idx):
    return mod_ref[row:row + 1, idx * D_MODEL:(idx + 1) * D_MODEL]


def _mod_norm(h, gain, scale, shift):
    ms = jnp.mean(h * h, axis=-1, keepdims=True)
    return (h * lax.rsqrt(ms + NORM_EPS)) * (gain * (1.0 + scale)) + shift


def _silu(x):
    return x * (1.0 / (1.0 + jnp.exp(-x)))


def _row_tile(n):
    return 512 if n % 512 == 0 else n


def _ada_kernel(act_ref, w_ref, b_ref, o_ref):
    a = _silu(act_ref[...])
    o_ref[...] = jnp.dot(a.astype(BF16), w_ref[...].astype(BF16),
                         preferred_element_type=F32) + b_ref[...]


def _ada_mod(c, c_ctx, ada_w, ada_b):
    act = jnp.zeros((8, D_MODEL), F32).at[0].set(c[0]).at[1].set(c_ctx)
    cols = 1536
    nc = 6 * D_MODEL // cols
    return pl.pallas_call(
        _ada_kernel,
        grid=(DEPTH, nc),
        in_specs=[pl.BlockSpec((8, D_MODEL), lambda i, j: (0, 0)),
                  pl.BlockSpec((None, D_MODEL, cols), lambda i, j: (i, 0, j)),
                  pl.BlockSpec((None, 1, cols), lambda i, j: (i, 0, j))],
        out_specs=pl.BlockSpec((None, 8, cols), lambda i, j: (i, 0, j)),
        out_shape=jax.ShapeDtypeStruct((DEPTH, 8, 6 * D_MODEL), F32),
        compiler_params=_cparams(2),
        name="ada_mod",
    )(act, ada_w, ada_b.reshape(DEPTH, 1, 6 * D_MODEL))


def _group_mean_matrix(width):
    g = np.arange(width) // HEAD_DIM
    return jnp.asarray((g[:, None] == g[None, :]).astype(np.float32) / HEAD_DIM, BF16)


def _rope_tables(n):
    n_rows = n // GRID_W
    rows = jnp.broadcast_to(jnp.arange(n_rows, dtype=F32)[:, None], (n_rows, GRID_W)).reshape(-1)
    cols = jnp.broadcast_to(jnp.arange(GRID_W, dtype=F32)[None, :], (n_rows, GRID_W)).reshape(-1)
    n_freq = HEAD_DIM // 4
    inv_freq = ROPE_THETA ** (-jnp.arange(n_freq, dtype=F32) / n_freq)
    ang_r = rows[:, None] * inv_freq[None, :]
    ang_c = cols[:, None] * inv_freq[None, :]
    ang = jnp.concatenate([ang_r, ang_r, ang_c, ang_c], axis=-1)
    cos, sin = jnp.cos(ang), jnp.sin(ang)
    first = (np.arange(HEAD_DIM) % (2 * n_freq)) < n_freq
    sin_up = jnp.where(first[None, :], -sin, 0.0)
    sin_dn = jnp.where(first[None, :], 0.0, sin)
    reps = LANES // HEAD_DIM
    return jnp.concatenate([jnp.tile(cos, (1, reps)), jnp.tile(sin_up, (1, reps)),
                            jnp.tile(sin_dn, (1, reps))], axis=-1)


def _rope(x, tab):
    w = x.shape[-1]
    reps = w // LANES
    cos = jnp.tile(tab[:, 0:LANES], (1, reps))
    sin_up = jnp.tile(tab[:, LANES:2 * LANES], (1, reps))
    sin_dn = jnp.tile(tab[:, 2 * LANES:3 * LANES], (1, reps))
    quarter = HEAD_DIM // 4
    return (x * cos + pltpu.roll(x, w - quarter, 1) * sin_up
            + pltpu.roll(x, quarter, 1) * sin_dn)


def _qkv_kernel(row, rotary, kw, q_scale, n_alias, *refs):
    refs = list(refs)
    del refs[-3 - n_alias:-3]
    if rotary:
        h_ref, mod_ref, n1_ref, w_ref, gm_ref, qg_ref, kg_ref, tab_ref, q_ref, k_ref, v_ref = refs
    else:
        h_ref, mod_ref, n1_ref, w_ref, gm_ref, qg_ref, kg_ref, q_ref, k_ref, v_ref = refs
    a = _mod_norm(h_ref[...], n1_ref[...], _mod(mod_ref, row, SC1), _mod(mod_ref, row, SH1))
    y = jnp.dot(a.astype(BF16), w_ref[...], preferred_element_type=F32)

    def head_norm(x, gain, width):
        ms = jnp.dot((x * x).astype(BF16), gm_ref[0:width, 0:width], preferred_element_type=F32)
        return x * lax.rsqrt(ms + NORM_EPS) * gain

    q = head_norm(y[:, :D_MODEL], qg_ref[...], D_MODEL)
    k = head_norm(y[:, D_MODEL:D_MODEL + kw], kg_ref[...], kw)
    if rotary:
        tab = tab_ref[...]
        q, k = _rope(q, tab), _rope(k, tab)
    q_ref[...] = (q * q_scale).astype(BF16)
    k_ref[...] = k.astype(BF16)
    v_ref[...] = y[:, D_MODEL + kw:].astype(BF16)


def _qkv_proj(h, mods, layer, row, n1, w, q_gain, k_gain, kw, vw, q_scale, tab,
              kv_rows=None, kv_into=None, kv_row_offset=0):
    n = h.shape[0]
    t = _row_tile(n)
    rotary = tab is not None
    wtot = D_MODEL + kw + vw
    in_specs = [pl.BlockSpec((t, D_MODEL), lambda i: (i, 0)),
                pl.BlockSpec((None, 8, 6 * D_MODEL), lambda i: (layer, 0, 0)),
                _const_spec((1, D_MODEL)),
                _const_spec((D_MODEL, wtot)),
                _const_spec((D_MODEL, D_MODEL)),
                _const_spec((1, D_MODEL)),
                _const_spec((1, kw))]
    args = [h, mods, n1.reshape(1, D_MODEL), w, _group_mean_matrix(D_MODEL),
            jnp.tile(q_gain, D_MODEL // HEAD_DIM).reshape(1, D_MODEL),
            jnp.tile(k_gain, kw // HEAD_DIM).reshape(1, kw)]
    if rotary:
        in_specs.append(pl.BlockSpec((t, 3 * LANES), lambda i: (i, 0)))
        args.append(tab)
    aliases = {}
    off = kv_row_offset // t
    kv_total = n if kv_rows is None else kv_rows
    if kv_into is not None:
        kv_total = kv_into[0].shape[0]
        aliases = {len(args): 1, len(args) + 1: 2}
        in_specs += [pl.BlockSpec(memory_space=pl.ANY)] * 2
        args += list(kv_into)
    return pl.pallas_call(
        functools.partial(_qkv_kernel, row, rotary, kw, q_scale, len(aliases)),
        grid=(n // t,),
        in_specs=in_specs,
        out_specs=[pl.BlockSpec((t, D_MODEL), lambda i: (i, 0)),
                   pl.BlockSpec((t, kw), lambda i: (i + off, 0)),
                   pl.BlockSpec((t, vw), lambda i: (i + off, 0))],
        out_shape=[jax.ShapeDtypeStruct((n, D_MODEL), BF16),
                   jax.ShapeDtypeStruct((kv_total, kw), BF16),
                   jax.ShapeDtypeStruct((kv_total, vw), BF16)],
        input_output_aliases=aliases,
        compiler_params=_cparams(1),
        name="qkv_proj",
    )(*args)


DIFF_TQ = 256
DIFF_TK = 1280


def _diff_attn_kernel(tq, tk, n_chunks, lam_init, q_ref, lamv_ref, sub_ref, k_ref, v_ref,
                      o_ref, qq_ref, sa_ref, sb_ref, m_ref, acc_ref):
    q = q_ref[...]
    lane = lax.broadcasted_iota(jnp.int32, q.shape, 1)
    zero = jnp.zeros_like(q)
    qq_ref[0:tq, :] = jnp.where(lane < HEAD_DIM, q, zero)
    qq_ref[tq:, :] = jnp.where(lane >= HEAD_DIM, q, zero)
    m_ref[...] = jnp.full(m_ref.shape, NEG_INF, F32)
    acc_ref[...] = jnp.zeros(acc_ref.shape, F32)

    def chunk(ref, c):
        if isinstance(c, int):
            return ref[c * tk:(c + 1) * tk, :]
        return ref[pl.ds(pl.multiple_of(c * tk, tk), tk), :]

    def scores(c, s_ref):
        s_ref[...] = lax.dot_general(qq_ref[...], chunk(k_ref, c), (((1,), (1,)), ((), ())),
                                     preferred_element_type=F32)

    def softmax_pv(c, s_ref):
        s = s_ref[...]
        m_prev = m_ref[...]
        m_new = jnp.maximum(m_prev, jnp.max(s, axis=-1, keepdims=True))
        alpha = jnp.exp2(m_prev - m_new)
        p = jnp.exp2(s - jnp.tile(m_new, (1, tk // LANES)))
        v_aug = jnp.concatenate([chunk(v_ref, c), jnp.ones((tk, LANES), BF16)], axis=1)
        pv = jnp.dot(p.astype(BF16), v_aug, preferred_element_type=F32)
        acc_ref[...] = acc_ref[...] * jnp.tile(alpha, (1, 2)) + pv
        m_ref[...] = m_new

    scores(0, sa_ref)

    def body(j, carry):
        scores(2 * j + 1, sb_ref)
        softmax_pv(2 * j, sa_ref)
        scores(2 * j + 2, sa_ref)
        softmax_pv(2 * j + 1, sb_ref)
        return carry

    if n_chunks > 1:
        lax.fori_loop(0, (n_chunks - 1) // 2, body, 0)
    softmax_pv(n_chunks - 1, sa_ref)

    acc = acc_ref[...]
    o = acc[:, :LANES] / acc[:, LANES:LANES + 1]
    lv = lamv_ref[...]
    lam = (jnp.exp(jnp.sum(lv[0:1] * lv[1:2], axis=-1, keepdims=True))
           - jnp.exp(jnp.sum(lv[2:3] * lv[3:4], axis=-1, keepdims=True)) + lam_init)
    d = o[:tq] - lam * o[tq:]
    ms = jnp.mean(d * d, axis=-1, keepdims=True)
    o_ref[...] = (d * lax.rsqrt(ms + NORM_EPS) * (sub_ref[...] * (1.0 - lam_init))).astype(BF16)


def _diff_attn(q, k, v, key_start, n_keys, lamv, subln, lam_init):
    nq = q.shape[0]
    tq = min(DIFF_TQ, nq)
    tk = min(DIFF_TK, n_keys)
    n_chunks = n_keys // tk
    assert n_keys % tk == 0 and n_chunks % 2 == 1 and key_start % n_keys == 0
    kb = key_start // n_keys
    return pl.pallas_call(
        functools.partial(_diff_attn_kernel, tq, tk, n_chunks, lam_init),
        grid=(DIFF_HEADS, nq // tq),
        in_specs=[pl.BlockSpec((tq, LANES), lambda h, i: (i, h)),
                  _const_spec((8, LANES)),
                  _const_spec((1, LANES)),
                  pl.BlockSpec((n_keys, LANES), lambda h, i: (kb, h)),
                  pl.BlockSpec((n_keys, LANES), lambda h, i: (kb, h))],
        out_specs=pl.BlockSpec((tq, LANES), lambda h, i: (i, h)),
        out_shape=jax.ShapeDtypeStruct((nq, D_MODEL), BF16),
        scratch_shapes=[pltpu.VMEM((2 * tq, LANES), BF16),
                        pltpu.VMEM((2 * tq, tk), F32),
                        pltpu.VMEM((2 * tq, tk), F32),
                        pltpu.VMEM((2 * tq, LANES), F32),
                        pltpu.VMEM((2 * tq, 2 * LANES), F32)],
        compiler_params=_cparams(2),
        name="diff_attn",
    )(q, lamv, subln.reshape(1, LANES), k, v)


def _win_attn_kernel(tq, n_lat, with_window, *refs):
    if with_window:
        (q_ref, kp_ref, kc_ref, kn_ref, vp_ref, vc_ref, vn_ref,
         kx_ref, vx_ref, sink_ref, o_ref) = refs
    else:
        q_ref, kx_ref, vx_ref, sink_ref, o_ref = refs
    i = pl.program_id(0)
    n_ctx = kx_ref.shape[0]
    if with_window:
        span = tq + 2 * WINDOW
        qpos = i * tq + lax.broadcasted_iota(jnp.int32, (tq, span), 0)
        kpos = i * tq - WINDOW + lax.broadcasted_iota(jnp.int32, (tq, span), 1)
        valid = (jnp.abs(kpos - qpos) <= WINDOW) & (kpos >= 0) & (kpos < n_lat)
    lane = lax.broadcasted_iota(jnp.int32, (tq, LANES), 1)
    low = lane < HEAD_DIM
    for pair in range(WIN_Q_HEADS // 2):
        kvh = pair // (WIN_GROUP // 2)
        qs = slice(pair * LANES, (pair + 1) * LANES)
        ks = slice(kvh * LANES, (kvh + 1) * LANES)
        qp = q_ref[:, qs]
        zero = jnp.zeros_like(qp)
        kx, vx = kx_ref[:, ks], vx_ref[:, ks]
        if with_window:
            kw = jnp.concatenate([kp_ref[:, ks], kc_ref[:, ks], kn_ref[:, ks]], axis=0)
            vw = jnp.concatenate([vp_ref[:, ks], vc_ref[:, ks], vn_ref[:, ks]], axis=0)
        outs = []
        for half in range(2):
            head = 2 * pair + half
            qh = jnp.where(low if half == 0 else ~low, qp, zero)
            sink = sink_ref[head:head + 1, 0:1]
            s_x = lax.dot_general(qh, kx, (((1,), (1,)), ((), ())), preferred_element_type=F32)
            m = jnp.maximum(jnp.max(s_x, axis=-1, keepdims=True), sink)
            if with_window:
                s_w = lax.dot_general(qh, kw, (((1,), (1,)), ((), ())),
                                      preferred_element_type=F32)
                s_w = jnp.where(valid, s_w, NEG_INF)
                m = jnp.maximum(m, jnp.max(s_w, axis=-1, keepdims=True))
            p_x = jnp.exp(s_x - m)
            den = jnp.sum(p_x, axis=-1, keepdims=True) + jnp.exp(sink - m)
            o = jnp.dot(p_x.astype(BF16), vx, preferred_element_type=F32)
            if with_window:
                p_w = jnp.exp(s_w - m)
                den = den + jnp.sum(p_w, axis=-1, keepdims=True)
                o = o + jnp.dot(p_w.astype(BF16), vw, preferred_element_type=F32)
            outs.append(o / den)
        o_ref[:, qs] = jnp.where(low, outs[0], outs[1]).astype(BF16)


def _win_attn(q, k, v, kx, vx, sink_rows, with_window):
    nq = q.shape[0]
    tq = min(256, nq)
    n_ctx = kx.shape[0]
    kvw = WIN_KV_HEADS * LANES
    in_specs = [pl.BlockSpec((tq, D_MODEL), lambda i: (i, 0))]
    args = [q]
    if with_window:
        r = tq // WINDOW
        last = nq // WINDOW - 1
        prev_spec = pl.BlockSpec((WINDOW, kvw), lambda i: (jnp.maximum(i * r - 1, 0), 0))
        cur_spec = pl.BlockSpec((tq, kvw), lambda i: (i, 0))
        next_spec = pl.BlockSpec((WINDOW, kvw), lambda i: (jnp.minimum((i + 1) * r, last), 0))
        in_specs += [prev_spec, cur_spec, next_spec] * 2
        args += [k, k, k, v, v, v]
    in_specs += [_const_spec((n_ctx, kvw)), _const_spec((n_ctx, kvw)),
                 _const_spec((WIN_Q_HEADS, LANES))]
    args += [kx, vx, sink_rows]
    return pl.pallas_call(
        functools.partial(_win_attn_kernel, tq, nq, with_window),
        grid=(nq // tq,),
        in_specs=in_specs,
        out_specs=pl.BlockSpec((tq, D_MODEL), lambda i: (i, 0)),
        out_shape=jax.ShapeDtypeStruct((nq, D_MODEL), BF16),
        compiler_params=_cparams(1),
        name="win_attn",
    )(*args)


def _out_res_kernel(row, has_bias, *refs):
    if has_bias:
        x_ref, h_ref, mod_ref, w_ref, b_ref, o_ref = refs
    else:
        x_ref, h_ref, mod_ref, w_ref, o_ref = refs
    y = jnp.dot(x_ref[...], w_ref[...], preferred_element_type=F32)
    if has_bias:
        y = y + b_ref[...]
    o_ref[...] = h_ref[...] + _mod(mod_ref, row, G1) * y


def _out_res(x, h, mods, layer, row, w, bias=None):
    n = h.shape[0]
    t = _row_tile(n)
    in_specs = [pl.BlockSpec((t, D_MODEL), lambda i: (i, 0)),
                pl.BlockSpec((t, D_MODEL), lambda i: (i, 0)),
                pl.BlockSpec((None, 8, 6 * D_MODEL), lambda i: (layer, 0, 0)),
                _const_spec((D_MODEL, D_MODEL))]
    args = [x, h, mods, w]
    if bias is not None:
        in_specs.append(_const_spec((1, D_MODEL)))
        args.append(bias.reshape(1, D_MODEL))
    return pl.pallas_call(
        functools.partial(_out_res_kernel, row, bias is not None),
        grid=(n // t,),
        in_specs=in_specs,
        out_specs=pl.BlockSpec((t, D_MODEL), lambda i: (i, 0)),
        out_shape=jax.ShapeDtypeStruct((n, D_MODEL), F32),
        compiler_params=_cparams(1),
        name="out_res",
    )(*args)


FFN_CHUNK = D_FF // 2


def _ffn_kernel(row, h_ref, mod_ref, n2_ref, wgu_ref, wd_ref, o_ref):
    h = h_ref[...]
    a = _mod_norm(h, n2_ref[...], _mod(mod_ref, row, SC2), _mod(mod_ref, row, SH2)).astype(BF16)
    acc = None
    for c in range(D_FF // FFN_CHUNK):
        lo = c * FFN_CHUNK
        g = jnp.dot(a, wgu_ref[:, lo:lo + FFN_CHUNK], preferred_element_type=F32)
        u = jnp.dot(a, wgu_ref[:, D_FF + lo:D_FF + lo + FFN_CHUNK], preferred_element_type=F32)
        act = (_silu(g) * u).astype(BF16)
        part = jnp.dot(act, wd_ref[lo:lo + FFN_CHUNK, :], preferred_element_type=F32)
        acc = part if acc is None else acc + part
    o_ref[...] = h + _mod(mod_ref, row, G2) * acc


def _ffn(h, mods, layer, row, n2, wgu, wd):
    n = h.shape[0]
    t = _row_tile(n)
    return pl.pallas_call(
        functools.partial(_ffn_kernel, row),
        grid=(n // t,),
        in_specs=[pl.BlockSpec((t, D_MODEL), lambda i: (i, 0)),
                  pl.BlockSpec((None, 8, 6 * D_MODEL), lambda i: (layer, 0, 0)),
                  _const_spec((1, D_MODEL)),
                  _const_spec((D_MODEL, 2 * D_FF)),
                  _const_spec((D_FF, D_MODEL))],
        out_specs=pl.BlockSpec((t, D_MODEL), lambda i: (i, 0)),
        out_shape=jax.ShapeDtypeStruct((n, D_MODEL), F32),
        compiler_params=_cparams(1),
        name="ffn",
    )(h, mods, n2.reshape(1, D_MODEL), wgu, wd)


def _in_proj_kernel(row, mode, *refs):
    if mode == "short":
        h_ref, mod_ref, n1_ref, w_ref, b_out, cu_out = refs
    else:
        h_ref, mod_ref, n1_ref, w_ref, bias_ref, glu_out = refs
    a = _mod_norm(h_ref[...], n1_ref[...], _mod(mod_ref, row, SC1), _mod(mod_ref, row, SH1))
    y = jnp.dot(a.astype(BF16), w_ref[...], preferred_element_type=F32)
    if mode == "short":
        b_out[...] = y[:, :D_MODEL]
        cu_out[...] = y[:, D_MODEL:2 * D_MODEL] * y[:, 2 * D_MODEL:]
    else:
        y = y + bias_ref[...]
        g = y[:, D_MODEL:]
        glu_out[...] = y[:, :D_MODEL] * (1.0 / (1.0 + jnp.exp(-g)))


def _in_proj(h, mods, layer, row, n1, w, mode, bias=None):
    n = h.shape[0]
    t = _row_tile(n)
    wtot = w.shape[1]
    in_specs = [pl.BlockSpec((t, D_MODEL), lambda i: (i, 0)),
                pl.BlockSpec((None, 8, 6 * D_MODEL), lambda i: (layer, 0, 0)),
                _const_spec((1, D_MODEL)),
                _const_spec((D_MODEL, wtot))]
    args = [h, mods, n1.reshape(1, D_MODEL), w]
    row_spec = pl.BlockSpec((t, D_MODEL), lambda i: (i, 0))
    row_shape = jax.ShapeDtypeStruct((n, D_MODEL), F32)
    if mode == "short":
        out_specs, out_shape = [row_spec, row_spec], [row_shape, row_shape]
    else:
        in_specs.append(_const_spec((1, wtot)))
        args.append(bias.reshape(1, wtot))
        out_specs, out_shape = row_spec, row_shape
    return pl.pallas_call(
        functools.partial(_in_proj_kernel, row, mode),
        grid=(n // t,),
        in_specs=in_specs,
        out_specs=out_specs,
        out_shape=out_shape,
        compiler_params=_cparams(1),
        name="in_proj_" + mode,
    )(*args)


def _dwconv_tile(xs_ref, prev_ref, cur_ref, next_ref, w_ref, taps, t):
    i = pl.program_id(0)
    last = pl.num_programs(0) - 1
    xs_ref[0:CONV_HALO, :] = jnp.where(i > 0, prev_ref[...], 0.0)
    xs_ref[CONV_HALO:CONV_HALO + t, :] = cur_ref[...]
    xs_ref[CONV_HALO + t:, :] = jnp.where(i < last, next_ref[...], 0.0)
    base = CONV_HALO - taps // 2
    acc = None
    for k in range(taps):
        term = xs_ref[base + k:base + k + t, :] * w_ref[k:k + 1, :]
        acc = term if acc is None else acc + term
    return acc


def _conv_out_kernel(row, mode, taps, t, *refs):
    if mode == "short":
        (prev_ref, cur_ref, next_ref, cw_ref, b_ref, h_ref, mod_ref, w_ref, o_ref, xs_ref) = refs
    else:
        (prev_ref, cur_ref, next_ref, cw_ref, dwb_ref, lng_ref, lnb_ref, h_ref, mod_ref,
         w_ref, pb_ref, o_ref, xs_ref) = refs
    u = _dwconv_tile(xs_ref, prev_ref, cur_ref, next_ref, cw_ref, taps, t)
    if mode == "short":
        y = jnp.dot((b_ref[...] * u).astype(BF16), w_ref[...], preferred_element_type=F32)
    else:
        u = u + dwb_ref[...]
        mu = jnp.mean(u, axis=-1, keepdims=True)
        uc = u - mu
        var = jnp.mean(uc * uc, axis=-1, keepdims=True)
        z = _silu(uc * lax.rsqrt(var + NORM_EPS) * lng_ref[...] + lnb_ref[...])
        y = jnp.dot(z.astype(BF16), w_ref[...], preferred_element_type=F32) + pb_ref[...]
    o_ref[...] = h_ref[...] + _mod(mod_ref, row, G1) * y


def _conv_out(x, h, mods, layer, row, conv_w, w, mode, extra):
    n = h.shape[0]
    t = _row_tile(n)
    taps = conv_w.shape[0]
    r = t // CONV_HALO
    last = n // CONV_HALO - 1
    taps_pad = -(-taps // 8) * 8
    cw = jnp.zeros((taps_pad, D_MODEL), F32).at[:taps].set(conv_w)
    row_spec = pl.BlockSpec((t, D_MODEL), lambda i: (i, 0))
    vec_spec = _const_spec((1, D_MODEL))
    in_specs = [pl.BlockSpec((CONV_HALO, D_MODEL), lambda i: (jnp.maximum(i * r - 1, 0), 0)),
                row_spec,
                pl.BlockSpec((CONV_HALO, D_MODEL), lambda i: (jnp.minimum((i + 1) * r, last), 0)),
                _const_spec((taps_pad, D_MODEL))]
    args = [x, x, x, cw]
    mod_spec = pl.BlockSpec((None, 8, 6 * D_MODEL), lambda i: (layer, 0, 0))
    if mode == "short":
        (b_gate,) = extra
        in_specs += [row_spec, row_spec, mod_spec, _const_spec((D_MODEL, D_MODEL))]
        args += [b_gate, h, mods, w]
    else:
        dw_b, ln_g, ln_b, pw_b = extra
        in_specs += [vec_spec, vec_spec, vec_spec, row_spec, mod_spec,
                     _const_spec((D_MODEL, D_MODEL)), vec_spec]
        args += [dw_b.reshape(1, D_MODEL), ln_g.reshape(1, D_MODEL), ln_b.reshape(1, D_MODEL),
                 h, mods, w, pw_b.reshape(1, D_MODEL)]
    return pl.pallas_call(
        functools.partial(_conv_out_kernel, row, mode, taps, t),
        grid=(n // t,),
        in_specs=in_specs,
        out_specs=row_spec,
        out_shape=jax.ShapeDtypeStruct((n, D_MODEL), F32),
        scratch_shapes=[pltpu.VMEM((t + 2 * CONV_HALO, D_MODEL), F32)],
        compiler_params=_cparams(1),
        name="conv_out_" + mode,
    )(*args)


def _diff_qkv_weight(w_qkv):
    idx = np.arange(D_MODEL).reshape(2, DIFF_HEADS, HEAD_DIM).transpose(1, 0, 2).reshape(-1)
    cols = np.concatenate([idx, D_MODEL + idx, 2 * D_MODEL + np.arange(D_MODEL)])
    return w_qkv[:, cols].astype(BF16)


def _win_qkv_weight(w_qkv):
    qw = WIN_Q_HEADS * HEAD_DIM
    kvw = WIN_KV_HEADS * HEAD_DIM
    dup = np.repeat(np.arange(kvw).reshape(WIN_KV_HEADS, 1, HEAD_DIM), 2, axis=1).reshape(-1)
    cols = np.concatenate([np.arange(qw), qw + dup, qw + kvw + dup])
    return w_qkv[:, cols].astype(BF16)


def kernel(x, c, ctx, c_ctx, ada_w, ada_b, norm1, norm2, ffn_w_gate_up, ffn_w_down, diff_w_qkv, diff_w_o, diff_q_norm, diff_k_norm, diff_lam_q1, diff_lam_k1, diff_lam_q2, diff_lam_k2, diff_subln, sc_w_in, sc_conv_w, sc_w_out, win_w_qkv, win_w_o, win_q_norm, win_k_norm, win_sink, cf_w_pw1, cf_b_pw1, cf_dw_w, cf_dw_b, cf_ln_g, cf_ln_b, cf_w_pw2, cf_b_pw2):
    n = x.shape[1]
    h, hc = x[0], ctx[0]
    mods = _ada_mod(c, c_ctx, ada_w, ada_b)
    tab = _rope_tables(n)
    for i in range(DEPTH):
        kind, j = i % N_MIXERS, i // N_MIXERS
        with_ctx = i < DEPTH - 1
        if kind == 0:
            lam_init = 0.8 - 0.6 * math.exp(-0.3 * i)
            w = _diff_qkv_weight(diff_w_qkv[j])
            w_o = diff_w_o[j].astype(BF16)
            lamv = jnp.zeros((8, LANES), F32).at[0:4, 0:HEAD_DIM].set(
                jnp.stack([diff_lam_q1[j], diff_lam_k1[j], diff_lam_q2[j], diff_lam_k2[j]]))
            q_scale = HEAD_DIM ** -0.5 * LOG2E
            proj = functools.partial(_qkv_proj, mods=mods, layer=i, n1=norm1[i], w=w,
                                     q_gain=diff_q_norm[j], k_gain=diff_k_norm[j],
                                     kw=D_MODEL, vw=D_MODEL, q_scale=q_scale)
            n_ctx = hc.shape[0]
            q_l, k_all, v_all = proj(h, row=0, tab=tab, kv_rows=n + n_ctx)
            q_c, k_all, v_all = proj(hc, row=1, tab=None, kv_into=(k_all, v_all), kv_row_offset=n)
            o_l = _diff_attn(q_l, k_all, v_all, 0, n + n_ctx, lamv, diff_subln[j], lam_init)
            h = _out_res(o_l, h, mods, i, 0, w_o)
            if with_ctx:
                o_c = _diff_attn(q_c, k_all, v_all, n, n_ctx, lamv, diff_subln[j], lam_init)
                hc = _out_res(o_c, hc, mods, i, 1, w_o)
        elif kind == 1:
            w_in = sc_w_in[j].astype(BF16)
            w_out = sc_w_out[j].astype(BF16)
            b_l, cu_l = _in_proj(h, mods, i, 0, norm1[i], w_in, "short")
            h = _conv_out(cu_l, h, mods, i, 0, sc_conv_w[j], w_out, "short", (b_l,))
            if with_ctx:
                b_c, cu_c = _in_proj(hc, mods, i, 1, norm1[i], w_in, "short")
                hc = _conv_out(cu_c, hc, mods, i, 1, sc_conv_w[j], w_out, "short", (b_c,))
        elif kind == 2:
            w = _win_qkv_weight(win_w_qkv[j])
            w_o = win_w_o[j].astype(BF16)
            kvw = WIN_KV_HEADS * LANES
            sink_rows = jnp.broadcast_to(win_sink[j][:, None], (WIN_Q_HEADS, LANES))
            proj = functools.partial(_qkv_proj, mods=mods, layer=i, n1=norm1[i], w=w,
                                     q_gain=win_q_norm[j], k_gain=win_k_norm[j],
                                     kw=kvw, vw=kvw, q_scale=HEAD_DIM ** -0.5)
            q_l, k_l, v_l = proj(h, row=0, tab=tab)
            q_c, k_c, v_c = proj(hc, row=1, tab=None)
            o_l = _win_attn(q_l, k_l, v_l, k_c, v_c, sink_rows, True)
            h = _out_res(o_l, h, mods, i, 0, w_o)
            if with_ctx:
                o_c = _win_attn(q_c, None, None, k_c, v_c, sink_rows, False)
                hc = _out_res(o_c, hc, mods, i, 1, w_o)
        else:
            w1 = cf_w_pw1[j].astype(BF16)
            w2 = cf_w_pw2[j].astype(BF16)
            extra = (cf_dw_b[j], cf_ln_g[j], cf_ln_b[j], cf_b_pw2[j])
            glu_l = _in_proj(h, mods, i, 0, norm1[i], w1, "glu", cf_b_pw1[j])
            h = _conv_out(glu_l, h, mods, i, 0, cf_dw_w[j], w2, "conf", extra)
            if with_ctx:
                glu_c = _in_proj(hc, mods, i, 1, norm1[i], w1, "glu", cf_b_pw1[j])
                hc = _conv_out(glu_c, hc, mods, i, 1, cf_dw_w[j], w2, "conf", extra)
        wgu = ffn_w_gate_up[i].astype(BF16)
        wd = ffn_w_down[i].astype(BF16)
        h = _ffn(h, mods, i, 0, norm2[i], wgu, wd)
        if with_ctx:
            hc = _ffn(hc, mods, i, 1, norm2[i], wgu, wd)
    return h[None]
```

```python
import functools
import math

import jax
import jax.numpy as jnp
import numpy as np
from jax import lax
from jax.experimental import pallas as pl
from jax.experimental.pallas import tpu as pltpu

F32 = jnp.float32
BF16 = jnp.bfloat16

D_MODEL = 1024
DEPTH = 4
N_MIXERS = 4
GRID_W = 64
NORM_EPS = 1e-6
ROPE_THETA = 10000.0
HEAD_DIM = 64
NEG_INF = -1e30
DIFF_HEADS = D_MODEL // (2 * HEAD_DIM)
SHORT_CONV_W = 3
WIN_Q_HEADS = D_MODEL // HEAD_DIM
WIN_KV_HEADS = 4
WIN_GROUP = WIN_Q_HEADS // WIN_KV_HEADS
WINDOW = 128
CONF_CONV_W = 31
D_FF = -(-(8 * D_MODEL) // (3 * 256)) * 256

LANES = 128
MXU_TILE = 256
CONV_HALO = 16
VMEM_LIMIT_BYTES = 56 * 1024 * 1024

LOG2E = math.log2(math.e)
SH1, SC1, G1, SH2, SC2, G2 = range(6)


def _cparams(n_axes):
    return pltpu.CompilerParams(dimension_semantics=("parallel",) * n_axes,
                                vmem_limit_bytes=VMEM_LIMIT_BYTES)


def _const_spec(shape):
    nd = len(shape)
    return pl.BlockSpec(shape, lambda *_: (0,) * nd, pipeline_mode=pl.Buffered(1))


def _mod(mod_ref, row, idx):
    return mod_ref[row:row + 1, idx * D_MODEL:(idx + 1) * D_MODEL]


def _mod_norm(h, gain, scale, shift):
    ms = jnp.mean(h * h, axis=-1, keepdims=True)
    return (h * lax.rsqrt(ms + NORM_EPS)) * (gain * (1.0 + scale)) + shift


def _silu(x):
    return x * (1.0 / (1.0 + jnp.exp(-x)))


def _row_tile(n):
    return 512 if n % 512 == 0 else n


def _ada_kernel(act_ref, w_ref, b_ref, o_ref):
    a = _silu(act_ref[...])
    o_ref[...] = jnp.dot(a.astype(BF16), w_ref[...].astype(BF16),
                         preferred_element_type=F32) + b_ref[...]


def _ada_mod(c, c_ctx, ada_w, ada_b):
    act = jnp.zeros((8, D_MODEL), F32).at[0].set(c[0]).at[1].set(c_ctx)
    cols = 1536
    nc = 6 * D_MODEL // cols
    return pl.pallas_call(
        _ada_kernel,
        grid=(DEPTH, nc),
        in_specs=[pl.BlockSpec((8, D_MODEL), lambda i, j: (0, 0)),
                  pl.BlockSpec((None, D_MODEL, cols), lambda i, j: (i, 0, j)),
                  pl.BlockSpec((None, 1, cols), lambda i, j: (i, 0, j))],
        out_specs=pl.BlockSpec((None, 8, cols), lambda i, j: (i, 0, j)),
        out_shape=jax.ShapeDtypeStruct((DEPTH, 8, 6 * D_MODEL), F32),
        compiler_params=_cparams(2),
        name="ada_mod",
    )(act, ada_w, ada_b.reshape(DEPTH, 1, 6 * D_MODEL))


def _group_mean_matrix(width):
    g = np.arange(width) // HEAD_DIM
    return jnp.asarray((g[:, None] == g[None, :]).astype(np.float32) / HEAD_DIM, BF16)


def _rope_tables(n):
    n_rows = n // GRID_W
    rows = jnp.broadcast_to(jnp.arange(n_rows, dtype=F32)[:, None], (n_rows, GRID_W)).reshape(-1)
    cols = jnp.broadcast_to(jnp.arange(GRID_W, dtype=F32)[None, :], (n_rows, GRID_W)).reshape(-1)
    n_freq = HEAD_DIM // 4
    inv_freq = ROPE_THETA ** (-jnp.arange(n_freq, dtype=F32) / n_freq)
    ang_r = rows[:, None] * inv_freq[None, :]
    ang_c = cols[:, None] * inv_freq[None, :]
    ang = jnp.concatenate([ang_r, ang_r, ang_c, ang_c], axis=-1)
    cos, sin = jnp.cos(ang), jnp.sin(ang)
    first = (np.arange(HEAD_DIM) % (2 * n_freq)) < n_freq
    sin_up = jnp.where(first[None, :], -sin, 0.0)
    sin_dn = jnp.where(first[None, :], 0.0, sin)
    reps = LANES // HEAD_DIM
    return jnp.concatenate([jnp.tile(cos, (1, reps)), jnp.tile(sin_up, (1, reps)),
                            jnp.tile(sin_dn, (1, reps))], axis=-1)


def _rope(x, tab):
    w = x.shape[-1]
    reps = w // LANES
    cos = jnp.tile(tab[:, 0:LANES], (1, reps))
    sin_up = jnp.tile(tab[:, LANES:2 * LANES], (1, reps))
    sin_dn = jnp.tile(tab[:, 2 * LANES:3 * LANES], (1, reps))
    quarter = HEAD_DIM // 4
    return (x * cos + pltpu.roll(x, w - quarter, 1) * sin_up
            + pltpu.roll(x, quarter, 1) * sin_dn)


def _qkv_kernel(row, rotary, kw, q_scale, n_alias, *refs):
    refs = list(refs)
    del refs[-3 - n_alias:-3]
    if rotary:
        h_ref, mod_ref, n1_ref, w_ref, gm_ref, qg_ref, kg_ref, tab_ref, q_ref, k_ref, v_ref = refs
    else:
        h_ref, mod_ref, n1_ref, w_ref, gm_ref, qg_ref, kg_ref, q_ref, k_ref, v_ref = refs
    a = _mod_norm(h_ref[...], n1_ref[...], _mod(mod_ref, row, SC1), _mod(mod_ref, row, SH1))
    y = jnp.dot(a.astype(BF16), w_ref[...], preferred_element_type=F32)

    def head_norm(x, gain, width):
        gm = gm_ref[...]
        x2 = (x * x).astype(BF16)
        ms = jnp.concatenate(
            [jnp.dot(x2[:, c:c + MXU_TILE], gm, preferred_element_type=F32)
             for c in range(0, width, MXU_TILE)], axis=1)
        return x * lax.rsqrt(ms + NORM_EPS) * gain

    q = head_norm(y[:, :D_MODEL], qg_ref[...], D_MODEL)
    k = head_norm(y[:, D_MODEL:D_MODEL + kw], kg_ref[...], kw)
    if rotary:
        tab = tab_ref[...]
        q, k = _rope(q, tab), _rope(k, tab)
    q_ref[...] = (q * q_scale).astype(BF16)
    k_ref[...] = k.astype(BF16)
    v_ref[...] = y[:, D_MODEL + kw:].astype(BF16)


def _qkv_proj(h, mods, layer, row, n1, w, q_gain, k_gain, kw, vw, q_scale, tab,
              kv_rows=None, kv_into=None, kv_row_offset=0):
    n = h.shape[0]
    t = _row_tile(n)
    rotary = tab is not None
    wtot = D_MODEL + kw + vw
    in_specs = [pl.BlockSpec((t, D_MODEL), lambda i: (i, 0)),
                pl.BlockSpec((None, 8, 6 * D_MODEL), lambda i: (layer, 0, 0)),
                _const_spec((1, D_MODEL)),
                _const_spec((D_MODEL, wtot)),
                _const_spec((MXU_TILE, MXU_TILE)),
                _const_spec((1, D_MODEL)),
                _const_spec((1, kw))]
    args = [h, mods, n1.reshape(1, D_MODEL), w, _group_mean_matrix(MXU_TILE),
            jnp.tile(q_gain, D_MODEL // HEAD_DIM).reshape(1, D_MODEL),
            jnp.tile(k_gain, kw // HEAD_DIM).reshape(1, kw)]
    if rotary:
        in_specs.append(pl.BlockSpec((t, 3 * LANES), lambda i: (i, 0)))
        args.append(tab)
    aliases = {}
    off = kv_row_offset // t
    kv_total = n if kv_rows is None else kv_rows
    if kv_into is not None:
        kv_total = kv_into[0].shape[0]
        aliases = {len(args): 1, len(args) + 1: 2}
        in_specs += [pl.BlockSpec(memory_space=pl.ANY)] * 2
        args += list(kv_into)
    return pl.pallas_call(
        functools.partial(_qkv_kernel, row, rotary, kw, q_scale, len(aliases)),
        grid=(n // t,),
        in_specs=in_specs,
        out_specs=[pl.BlockSpec((t, D_MODEL), lambda i: (i, 0)),
                   pl.BlockSpec((t, kw), lambda i: (i + off, 0)),
                   pl.BlockSpec((t, vw), lambda i: (i + off, 0))],
        out_shape=[jax.ShapeDtypeStruct((n, D_MODEL), BF16),
                   jax.ShapeDtypeStruct((kv_total, kw), BF16),
                   jax.ShapeDtypeStruct((kv_total, vw), BF16)],
        input_output_aliases=aliases,
        compiler_params=_cparams(1),
        name="qkv_proj",
    )(*args)


DIFF_TQ = 256
DIFF_TK = 1280


def _diff_attn_kernel(tq, tk, n_chunks, lam_init, q_ref, lamv_ref, sub_ref, k_ref, v_ref,
                      o_ref, qq_ref, sa_ref, sb_ref, m_ref, acc_ref):
    q = q_ref[...]
    lane = lax.broadcasted_iota(jnp.int32, q.shape, 1)
    zero = jnp.zeros_like(q)
    qq_ref[0:tq, :] = jnp.where(lane < HEAD_DIM, q, zero)
    qq_ref[tq:, :] = jnp.where(lane >= HEAD_DIM, q, zero)
    m_ref[...] = jnp.full(m_ref.shape, NEG_INF, F32)
    acc_ref[...] = jnp.zeros(acc_ref.shape, F32)

    def chunk(ref, c):
        if isinstance(c, int):
            return ref[c * tk:(c + 1) * tk, :]
        return ref[pl.ds(pl.multiple_of(c * tk, tk), tk), :]

    def scores(c, s_ref):
        s_ref[...] = lax.dot_general(qq_ref[...], chunk(k_ref, c), (((1,), (1,)), ((), ())),
                                     preferred_element_type=F32)

    def softmax_pv(c, s_ref):
        s = s_ref[...]
        m_prev = m_ref[...]
        m_new = jnp.maximum(m_prev, jnp.max(s, axis=-1, keepdims=True))
        alpha = jnp.exp2(m_prev - m_new)
        p = jnp.exp2(s - jnp.tile(m_new, (1, tk // LANES)))
        v_aug = jnp.concatenate([chunk(v_ref, c), jnp.ones((tk, LANES), BF16)], axis=1)
        pv = jnp.dot(p.astype(BF16), v_aug, preferred_element_type=F32)
        acc_ref[...] = acc_ref[...] * jnp.tile(alpha, (1, 2)) + pv
        m_ref[...] = m_new

    bufs = (sa_ref, sb_ref)
    scores(0, bufs[0])
    for c in range(n_chunks):
        if c + 1 < n_chunks:
            scores(c + 1, bufs[(c + 1) % 2])
        softmax_pv(c, bufs[c % 2])

    acc = acc_ref[...]
    o = acc[:, :LANES] / acc[:, LANES:LANES + 1]
    lv = lamv_ref[...]
    lam = (jnp.exp(jnp.sum(lv[0:1] * lv[1:2], axis=-1, keepdims=True))
           - jnp.exp(jnp.sum(lv[2:3] * lv[3:4], axis=-1, keepdims=True)) + lam_init)
    d = o[:tq] - lam * o[tq:]
    ms = jnp.mean(d * d, axis=-1, keepdims=True)
    o_ref[...] = (d * lax.rsqrt(ms + NORM_EPS) * (sub_ref[...] * (1.0 - lam_init))).astype(BF16)


def _diff_attn(q, k, v, key_start, n_keys, lamv, subln, lam_init):
    nq = q.shape[0]
    tq = min(DIFF_TQ, nq)
    tk = min(DIFF_TK, n_keys)
    n_chunks = n_keys // tk
    assert n_keys % tk == 0 and n_chunks % 2 == 1 and key_start % n_keys == 0
    kb = key_start // n_keys
    return pl.pallas_call(
        functools.partial(_diff_attn_kernel, tq, tk, n_chunks, lam_init),
        grid=(DIFF_HEADS, nq // tq),
        in_specs=[pl.BlockSpec((tq, LANES), lambda h, i: (i, h)),
                  _const_spec((8, LANES)),
                  _const_spec((1, LANES)),
                  pl.BlockSpec((n_keys, LANES), lambda h, i: (kb, h)),
                  pl.BlockSpec((n_keys, LANES), lambda h, i: (kb, h))],
        out_specs=pl.BlockSpec((tq, LANES), lambda h, i: (i, h)),
        out_shape=jax.ShapeDtypeStruct((nq, D_MODEL), BF16),
        scratch_shapes=[pltpu.VMEM((2 * tq, LANES), BF16),
                        pltpu.VMEM((2 * tq, tk), F32),
                        pltpu.VMEM((2 * tq, tk), F32),
                        pltpu.VMEM((2 * tq, LANES), F32),
                        pltpu.VMEM((2 * tq, 2 * LANES), F32)],
        compiler_params=_cparams(2),
        name="diff_attn",
    )(q, lamv, subln.reshape(1, LANES), k, v)


def _win_attn_kernel(tq, n_lat, with_window, *refs):
    if with_window:
        (q_ref, kp_ref, kc_ref, kn_ref, vp_ref, vc_ref, vn_ref,
         kx_ref, vx_ref, sink_ref, o_ref) = refs
    else:
        q_ref, kx_ref, vx_ref, sink_ref, o_ref = refs
    i = pl.program_id(0)
    n_ctx = kx_ref.shape[0]
    sub = WINDOW if with_window else tq
    n_win = 3 * WINDOW if with_window else 0
    n_keys = n_win + n_ctx
    rows = WIN_GROUP * sub
    low = lax.broadcasted_iota(jnp.int32, (sub, LANES), 1) < HEAD_DIM
    ones = jnp.ones((n_keys, LANES), BF16)
    nt = (((1,), (1,)), ((), ()))
    for sb in range(tq // sub):
        r0 = sb * sub
        if with_window:
            qpos = i * tq + r0 + lax.broadcasted_iota(jnp.int32, (sub, n_win), 0)
            kpos = i * tq + r0 - WINDOW + lax.broadcasted_iota(jnp.int32, (sub, n_win), 1)
            valid = (jnp.abs(kpos - qpos) <= WINDOW) & (kpos >= 0) & (kpos < n_lat)
            bias = jnp.where(valid, 0.0, NEG_INF)
            bias = jnp.concatenate([bias] * WIN_GROUP, axis=0)
        for kvh in range(WIN_KV_HEADS):
            ks = slice(kvh * LANES, (kvh + 1) * LANES)
            heads = [kvh * WIN_GROUP + g for g in range(WIN_GROUP)]
            blocks = [kvh * (WIN_GROUP // 2) + pr for pr in range(WIN_GROUP // 2)]
            qs = []
            for blk in blocks:
                qp = q_ref[r0:r0 + sub, blk * LANES:(blk + 1) * LANES]
                zero = jnp.zeros_like(qp)
                qs += [jnp.where(low, qp, zero), jnp.where(low, zero, qp)]
            qq = jnp.concatenate(qs, axis=0)
            kk, vv = kx_ref[:, ks], vx_ref[:, ks]
            if with_window:
                w0 = sb * WINDOW
                k_span = jnp.concatenate([kp_ref[:, ks], kc_ref[:, ks], kn_ref[:, ks]], axis=0)
                v_span = jnp.concatenate([vp_ref[:, ks], vc_ref[:, ks], vn_ref[:, ks]], axis=0)
                kk = jnp.concatenate([k_span[w0:w0 + n_win], kk], axis=0)
                vv = jnp.concatenate([v_span[w0:w0 + n_win], vv], axis=0)
            s = lax.dot_general(qq, kk, nt, preferred_element_type=F32)
            if with_window:
                s = jnp.concatenate([s[:, :n_win] + bias, s[:, n_win:]], axis=1)
            sink = jnp.concatenate(
                [jnp.broadcast_to(sink_ref[hd:hd + 1, :], (sub, LANES)) for hd in heads], axis=0)
            m = jnp.maximum(jnp.max(s, axis=-1, keepdims=True), sink)
            p = jnp.exp(s - jnp.tile(m, (1, n_keys // LANES)))
            pv = jnp.dot(p.astype(BF16), jnp.concatenate([vv, ones], axis=1),
                         preferred_element_type=F32)
            den = pv[:, LANES:LANES + 1] + jnp.exp(sink[:, 0:1] - m[:, 0:1])
            o = pv[:, :LANES] / den
            for pr, blk in enumerate(blocks):
                lo = o[(2 * pr) * sub:(2 * pr + 1) * sub]
                hi = o[(2 * pr + 1) * sub:(2 * pr + 2) * sub]
                o_ref[r0:r0 + sub, blk * LANES:(blk + 1) * LANES] = jnp.where(low, lo, hi).astype(BF16)


def _win_attn(q, k, v, kx, vx, sink_rows, with_window):
    nq = q.shape[0]
    tq = min(256, nq)
    n_ctx = kx.shape[0]
    kvw = WIN_KV_HEADS * LANES
    in_specs = [pl.BlockSpec((tq, D_MODEL), lambda i: (i, 0))]
    args = [q]
    if with_window:
        r = tq // WINDOW
        last = nq // WINDOW - 1
        prev_spec = pl.BlockSpec((WINDOW, kvw), lambda i: (jnp.maximum(i * r - 1, 0), 0))
        cur_spec = pl.BlockSpec((tq, kvw), lambda i: (i, 0))
        next_spec = pl.BlockSpec((WINDOW, kvw), lambda i: (jnp.minimum((i + 1) * r, last), 0))
        in_specs += [prev_spec, cur_spec, next_spec] * 2
        args += [k, k, k, v, v, v]
    in_specs += [_const_spec((n_ctx, kvw)), _const_spec((n_ctx, kvw)),
                 _const_spec((WIN_Q_HEADS, LANES))]
    args += [kx, vx, sink_rows]
    return pl.pallas_call(
        functools.partial(_win_attn_kernel, tq, nq, with_window),
        grid=(nq // tq,),
        in_specs=in_specs,
        out_specs=pl.BlockSpec((tq, D_MODEL), lambda i: (i, 0)),
        out_shape=jax.ShapeDtypeStruct((nq, D_MODEL), BF16),
        compiler_params=_cparams(1),
        name="win_attn",
    )(*args)


def _out_res_kernel(row, has_bias, *refs):
    if has_bias:
        x_ref, h_ref, mod_ref, w_ref, b_ref, o_ref = refs
    else:
        x_ref, h_ref, mod_ref, w_ref, o_ref = refs
    y = jnp.dot(x_ref[...], w_ref[...], preferred_element_type=F32)
    if has_bias:
        y = y + b_ref[...]
    o_ref[...] = h_ref[...] + _mod(mod_ref, row, G1) * y


def _out_res(x, h, mods, layer, row, w, bias=None):
    n = h.shape[0]
    t = _row_tile(n)
    in_specs = [pl.BlockSpec((t, D_MODEL), lambda i: (i, 0)),
                pl.BlockSpec((t, D_MODEL), lambda i: (i, 0)),
                pl.BlockSpec((None, 8, 6 * D_MODEL), lambda i: (layer, 0, 0)),
                _const_spec((D_MODEL, D_MODEL))]
    args = [x, h, mods, w]
    if bias is not None:
        in_specs.append(_const_spec((1, D_MODEL)))
        args.append(bias.reshape(1, D_MODEL))
    return pl.pallas_call(
        functools.partial(_out_res_kernel, row, bias is not None),
        grid=(n // t,),
        in_specs=in_specs,
        out_specs=pl.BlockSpec((t, D_MODEL), lambda i: (i, 0)),
        out_shape=jax.ShapeDtypeStruct((n, D_MODEL), F32),
        compiler_params=_cparams(1),
        name="out_res",
    )(*args)


FFN_CHUNK = D_FF // 2


def _ffn_kernel(row, h_ref, mod_ref, n2_ref, wgu_ref, wd_ref, o_ref):
    h = h_ref[...]
    a = _mod_norm(h, n2_ref[...], _mod(mod_ref, row, SC2), _mod(mod_ref, row, SH2)).astype(BF16)
    acc = None
    for c in range(D_FF // FFN_CHUNK):
        lo = c * FFN_CHUNK
        g = jnp.dot(a, wgu_ref[:, lo:lo + FFN_CHUNK], preferred_element_type=F32)
        u = jnp.dot(a, wgu_ref[:, D_FF + lo:D_FF + lo + FFN_CHUNK], preferred_element_type=F32)
        act = (_silu(g) * u).astype(BF16)
        part = jnp.dot(act, wd_ref[lo:lo + FFN_CHUNK, :], preferred_element_type=F32)
        acc = part if acc is None else acc + part
    o_ref[...] = h + _mod(mod_ref, row, G2) * acc


def _ffn(h, mods, layer, row, n2, wgu, wd):
    n = h.shape[0]
    t = _row_tile(n)
    return pl.pallas_call(
        functools.partial(_ffn_kernel, row),
        grid=(n // t,),
        in_specs=[pl.BlockSpec((t, D_MODEL), lambda i: (i, 0)),
                  pl.BlockSpec((None, 8, 6 * D_MODEL), lambda i: (layer, 0, 0)),
                  _const_spec((1, D_MODEL)),
                  _const_spec((D_MODEL, 2 * D_FF)),
                  _const_spec((D_FF, D_MODEL))],
        out_specs=pl.BlockSpec((t, D_MODEL), lambda i: (i, 0)),
        out_shape=jax.ShapeDtypeStruct((n, D_MODEL), F32),
        compiler_params=_cparams(1),
        name="ffn",
    )(h, mods, n2.reshape(1, D_MODEL), wgu, wd)


def _in_proj_kernel(row, mode, *refs):
    if mode == "short":
        h_ref, mod_ref, n1_ref, w_ref, b_out, cu_out = refs
    else:
        h_ref, mod_ref, n1_ref, w_ref, bias_ref, glu_out = refs
    a = _mod_norm(h_ref[...], n1_ref[...], _mod(mod_ref, row, SC1), _mod(mod_ref, row, SH1))
    y = jnp.dot(a.astype(BF16), w_ref[...], preferred_element_type=F32)
    if mode == "short":
        b_out[...] = y[:, :D_MODEL]
        cu_out[...] = y[:, D_MODEL:2 * D_MODEL] * y[:, 2 * D_MODEL:]
    else:
        y = y + bias_ref[...]
        g = y[:, D_MODEL:]
        glu_out[...] = y[:, :D_MODEL] * (1.0 / (1.0 + jnp.exp(-g)))


def _in_proj(h, mods, layer, row, n1, w, mode, bias=None):
    n = h.shape[0]
    t = _row_tile(n)
    wtot = w.shape[1]
    in_specs = [pl.BlockSpec((t, D_MODEL), lambda i: (i, 0)),
                pl.BlockSpec((None, 8, 6 * D_MODEL), lambda i: (layer, 0, 0)),
                _const_spec((1, D_MODEL)),
                _const_spec((D_MODEL, wtot))]
    args = [h, mods, n1.reshape(1, D_MODEL), w]
    row_spec = pl.BlockSpec((t, D_MODEL), lambda i: (i, 0))
    row_shape = jax.ShapeDtypeStruct((n, D_MODEL), F32)
    if mode == "short":
        out_specs, out_shape = [row_spec, row_spec], [row_shape, row_shape]
    else:
        in_specs.append(_const_spec((1, wtot)))
        args.append(bias.reshape(1, wtot))
        out_specs, out_shape = row_spec, row_shape
    return pl.pallas_call(
        functools.partial(_in_proj_kernel, row, mode),
        grid=(n // t,),
        in_specs=in_specs,
        out_specs=out_specs,
        out_shape=out_shape,
        compiler_params=_cparams(1),
        name="in_proj_" + mode,
    )(*args)


CONV_ROW_BLOCK = 64
SUBLANES = 8


def _conv_shifts(taps):
    return [divmod(CONV_HALO - taps // 2 + k, SUBLANES) for k in range(taps)]


def _conv_rems(taps):
    return sorted({r for _, r in _conv_shifts(taps) if r})


def _conv_span(taps, t):
    return t + SUBLANES * max(a for a, _ in _conv_shifts(taps))


def _dwconv_tile(xs_ref, sh_ref, u_ref, prev_ref, cur_ref, next_ref, w_ref, taps, t):
    i = pl.program_id(0)
    last = pl.num_programs(0) - 1
    ncb = D_MODEL // LANES
    for cb in range(ncb):
        cs = slice(cb * LANES, (cb + 1) * LANES)
        xs_ref[cb, 0:CONV_HALO, :] = jnp.where(i > 0, prev_ref[:, cs], 0.0)
        xs_ref[cb, CONV_HALO:CONV_HALO + t, :] = cur_ref[:, cs]
        xs_ref[cb, CONV_HALO + t:, :] = jnp.where(i < last, next_ref[:, cs], 0.0)
    shifts = _conv_shifts(taps)
    rems = _conv_rems(taps)
    span = _conv_span(taps, t)
    for j, r in enumerate(rems):
        sh_ref[j] = xs_ref[:, r:r + span, :]

    def block(idx, carry):
        rb, cb = idx // ncb, idx % ncb
        row0 = pl.multiple_of(rb * CONV_ROW_BLOCK, CONV_ROW_BLOCK)
        accs = [None, None]
        for k, (a, r) in enumerate(shifts):
            rows = pl.ds(row0 + a * SUBLANES, CONV_ROW_BLOCK)
            x = xs_ref[cb, rows, :] if r == 0 else sh_ref[rems.index(r), cb, rows, :]
            term = x * w_ref[cb, k:k + 1, :]
            accs[k % 2] = term if accs[k % 2] is None else accs[k % 2] + term
        u_ref[cb, pl.ds(row0, CONV_ROW_BLOCK), :] = accs[0] + accs[1]
        return carry

    lax.fori_loop(0, (t // CONV_ROW_BLOCK) * ncb, block, 0)
    return jnp.concatenate([u_ref[cb] for cb in range(ncb)], axis=1)


def _conv_out_kernel(row, mode, taps, t, *refs):
    if mode == "short":
        (prev_ref, cur_ref, next_ref, cw_ref, b_ref, h_ref, mod_ref, w_ref, o_ref,
         xs_ref, sh_ref, u_ref) = refs
    else:
        (prev_ref, cur_ref, next_ref, cw_ref, dwb_ref, lng_ref, lnb_ref, h_ref, mod_ref,
         w_ref, pb_ref, o_ref, xs_ref, sh_ref, u_ref) = refs
    u = _dwconv_tile(xs_ref, sh_ref, u_ref, prev_ref, cur_ref, next_ref, cw_ref, taps, t)
    if mode == "short":
        y = jnp.dot((b_ref[...] * u).astype(BF16), w_ref[...], preferred_element_type=F32)
    else:
        u = u + dwb_ref[...]
        mu = jnp.mean(u, axis=-1, keepdims=True)
        uc = u - mu
        var = jnp.mean(uc * uc, axis=-1, keepdims=True)
        z = _silu(uc * lax.rsqrt(var + NORM_EPS) * lng_ref[...] + lnb_ref[...])
        y = jnp.dot(z.astype(BF16), w_ref[...], preferred_element_type=F32) + pb_ref[...]
    o_ref[...] = h_ref[...] + _mod(mod_ref, row, G1) * y


def _conv_out(x, h, mods, layer, row, conv_w, w, mode, extra):
    n = h.shape[0]
    t = _row_tile(n)
    taps = conv_w.shape[0]
    r = t // CONV_HALO
    last = n // CONV_HALO - 1
    taps_pad = -(-taps // 8) * 8
    ncb = D_MODEL // LANES
    cw = jnp.zeros((taps_pad, D_MODEL), F32).at[:taps].set(conv_w)
    cw = cw.reshape(taps_pad, ncb, LANES).transpose(1, 0, 2)
    row_spec = pl.BlockSpec((t, D_MODEL), lambda i: (i, 0))
    vec_spec = _const_spec((1, D_MODEL))
    in_specs = [pl.BlockSpec((CONV_HALO, D_MODEL), lambda i: (jnp.maximum(i * r - 1, 0), 0)),
                row_spec,
                pl.BlockSpec((CONV_HALO, D_MODEL), lambda i: (jnp.minimum((i + 1) * r, last), 0)),
                _const_spec((ncb, taps_pad, LANES))]
    args = [x, x, x, cw]
    mod_spec = pl.BlockSpec((None, 8, 6 * D_MODEL), lambda i: (layer, 0, 0))
    if mode == "short":
        (b_gate,) = extra
        in_specs += [row_spec, row_spec, mod_spec, _const_spec((D_MODEL, D_MODEL))]
        args += [b_gate, h, mods, w]
    else:
        dw_b, ln_g, ln_b, pw_b = extra
        in_specs += [vec_spec, vec_spec, vec_spec, row_spec, mod_spec,
                     _const_spec((D_MODEL, D_MODEL)), vec_spec]
        args += [dw_b.reshape(1, D_MODEL), ln_g.reshape(1, D_MODEL), ln_b.reshape(1, D_MODEL),
                 h, mods, w, pw_b.reshape(1, D_MODEL)]
    return pl.pallas_call(
        functools.partial(_conv_out_kernel, row, mode, taps, t),
        grid=(n // t,),
        in_specs=in_specs,
        out_specs=row_spec,
        out_shape=jax.ShapeDtypeStruct((n, D_MODEL), F32),
        scratch_shapes=[pltpu.VMEM((ncb, t + 2 * CONV_HALO, LANES), F32),
                        pltpu.VMEM((len(_conv_rems(taps)), ncb, _conv_span(taps, t), LANES), F32),
                        pltpu.VMEM((ncb, t, LANES), F32)],
        compiler_params=_cparams(1),
        name="conv_out_" + mode,
    )(*args)


def _diff_qkv_weight(w_qkv):
    idx = np.arange(D_MODEL).reshape(2, DIFF_HEADS, HEAD_DIM).transpose(1, 0, 2).reshape(-1)
    cols = np.concatenate([idx, D_MODEL + idx, 2 * D_MODEL + np.arange(D_MODEL)])
    return w_qkv[:, cols].astype(BF16)


def _win_qkv_weight(w_qkv):
    qw = WIN_Q_HEADS * HEAD_DIM
    kvw = WIN_KV_HEADS * HEAD_DIM
    dup = np.repeat(np.arange(kvw).reshape(WIN_KV_HEADS, 1, HEAD_DIM), 2, axis=1).reshape(-1)
    cols = np.concatenate([np.arange(qw), qw + dup, qw + kvw + dup])
    return w_qkv[:, cols].astype(BF16)


def kernel(x, c, ctx, c_ctx, ada_w, ada_b, norm1, norm2, ffn_w_gate_up, ffn_w_down, diff_w_qkv, diff_w_o, diff_q_norm, diff_k_norm, diff_lam_q1, diff_lam_k1, diff_lam_q2, diff_lam_k2, diff_subln, sc_w_in, sc_conv_w, sc_w_out, win_w_qkv, win_w_o, win_q_norm, win_k_norm, win_sink, cf_w_pw1, cf_b_pw1, cf_dw_w, cf_dw_b, cf_ln_g, cf_ln_b, cf_w_pw2, cf_b_pw2):
    n = x.shape[1]
    h, hc = x[0], ctx[0]
    mods = _ada_mod(c, c_ctx, ada_w, ada_b)
    tab = _rope_tables(n)
    for i in range(DEPTH):
        kind, j = i % N_MIXERS, i // N_MIXERS
        with_ctx = i < DEPTH - 1
        if kind == 0:
            lam_init = 0.8 - 0.6 * math.exp(-0.3 * i)
            w = _diff_qkv_weight(diff_w_qkv[j])
            w_o = diff_w_o[j].astype(BF16)
            lamv = jnp.zeros((8, LANES), F32).at[0:4, 0:HEAD_DIM].set(
                jnp.stack([diff_lam_q1[j], diff_lam_k1[j], diff_lam_q2[j], diff_lam_k2[j]]))
            q_scale = HEAD_DIM ** -0.5 * LOG2E
            proj = functools.partial(_qkv_proj, mods=mods, layer=i, n1=norm1[i], w=w,
                                     q_gain=diff_q_norm[j], k_gain=diff_k_norm[j],
                                     kw=D_MODEL, vw=D_MODEL, q_scale=q_scale)
            n_ctx = hc.shape[0]
            q_l, k_all, v_all = proj(h, row=0, tab=tab, kv_rows=n + n_ctx)
            q_c, k_all, v_all = proj(hc, row=1, tab=None, kv_into=(k_all, v_all), kv_row_offset=n)
            o_l = _diff_attn(q_l, k_all, v_all, 0, n + n_ctx, lamv, diff_subln[j], lam_init)
            h = _out_res(o_l, h, mods, i, 0, w_o)
            if with_ctx:
                o_c = _diff_attn(q_c, k_all, v_all, n, n_ctx, lamv, diff_subln[j], lam_init)
                hc = _out_res(o_c, hc, mods, i, 1, w_o)
        elif kind == 1:
            w_in = sc_w_in[j].astype(BF16)
            w_out = sc_w_out[j].astype(BF16)
            b_l, cu_l = _in_proj(h, mods, i, 0, norm1[i], w_in, "short")
            h = _conv_out(cu_l, h, mods, i, 0, sc_conv_w[j], w_out, "short", (b_l,))
            if with_ctx:
                b_c, cu_c = _in_proj(hc, mods, i, 1, norm1[i], w_in, "short")
                hc = _conv_out(cu_c, hc, mods, i, 1, sc_conv_w[j], w_out, "short", (b_c,))
        elif kind == 2:
            w = _win_qkv_weight(win_w_qkv[j])
            w_o = win_w_o[j].astype(BF16)
            kvw = WIN_KV_HEADS * LANES
            sink_rows = jnp.broadcast_to(win_sink[j][:, None], (WIN_Q_HEADS, LANES))
            proj = functools.partial(_qkv_proj, mods=mods, layer=i, n1=norm1[i], w=w,
                                     q_gain=win_q_norm[j], k_gain=win_k_norm[j],
                                     kw=kvw, vw=kvw, q_scale=HEAD_DIM ** -0.5)
            q_l, k_l, v_l = proj(h, row=0, tab=tab)
            q_c, k_c, v_c = proj(hc, row=1, tab=None)
            o_l = _win_attn(q_l, k_l, v_l, k_c, v_c, sink_rows, True)
            h = _out_res(o_l, h, mods, i, 0, w_o)
            if with_ctx:
                o_c = _win_attn(q_c, None, None, k_c, v_c, sink_rows, False)
                hc = _out_res(o_c, hc, mods, i, 1, w_o)
        else:
            w1 = cf_w_pw1[j].astype(BF16)
            w2 = cf_w_pw2[j].astype(BF16)
            extra = (cf_dw_b[j], cf_ln_g[j], cf_ln_b[j], cf_b_pw2[j])
            glu_l = _in_proj(h, mods, i, 0, norm1[i], w1, "glu", cf_b_pw1[j])
            h = _conv_out(glu_l, h, mods, i, 0, cf_dw_w[j], w2, "conf", extra)
            if with_ctx:
                glu_c = _in_proj(hc, mods, i, 1, norm1[i], w1, "glu", cf_b_pw1[j])
                hc = _conv_out(glu_c, hc, mods, i, 1, cf_dw_w[j], w2, "conf", extra)
        wgu = ffn_w_gate_up[i].astype(BF16)
        wd = ffn_w_down[i].astype(BF16)
        h = _ffn(h, mods, i, 0, norm2[i], wgu, wd)
        if with_ctx:
            hc = _ffn(hc, mods, i, 1, norm2[i], wgu, wd)
    return h[None]
```

```python
import functools
import math

import jax
import jax.numpy as jnp
import numpy as np
from jax import lax
from jax.experimental import pallas as pl
from jax.experimental.pallas import tpu as pltpu

F32 = jnp.float32
BF16 = jnp.bfloat16

D_MODEL = 1024
DEPTH = 4
N_MIXERS = 4
GRID_W = 64
NORM_EPS = 1e-6
ROPE_THETA = 10000.0
HEAD_DIM = 64
NEG_INF = -1e30
DIFF_HEADS = D_MODEL // (2 * HEAD_DIM)
SHORT_CONV_W = 3
WIN_Q_HEADS = D_MODEL // HEAD_DIM
WIN_KV_HEADS = 4
WIN_GROUP = WIN_Q_HEADS // WIN_KV_HEADS
WINDOW = 128
CONF_CONV_W = 31
D_FF = -(-(8 * D_MODEL) // (3 * 256)) * 256

LANES = 128
MXU_TILE = 256
CONV_HALO = 16
VMEM_LIMIT_BYTES = 56 * 1024 * 1024

LOG2E = math.log2(math.e)
SH1, SC1, G1, SH2, SC2, G2 = range(6)


def _cparams(n_axes):
    return pltpu.CompilerParams(dimension_semantics=("parallel",) * n_axes,
                                vmem_limit_bytes=VMEM_LIMIT_BYTES)


def _const_spec(shape):
    nd = len(shape)
    return pl.BlockSpec(shape, lambda *_: (0,) * nd, pipeline_mode=pl.Buffered(1))


def _mod(mod_ref, row, idx):
    return mod_ref[row:row + 1, idx * D_MODEL:(idx + 1) * D_MODEL]


def _mod_norm(h, gain, scale, shift):
    ms = jnp.mean(h * h, axis=-1, keepdims=True)
    return (h * lax.rsqrt(ms + NORM_EPS)) * (gain * (1.0 + scale)) + shift


def _silu(x):
    return x * (1.0 / (1.0 + jnp.exp(-x)))


def _row_tile(n):
    return 512 if n % 512 == 0 else n


def _ada_kernel(act_ref, w_ref, b_ref, o_ref):
    a = _silu(act_ref[...])
    o_ref[...] = jnp.dot(a.astype(BF16), w_ref[...].astype(BF16),
                         preferred_element_type=F32) + b_ref[...]


def _ada_mod(c, c_ctx, ada_w, ada_b):
    act = jnp.zeros((8, D_MODEL), F32).at[0].set(c[0]).at[1].set(c_ctx)
    cols = 1536
    nc = 6 * D_MODEL // cols
    return pl.pallas_call(
        _ada_kernel,
        grid=(DEPTH, nc),
        in_specs=[pl.BlockSpec((8, D_MODEL), lambda i, j: (0, 0)),
                  pl.BlockSpec((None, D_MODEL, cols), lambda i, j: (i, 0, j)),
                  pl.BlockSpec((None, 1, cols), lambda i, j: (i, 0, j))],
        out_specs=pl.BlockSpec((None, 8, cols), lambda i, j: (i, 0, j)),
        out_shape=jax.ShapeDtypeStruct((DEPTH, 8, 6 * D_MODEL), F32),
        compiler_params=_cparams(2),
        name="ada_mod",
    )(act, ada_w, ada_b.reshape(DEPTH, 1, 6 * D_MODEL))


def _group_mean_matrix(width):
    g = np.arange(width) // HEAD_DIM
    return jnp.asarray((g[:, None] == g[None, :]).astype(np.float32) / HEAD_DIM, BF16)


def _rope_tables(n, t):
    n_freq = HEAD_DIM // 4
    inv_freq = ROPE_THETA ** (-jnp.arange(n_freq, dtype=F32) / n_freq)
    reps = LANES // HEAD_DIM

    def tables(count, is_row):
        ang = jnp.arange(count, dtype=F32)[:, None] * inv_freq[None, :]
        cos, sin, zero = jnp.cos(ang), jnp.sin(ang), jnp.zeros_like(ang)
        halves = [[cos, cos], [-sin, zero], [zero, sin]]
        out = []
        for first, second in halves:
            pair = [first, second, zero, zero] if is_row else [zero, zero, first, second]
            out.append(jnp.tile(jnp.concatenate(pair, axis=-1), (1, reps)))
        return jnp.concatenate(out, axis=-1)

    return tables(n // GRID_W, True), jnp.tile(tables(GRID_W, False), (t // GRID_W, 1))


def _rope(x, tab):
    w = x.shape[-1]
    reps = w // LANES
    cos = jnp.tile(tab[:, 0:LANES], (1, reps))
    sin_up = jnp.tile(tab[:, LANES:2 * LANES], (1, reps))
    sin_dn = jnp.tile(tab[:, 2 * LANES:3 * LANES], (1, reps))
    quarter = HEAD_DIM // 4
    return (x * cos + pltpu.roll(x, w - quarter, 1) * sin_up
            + pltpu.roll(x, quarter, 1) * sin_dn)


QKV_SUB_ROWS = 256


def _qkv_kernel(row, rotary, kw, q_scale, n_alias, *refs):
    refs = list(refs)
    del refs[-3 - n_alias:-3]
    if rotary:
        (h_ref, mod_ref, n1_ref, w_ref, gm_ref, qg_ref, kg_ref, rt_ref, ct_ref,
         q_ref, k_ref, v_ref) = refs
    else:
        h_ref, mod_ref, n1_ref, w_ref, gm_ref, qg_ref, kg_ref, q_ref, k_ref, v_ref = refs
    gm = gm_ref[...]

    def head_norm(x, gain, width):
        x2 = (x * x).astype(BF16)
        ms = jnp.concatenate(
            [jnp.dot(x2[:, c:c + MXU_TILE], gm, preferred_element_type=F32)
             for c in range(0, width, MXU_TILE)], axis=1)
        return x * lax.rsqrt(ms + NORM_EPS) * gain

    t = h_ref.shape[0]
    sub = min(QKV_SUB_ROWS, t)
    for sb in range(t // sub):
        rs = slice(sb * sub, (sb + 1) * sub)
        a = _mod_norm(h_ref[rs, :], n1_ref[...], _mod(mod_ref, row, SC1), _mod(mod_ref, row, SH1))
        y = jnp.dot(a.astype(BF16), w_ref[...], preferred_element_type=F32)
        q = head_norm(y[:, :D_MODEL], qg_ref[...], D_MODEL)
        k = head_norm(y[:, D_MODEL:D_MODEL + kw], kg_ref[...], kw)
        if rotary:
            g0 = sb * sub // GRID_W
            row_part = jnp.concatenate(
                [jnp.broadcast_to(rt_ref[g0 + r:g0 + r + 1, :], (GRID_W, 3 * LANES))
                 for r in range(sub // GRID_W)], axis=0)
            tab = row_part + ct_ref[rs, :]
            q, k = _rope(q, tab), _rope(k, tab)
        q_ref[rs, :] = (q * q_scale).astype(BF16)
        k_ref[rs, :] = k.astype(BF16)
        v_ref[rs, :] = y[:, D_MODEL + kw:].astype(BF16)


def _qkv_proj(h, mods, layer, row, n1, w, q_gain, k_gain, kw, vw, q_scale, tab,
              kv_rows=None, kv_into=None, kv_row_offset=0):
    n = h.shape[0]
    t = _row_tile(n)
    rotary = tab is not None
    wtot = D_MODEL + kw + vw
    in_specs = [pl.BlockSpec((t, D_MODEL), lambda i: (i, 0)),
                pl.BlockSpec((None, 8, 6 * D_MODEL), lambda i: (layer, 0, 0)),
                _const_spec((1, D_MODEL)),
                _const_spec((D_MODEL, wtot)),
                _const_spec((MXU_TILE, MXU_TILE)),
                _const_spec((1, D_MODEL)),
                _const_spec((1, kw))]
    args = [h, mods, n1.reshape(1, D_MODEL), w, _group_mean_matrix(MXU_TILE),
            jnp.tile(q_gain, D_MODEL // HEAD_DIM).reshape(1, D_MODEL),
            jnp.tile(k_gain, kw // HEAD_DIM).reshape(1, kw)]
    if rotary:
        row_tab, col_tab = tab
        in_specs += [pl.BlockSpec((t // GRID_W, 3 * LANES), lambda i: (i, 0)),
                     _const_spec((t, 3 * LANES))]
        args += [row_tab, col_tab]
    aliases = {}
    off = kv_row_offset // t
    kv_total = n if kv_rows is None else kv_rows
    if kv_into is not None:
        kv_total = kv_into[0].shape[0]
        aliases = {len(args): 1, len(args) + 1: 2}
        in_specs += [pl.BlockSpec(memory_space=pl.ANY)] * 2
        args += list(kv_into)
    return pl.pallas_call(
        functools.partial(_qkv_kernel, row, rotary, kw, q_scale, len(aliases)),
        grid=(n // t,),
        in_specs=in_specs,
        out_specs=[pl.BlockSpec((t, D_MODEL), lambda i: (i, 0)),
                   pl.BlockSpec((t, kw), lambda i: (i + off, 0)),
                   pl.BlockSpec((t, vw), lambda i: (i + off, 0))],
        out_shape=[jax.ShapeDtypeStruct((n, D_MODEL), BF16),
                   jax.ShapeDtypeStruct((kv_total, kw), BF16),
                   jax.ShapeDtypeStruct((kv_total, vw), BF16)],
        input_output_aliases=aliases,
        compiler_params=_cparams(1),
        name="qkv_proj",
    )(*args)


DIFF_TQ = 256
DIFF_TK = 1280


def _diff_attn_kernel(tq, tk, n_tiles, n_chunks, lam_init, q_ref, lamv_ref, sub_ref, k_ref, v_ref,
                      o_ref, qq_ref, s_ref, m_ref, acc_ref):
    lane = lax.broadcasted_iota(jnp.int32, (tq, LANES), 1)
    lv = lamv_ref[...]
    lam = (jnp.exp(jnp.sum(lv[0:1] * lv[1:2], axis=-1, keepdims=True))
           - jnp.exp(jnp.sum(lv[2:3] * lv[3:4], axis=-1, keepdims=True)) + lam_init)
    out_gain = sub_ref[...] * (1.0 - lam_init)

    def rows_of(t):
        return pl.ds(t * tq, tq) if isinstance(t, int) else pl.ds(pl.multiple_of(t * tq, tq), tq)

    def stack_queries(t, slot):
        q = q_ref[rows_of(t), :]
        zero = jnp.zeros_like(q)
        qq_ref[slot, 0:tq, :] = jnp.where(lane < HEAD_DIM, q, zero)
        qq_ref[slot, tq:, :] = jnp.where(lane >= HEAD_DIM, q, zero)

    def scores(slot, c, buf):
        s_ref[buf] = lax.dot_general(qq_ref[slot], k_ref[c * tk:(c + 1) * tk, :],
                                     (((1,), (1,)), ((), ())), preferred_element_type=F32)

    def softmax_pv(c, buf):
        s = s_ref[buf]
        m_prev = m_ref[...]
        m_new = jnp.maximum(m_prev, jnp.max(s, axis=-1, keepdims=True))
        alpha = jnp.exp2(m_prev - m_new)
        p = jnp.exp2(s - jnp.tile(m_new, (1, tk // LANES)))
        v_aug = jnp.concatenate([v_ref[c * tk:(c + 1) * tk, :], jnp.ones((tk, LANES), BF16)], axis=1)
        pv = jnp.dot(p.astype(BF16), v_aug, preferred_element_type=F32)
        acc_ref[...] = acc_ref[...] * jnp.tile(alpha, (1, 2)) + pv
        m_ref[...] = m_new

    def tile(t, t_next, slot, first_buf):
        m_ref[...] = jnp.full(m_ref.shape, NEG_INF, F32)
        acc_ref[...] = jnp.zeros(acc_ref.shape, F32)
        for c in range(n_chunks):
            buf = (first_buf + c) % 2
            if c + 1 < n_chunks:
                scores(slot, c + 1, 1 - buf)
            elif t_next is not None:
                stack_queries(t_next, 1 - slot)
                scores(1 - slot, 0, 1 - buf)
            softmax_pv(c, buf)
        acc = acc_ref[...]
        o = acc[:, :LANES] / acc[:, LANES:LANES + 1]
        d = o[:tq] - lam * o[tq:]
        ms = jnp.mean(d * d, axis=-1, keepdims=True)
        o_ref[rows_of(t), :] = (d * lax.rsqrt(ms + NORM_EPS) * out_gain).astype(BF16)

    stack_queries(0, 0)
    scores(0, 0, 0)
    flip = n_chunks % 2
    if n_tiles % 2 == 0 and n_tiles > 2:
        def pair(tp, carry):
            t0 = 2 * tp
            tile(t0, t0 + 1, 0, 0)
            tile(t0 + 1, jnp.minimum(t0 + 2, n_tiles - 1), 1, flip)
            return carry
        lax.fori_loop(0, n_tiles // 2, pair, 0)
    else:
        for t in range(n_tiles):
            tile(t, t + 1 if t + 1 < n_tiles else None, t % 2, (t * flip) % 2)


def _diff_attn(q, k, v, key_start, n_keys, lamv, subln, lam_init):
    nq = q.shape[0]
    tq = min(DIFF_TQ, nq)
    tk = min(DIFF_TK, n_keys)
    n_chunks = n_keys // tk
    assert nq % tq == 0 and n_keys % tk == 0 and key_start % n_keys == 0
    kb = key_start // n_keys
    return pl.pallas_call(
        functools.partial(_diff_attn_kernel, tq, tk, nq // tq, n_chunks, lam_init),
        grid=(DIFF_HEADS,),
        in_specs=[pl.BlockSpec((nq, LANES), lambda h: (0, h)),
                  _const_spec((8, LANES)),
                  _const_spec((1, LANES)),
                  pl.BlockSpec((n_keys, LANES), lambda h: (kb, h)),
                  pl.BlockSpec((n_keys, LANES), lambda h: (kb, h))],
        out_specs=pl.BlockSpec((nq, LANES), lambda h: (0, h)),
        out_shape=jax.ShapeDtypeStruct((nq, D_MODEL), BF16),
        scratch_shapes=[pltpu.VMEM((2, 2 * tq, LANES), BF16),
                        pltpu.VMEM((2, 2 * tq, tk), F32),
                        pltpu.VMEM((2 * tq, LANES), F32),
                        pltpu.VMEM((2 * tq, 2 * LANES), F32)],
        compiler_params=_cparams(1),
        name="diff_attn",
    )(q, lamv, subln.reshape(1, LANES), k, v)


def _win_attn_kernel(tq, n_lat, with_window, *refs):
    if with_window:
        (q_ref, kp_ref, kc_ref, kn_ref, vp_ref, vc_ref, vn_ref,
         kx_ref, vx_ref, sink_ref, o_ref) = refs
    else:
        q_ref, kx_ref, vx_ref, sink_ref, o_ref = refs
    i = pl.program_id(0)
    n_ctx = kx_ref.shape[0]
    sub = WINDOW if with_window else tq
    n_win = 3 * WINDOW if with_window else 0
    n_keys = n_win + n_ctx
    rows = WIN_GROUP * sub
    low = lax.broadcasted_iota(jnp.int32, (sub, LANES), 1) < HEAD_DIM
    ones = jnp.ones((n_keys, LANES), BF16)
    nt = (((1,), (1,)), ((), ()))
    for sb in range(tq // sub):
        r0 = sb * sub
        if with_window:
            qpos = i * tq + r0 + lax.broadcasted_iota(jnp.int32, (sub, n_win), 0)
            kpos = i * tq + r0 - WINDOW + lax.broadcasted_iota(jnp.int32, (sub, n_win), 1)
            valid = (jnp.abs(kpos - qpos) <= WINDOW) & (kpos >= 0) & (kpos < n_lat)
            bias = jnp.where(valid, 0.0, NEG_INF)
            bias = jnp.concatenate([bias] * WIN_GROUP, axis=0)
        for kvh in range(WIN_KV_HEADS):
            ks = slice(kvh * LANES, (kvh + 1) * LANES)
            heads = [kvh * WIN_GROUP + g for g in range(WIN_GROUP)]
            blocks = [kvh * (WIN_GROUP // 2) + pr for pr in range(WIN_GROUP // 2)]
            qs = []
            for blk in blocks:
                qp = q_ref[r0:r0 + sub, blk * LANES:(blk + 1) * LANES]
                zero = jnp.zeros_like(qp)
                qs += [jnp.where(low, qp, zero), jnp.where(low, zero, qp)]
            qq = jnp.concatenate(qs, axis=0)
            kk, vv = kx_ref[:, ks], vx_ref[:, ks]
            if with_window:
                w0 = sb * WINDOW
                k_span = jnp.concatenate([kp_ref[:, ks], kc_ref[:, ks], kn_ref[:, ks]], axis=0)
                v_span = jnp.concatenate([vp_ref[:, ks], vc_ref[:, ks], vn_ref[:, ks]], axis=0)
                kk = jnp.concatenate([k_span[w0:w0 + n_win], kk], axis=0)
                vv = jnp.concatenate([v_span[w0:w0 + n_win], vv], axis=0)
            s = lax.dot_general(qq, kk, nt, preferred_element_type=F32)
            if with_window:
                s = jnp.concatenate([s[:, :n_win] + bias, s[:, n_win:]], axis=1)
            sink = jnp.concatenate(
                [jnp.broadcast_to(sink_ref[hd:hd + 1, :], (sub, LANES)) for hd in heads], axis=0)
            m = jnp.maximum(jnp.max(s, axis=-1, keepdims=True), sink)
            p = jnp.exp(s - jnp.tile(m, (1, n_keys // LANES)))
            pv = jnp.dot(p.astype(BF16), jnp.concatenate([vv, ones], axis=1),
                         preferred_element_type=F32)
            den = pv[:, LANES:LANES + 1] + jnp.exp(sink[:, 0:1] - m[:, 0:1])
            o = pv[:, :LANES] / den
            for pr, blk in enumerate(blocks):
                lo = o[(2 * pr) * sub:(2 * pr + 1) * sub]
                hi = o[(2 * pr + 1) * sub:(2 * pr + 2) * sub]
                o_ref[r0:r0 + sub, blk * LANES:(blk + 1) * LANES] = jnp.where(low, lo, hi).astype(BF16)


def _win_attn(q, k, v, kx, vx, sink_rows, with_window):
    nq = q.shape[0]
    tq = min(256, nq)
    n_ctx = kx.shape[0]
    kvw = WIN_KV_HEADS * LANES
    in_specs = [pl.BlockSpec((tq, D_MODEL), lambda i: (i, 0))]
    args = [q]
    if with_window:
        r = tq // WINDOW
        last = nq // WINDOW - 1
        prev_spec = pl.BlockSpec((WINDOW, kvw), lambda i: (jnp.maximum(i * r - 1, 0), 0))
        cur_spec = pl.BlockSpec((tq, kvw), lambda i: (i, 0))
        next_spec = pl.BlockSpec((WINDOW, kvw), lambda i: (jnp.minimum((i + 1) * r, last), 0))
        in_specs += [prev_spec, cur_spec, next_spec] * 2
        args += [k, k, k, v, v, v]
    in_specs += [_const_spec((n_ctx, kvw)), _const_spec((n_ctx, kvw)),
                 _const_spec((WIN_Q_HEADS, LANES))]
    args += [kx, vx, sink_rows]
    return pl.pallas_call(
        functools.partial(_win_attn_kernel, tq, nq, with_window),
        grid=(nq // tq,),
        in_specs=in_specs,
        out_specs=pl.BlockSpec((tq, D_MODEL), lambda i: (i, 0)),
        out_shape=jax.ShapeDtypeStruct((nq, D_MODEL), BF16),
        compiler_params=_cparams(1),
        name="win_attn",
    )(*args)


def _out_res_kernel(row, has_bias, *refs):
    if has_bias:
        x_ref, h_ref, mod_ref, w_ref, b_ref, o_ref = refs
    else:
        x_ref, h_ref, mod_ref, w_ref, o_ref = refs
    y = jnp.dot(x_ref[...], w_ref[...], preferred_element_type=F32)
    if has_bias:
        y = y + b_ref[...]
    o_ref[...] = h_ref[...] + _mod(mod_ref, row, G1) * y


def _out_res(x, h, mods, layer, row, w, bias=None):
    n = h.shape[0]
    t = _row_tile(n)
    in_specs = [pl.BlockSpec((t, D_MODEL), lambda i: (i, 0)),
                pl.BlockSpec((t, D_MODEL), lambda i: (i, 0)),
                pl.BlockSpec((None, 8, 6 * D_MODEL), lambda i: (layer, 0, 0)),
                _const_spec((D_MODEL, D_MODEL))]
    args = [x, h, mods, w]
    if bias is not None:
        in_specs.append(_const_spec((1, D_MODEL)))
        args.append(bias.reshape(1, D_MODEL))
    return pl.pallas_call(
        functools.partial(_out_res_kernel, row, bias is not None),
        grid=(n // t,),
        in_specs=in_specs,
        out_specs=pl.BlockSpec((t, D_MODEL), lambda i: (i, 0)),
        out_shape=jax.ShapeDtypeStruct((n, D_MODEL), F32),
        compiler_params=_cparams(1),
        name="out_res",
    )(*args)


FFN_CHUNK = D_FF // 2


def _ffn_kernel(row, h_ref, mod_ref, n2_ref, wgu_ref, wd_ref, o_ref):
    h = h_ref[...]
    a = _mod_norm(h, n2_ref[...], _mod(mod_ref, row, SC2), _mod(mod_ref, row, SH2)).astype(BF16)
    acc = None
    for c in range(D_FF // FFN_CHUNK):
        lo = c * FFN_CHUNK
        g = jnp.dot(a, wgu_ref[:, lo:lo + FFN_CHUNK], preferred_element_type=F32)
        u = jnp.dot(a, wgu_ref[:, D_FF + lo:D_FF + lo + FFN_CHUNK], preferred_element_type=F32)
        act = (_silu(g) * u).astype(BF16)
        part = jnp.dot(act, wd_ref[lo:lo + FFN_CHUNK, :], preferred_element_type=F32)
        acc = part if acc is None else acc + part
    o_ref[...] = h + _mod(mod_ref, row, G2) * acc


def _ffn(h, mods, layer, row, n2, wgu, wd):
    n = h.shape[0]
    t = _row_tile(n)
    return pl.pallas_call(
        functools.partial(_ffn_kernel, row),
        grid=(n // t,),
        in_specs=[pl.BlockSpec((t, D_MODEL), lambda i: (i, 0)),
                  pl.BlockSpec((None, 8, 6 * D_MODEL), lambda i: (layer, 0, 0)),
                  _const_spec((1, D_MODEL)),
                  _const_spec((D_MODEL, 2 * D_FF)),
                  _const_spec((D_FF, D_MODEL))],
        out_specs=pl.BlockSpec((t, D_MODEL), lambda i: (i, 0)),
        out_shape=jax.ShapeDtypeStruct((n, D_MODEL), F32),
        compiler_params=_cparams(1),
        name="ffn",
    )(h, mods, n2.reshape(1, D_MODEL), wgu, wd)


def _in_proj_kernel(row, mode, *refs):
    if mode == "short":
        h_ref, mod_ref, n1_ref, w_ref, b_out, cu_out = refs
    else:
        h_ref, mod_ref, n1_ref, w_ref, bias_ref, glu_out = refs
    a = _mod_norm(h_ref[...], n1_ref[...], _mod(mod_ref, row, SC1), _mod(mod_ref, row, SH1))
    y = jnp.dot(a.astype(BF16), w_ref[...], preferred_element_type=F32)
    if mode == "short":
        b_out[...] = y[:, :D_MODEL]
        cu_out[...] = y[:, D_MODEL:2 * D_MODEL] * y[:, 2 * D_MODEL:]
    else:
        y = y + bias_ref[...]
        g = y[:, D_MODEL:]
        glu_out[...] = y[:, :D_MODEL] * (1.0 / (1.0 + jnp.exp(-g)))


def _in_proj(h, mods, layer, row, n1, w, mode, bias=None):
    n = h.shape[0]
    t = _row_tile(n)
    wtot = w.shape[1]
    in_specs = [pl.BlockSpec((t, D_MODEL), lambda i: (i, 0)),
                pl.BlockSpec((None, 8, 6 * D_MODEL), lambda i: (layer, 0, 0)),
                _const_spec((1, D_MODEL)),
                _const_spec((D_MODEL, wtot))]
    args = [h, mods, n1.reshape(1, D_MODEL), w]
    row_spec = pl.BlockSpec((t, D_MODEL), lambda i: (i, 0))
    row_shape = jax.ShapeDtypeStruct((n, D_MODEL), F32)
    if mode == "short":
        out_specs, out_shape = [row_spec, row_spec], [row_shape, row_shape]
    else:
        in_specs.append(_const_spec((1, wtot)))
        args.append(bias.reshape(1, wtot))
        out_specs, out_shape = row_spec, row_shape
    return pl.pallas_call(
        functools.partial(_in_proj_kernel, row, mode),
        grid=(n // t,),
        in_specs=in_specs,
        out_specs=out_specs,
        out_shape=out_shape,
        compiler_params=_cparams(1),
        name="in_proj_" + mode,
    )(*args)


CONV_ROW_BLOCK = 64
SUBLANES = 8


def _conv_shifts(taps):
    return [divmod(CONV_HALO - taps // 2 + k, SUBLANES) for k in range(taps)]


def _conv_rems(taps):
    return sorted({r for _, r in _conv_shifts(taps) if r})


def _conv_span(taps, t):
    return t + SUBLANES * max(a for a, _ in _conv_shifts(taps))


def _dwconv_tile(xs_ref, sh_ref, u_ref, prev_ref, cur_ref, next_ref, w_ref, taps, t):
    i = pl.program_id(0)
    last = pl.num_programs(0) - 1
    ncb = D_MODEL // LANES
    for cb in range(ncb):
        cs = slice(cb * LANES, (cb + 1) * LANES)
        xs_ref[cb, 0:CONV_HALO, :] = jnp.where(i > 0, prev_ref[:, cs], 0.0)
        xs_ref[cb, CONV_HALO:CONV_HALO + t, :] = cur_ref[:, cs]
        xs_ref[cb, CONV_HALO + t:, :] = jnp.where(i < last, next_ref[:, cs], 0.0)
    shifts = _conv_shifts(taps)
    rems = _conv_rems(taps)
    span = _conv_span(taps, t)
    for j, r in enumerate(rems):
        sh_ref[j] = xs_ref[:, r:r + span, :]

    def block(idx, carry):
        rb, cb = idx // ncb, idx % ncb
        row0 = pl.multiple_of(rb * CONV_ROW_BLOCK, CONV_ROW_BLOCK)
        accs = [None, None]
        for k, (a, r) in enumerate(shifts):
            rows = pl.ds(row0 + a * SUBLANES, CONV_ROW_BLOCK)
            x = xs_ref[cb, rows, :] if r == 0 else sh_ref[rems.index(r), cb, rows, :]
            term = x * w_ref[cb, k:k + 1, :]
            accs[k % 2] = term if accs[k % 2] is None else accs[k % 2] + term
        u_ref[cb, pl.ds(row0, CONV_ROW_BLOCK), :] = accs[0] + accs[1]
        return carry

    lax.fori_loop(0, (t // CONV_ROW_BLOCK) * ncb, block, 0)
    return jnp.concatenate([u_ref[cb] for cb in range(ncb)], axis=1)


def _conv_out_kernel(row, mode, taps, t, *refs):
    if mode == "short":
        (prev_ref, cur_ref, next_ref, cw_ref, b_ref, h_ref, mod_ref, w_ref, o_ref,
         xs_ref, sh_ref, u_ref) = refs
    else:
        (prev_ref, cur_ref, next_ref, cw_ref, dwb_ref, lng_ref, lnb_ref, h_ref, mod_ref,
         w_ref, pb_ref, o_ref, xs_ref, sh_ref, u_ref) = refs
    u = _dwconv_tile(xs_ref, sh_ref, u_ref, prev_ref, cur_ref, next_ref, cw_ref, taps, t)
    if mode == "short":
        y = jnp.dot((b_ref[...] * u).astype(BF16), w_ref[...], preferred_element_type=F32)
    else:
        u = u + dwb_ref[...]
        mu = jnp.mean(u, axis=-1, keepdims=True)
        uc = u - mu
        var = jnp.mean(uc * uc, axis=-1, keepdims=True)
        z = _silu(uc * lax.rsqrt(var + NORM_EPS) * lng_ref[...] + lnb_ref[...])
        y = jnp.dot(z.astype(BF16), w_ref[...], preferred_element_type=F32) + pb_ref[...]
    o_ref[...] = h_ref[...] + _mod(mod_ref, row, G1) * y


def _conv_out(x, h, mods, layer, row, conv_w, w, mode, extra):
    n = h.shape[0]
    t = _row_tile(n)
    taps = conv_w.shape[0]
    r = t // CONV_HALO
    last = n // CONV_HALO - 1
    taps_pad = -(-taps // 8) * 8
    ncb = D_MODEL // LANES
    cw = jnp.zeros((taps_pad, D_MODEL), F32).at[:taps].set(conv_w)
    cw = cw.reshape(taps_pad, ncb, LANES).transpose(1, 0, 2)
    row_spec = pl.BlockSpec((t, D_MODEL), lambda i: (i, 0))
    vec_spec = _const_spec((1, D_MODEL))
    in_specs = [pl.BlockSpec((CONV_HALO, D_MODEL), lambda i: (jnp.maximum(i * r - 1, 0), 0)),
                row_spec,
                pl.BlockSpec((CONV_HALO, D_MODEL), lambda i: (jnp.minimum((i + 1) * r, last), 0)),
                _const_spec((ncb, taps_pad, LANES))]
    args = [x, x, x, cw]
    mod_spec = pl.BlockSpec((None, 8, 6 * D_MODEL), lambda i: (layer, 0, 0))
    if mode == "short":
        (b_gate,) = extra
        in_specs += [row_spec, row_spec, mod_spec, _const_spec((D_MODEL, D_MODEL))]
        args += [b_gate, h, mods, w]
    else:
        dw_b, ln_g, ln_b, pw_b = extra
        in_specs += [vec_spec, vec_spec, vec_spec, row_spec, mod_spec,
                     _const_spec((D_MODEL, D_MODEL)), vec_spec]
        args += [dw_b.reshape(1, D_MODEL), ln_g.reshape(1, D_MODEL), ln_b.reshape(1, D_MODEL),
                 h, mods, w, pw_b.reshape(1, D_MODEL)]
    return pl.pallas_call(
        functools.partial(_conv_out_kernel, row, mode, taps, t),
        grid=(n // t,),
        in_specs=in_specs,
        out_specs=row_spec,
        out_shape=jax.ShapeDtypeStruct((n, D_MODEL), F32),
        scratch_shapes=[pltpu.VMEM((ncb, t + 2 * CONV_HALO, LANES), F32),
                        pltpu.VMEM((len(_conv_rems(taps)), ncb, _conv_span(taps, t), LANES), F32),
                        pltpu.VMEM((ncb, t, LANES), F32)],
        compiler_params=_cparams(1),
        name="conv_out_" + mode,
    )(*args)


def _diff_qkv_weight(w_qkv):
    idx = np.arange(D_MODEL).reshape(2, DIFF_HEADS, HEAD_DIM).transpose(1, 0, 2).reshape(-1)
    cols = np.concatenate([idx, D_MODEL + idx, 2 * D_MODEL + np.arange(D_MODEL)])
    return w_qkv[:, cols].astype(BF16)


def _win_qkv_weight(w_qkv):
    qw = WIN_Q_HEADS * HEAD_DIM
    kvw = WIN_KV_HEADS * HEAD_DIM
    dup = np.repeat(np.arange(kvw).reshape(WIN_KV_HEADS, 1, HEAD_DIM), 2, axis=1).reshape(-1)
    cols = np.concatenate([np.arange(qw), qw + dup, qw + kvw + dup])
    return w_qkv[:, cols].astype(BF16)


def kernel(x, c, ctx, c_ctx, ada_w, ada_b, norm1, norm2, ffn_w_gate_up, ffn_w_down, diff_w_qkv, diff_w_o, diff_q_norm, diff_k_norm, diff_lam_q1, diff_lam_k1, diff_lam_q2, diff_lam_k2, diff_subln, sc_w_in, sc_conv_w, sc_w_out, win_w_qkv, win_w_o, win_q_norm, win_k_norm, win_sink, cf_w_pw1, cf_b_pw1, cf_dw_w, cf_dw_b, cf_ln_g, cf_ln_b, cf_w_pw2, cf_b_pw2):
    n = x.shape[1]
    h, hc = x.reshape(n, D_MODEL), ctx.reshape(ctx.shape[1], D_MODEL)
    mods = _ada_mod(c, c_ctx, ada_w, ada_b)
    tab = _rope_tables(n, _row_tile(n))
    for i in range(DEPTH):
        kind, j = i % N_MIXERS, i // N_MIXERS
        with_ctx = i < DEPTH - 1
        if kind == 0:
            lam_init = 0.8 - 0.6 * math.exp(-0.3 * i)
            w = _diff_qkv_weight(diff_w_qkv[j])
            w_o = diff_w_o[j].astype(BF16)
            lamv = jnp.zeros((8, LANES), F32).at[0:4, 0:HEAD_DIM].set(
                jnp.stack([diff_lam_q1[j], diff_lam_k1[j], diff_lam_q2[j], diff_lam_k2[j]]))
            q_scale = HEAD_DIM ** -0.5 * LOG2E
            proj = functools.partial(_qkv_proj, mods=mods, layer=i, n1=norm1[i], w=w,
                                     q_gain=diff_q_norm[j], k_gain=diff_k_norm[j],
                                     kw=D_MODEL, vw=D_MODEL, q_scale=q_scale)
            n_ctx = hc.shape[0]
            q_l, k_all, v_all = proj(h, row=0, tab=tab, kv_rows=n + n_ctx)
            q_c, k_all, v_all = proj(hc, row=1, tab=None, kv_into=(k_all, v_all), kv_row_offset=n)
            o_l = _diff_attn(q_l, k_all, v_all, 0, n + n_ctx, lamv, diff_subln[j], lam_init)
            h = _out_res(o_l, h, mods, i, 0, w_o)
            if with_ctx:
                o_c = _diff_attn(q_c, k_all, v_all, n, n_ctx, lamv, diff_subln[j], lam_init)
                hc = _out_res(o_c, hc, mods, i, 1, w_o)
        elif kind == 1:
            w_in = sc_w_in[j].astype(BF16)
            w_out = sc_w_out[j].astype(BF16)
            b_l, cu_l = _in_proj(h, mods, i, 0, norm1[i], w_in, "short")
            h = _conv_out(cu_l, h, mods, i, 0, sc_conv_w[j], w_out, "short", (b_l,))
            if with_ctx:
                b_c, cu_c = _in_proj(hc, mods, i, 1, norm1[i], w_in, "short")
                hc = _conv_out(cu_c, hc, mods, i, 1, sc_conv_w[j], w_out, "short", (b_c,))
        elif kind == 2:
            w = _win_qkv_weight(win_w_qkv[j])
            w_o = win_w_o[j].astype(BF16)
            kvw = WIN_KV_HEADS * LANES
            sink_rows = jnp.broadcast_to(win_sink[j][:, None], (WIN_Q_HEADS, LANES))
            proj = functools.partial(_qkv_proj, mods=mods, layer=i, n1=norm1[i], w=w,
                                     q_gain=win_q_norm[j], k_gain=win_k_norm[j],
                                     kw=kvw, vw=kvw, q_scale=HEAD_DIM ** -0.5)
            q_l, k_l, v_l = proj(h, row=0, tab=tab)
            q_c, k_c, v_c = proj(hc, row=1, tab=None)
            o_l = _win_attn(q_l, k_l, v_l, k_c, v_c, sink_rows, True)
            h = _out_res(o_l, h, mods, i, 0, w_o)
            if with_ctx:
                o_c = _win_attn(q_c, None, None, k_c, v_c, sink_rows, False)
                hc = _out_res(o_c, hc, mods, i, 1, w_o)
        else:
            w1 = cf_w_pw1[j].astype(BF16)
            w2 = cf_w_pw2[j].astype(BF16)
            extra = (cf_dw_b[j], cf_ln_g[j], cf_ln_b[j], cf_b_pw2[j])
            glu_l = _in_proj(h, mods, i, 0, norm1[i], w1, "glu", cf_b_pw1[j])
            h = _conv_out(glu_l, h, mods, i, 0, cf_dw_w[j], w2, "conf", extra)
            if with_ctx:
                glu_c = _in_proj(hc, mods, i, 1, norm1[i], w1, "glu", cf_b_pw1[j])
                hc = _conv_out(glu_c, hc, mods, i, 1, cf_dw_w[j], w2, "conf", extra)
        wgu = ffn_w_gate_up[i].astype(BF16)
        wd = ffn_w_down[i].astype(BF16)
        h = _ffn(h, mods, i, 0, norm2[i], wgu, wd)
        if with_ctx:
            hc = _ffn(hc, mods, i, 1, norm2[i], wgu, wd)
    return h.reshape(1, n, D_MODEL)
```

```python
import functools
import math

import jax
import jax.numpy as jnp
import numpy as np
from jax import lax
from jax.experimental import pallas as pl
from jax.experimental.pallas import tpu as pltpu

F32 = jnp.float32
BF16 = jnp.bfloat16

D_MODEL = 1024
DEPTH = 4
N_MIXERS = 4
GRID_W = 64
NORM_EPS = 1e-6
ROPE_THETA = 10000.0
HEAD_DIM = 64
NEG_INF = -1e30
DIFF_HEADS = D_MODEL // (2 * HEAD_DIM)
SHORT_CONV_W = 3
WIN_Q_HEADS = D_MODEL // HEAD_DIM
WIN_KV_HEADS = 4
WIN_GROUP = WIN_Q_HEADS // WIN_KV_HEADS
WINDOW = 128
CONF_CONV_W = 31
D_FF = -(-(8 * D_MODEL) // (3 * 256)) * 256

LANES = 128
MXU_TILE = 256
CONV_HALO = 16
VMEM_LIMIT_BYTES = 56 * 1024 * 1024

LOG2E = math.log2(math.e)
SH1, SC1, G1, SH2, SC2, G2 = range(6)


def _cparams(n_axes):
    return pltpu.CompilerParams(dimension_semantics=("parallel",) * n_axes,
                                vmem_limit_bytes=VMEM_LIMIT_BYTES)


def _const_spec(shape):
    nd = len(shape)
    return pl.BlockSpec(shape, lambda *_: (0,) * nd, pipeline_mode=pl.Buffered(1))


def _mod(mod_ref, row, idx):
    return mod_ref[row:row + 1, idx * D_MODEL:(idx + 1) * D_MODEL]


def _mod_norm(h, gain, scale, shift):
    ms = jnp.mean(h * h, axis=-1, keepdims=True)
    return (h * lax.rsqrt(ms + NORM_EPS)) * (gain * (1.0 + scale)) + shift


def _silu(x):
    return x * (1.0 / (1.0 + jnp.exp(-x)))


def _row_tile(n):
    return 512 if n % 512 == 0 else n


def _ada_kernel(act_ref, w_ref, b_ref, o_ref):
    a = _silu(act_ref[...])
    o_ref[...] = jnp.dot(a.astype(BF16), w_ref[...].astype(BF16),
                         preferred_element_type=F32) + b_ref[...]


def _ada_mod(c, c_ctx, ada_w, ada_b):
    act = jnp.zeros((8, D_MODEL), F32).at[0].set(c[0]).at[1].set(c_ctx)
    cols = 1536
    nc = 6 * D_MODEL // cols
    return pl.pallas_call(
        _ada_kernel,
        grid=(DEPTH, nc),
        in_specs=[pl.BlockSpec((8, D_MODEL), lambda i, j: (0, 0)),
                  pl.BlockSpec((None, D_MODEL, cols), lambda i, j: (i, 0, j)),
                  pl.BlockSpec((None, 1, cols), lambda i, j: (i, 0, j))],
        out_specs=pl.BlockSpec((None, 8, cols), lambda i, j: (i, 0, j)),
        out_shape=jax.ShapeDtypeStruct((DEPTH, 8, 6 * D_MODEL), F32),
        compiler_params=_cparams(2),
        name="ada_mod",
    )(act, ada_w, ada_b.reshape(DEPTH, 1, 6 * D_MODEL))


def _group_mean_matrix(width):
    g = np.arange(width) // HEAD_DIM
    return jnp.asarray((g[:, None] == g[None, :]).astype(np.float32) / HEAD_DIM, BF16)


def _rope_tables(n, t):
    n_freq = HEAD_DIM // 4
    inv_freq = ROPE_THETA ** (-jnp.arange(n_freq, dtype=F32) / n_freq)
    reps = LANES // HEAD_DIM

    def tables(count, is_row):
        ang = jnp.arange(count, dtype=F32)[:, None] * inv_freq[None, :]
        cos, sin, zero = jnp.cos(ang), jnp.sin(ang), jnp.zeros_like(ang)
        halves = [[cos, cos], [-sin, zero], [zero, sin]]
        out = []
        for first, second in halves:
            pair = [first, second, zero, zero] if is_row else [zero, zero, first, second]
            out.append(jnp.tile(jnp.concatenate(pair, axis=-1), (1, reps)))
        return jnp.concatenate(out, axis=-1)

    return tables(n // GRID_W, True), jnp.tile(tables(GRID_W, False), (t // GRID_W, 1))


def _rope(x, tab):
    w = x.shape[-1]
    reps = w // LANES
    cos = jnp.tile(tab[:, 0:LANES], (1, reps))
    sin_up = jnp.tile(tab[:, LANES:2 * LANES], (1, reps))
    sin_dn = jnp.tile(tab[:, 2 * LANES:3 * LANES], (1, reps))
    quarter = HEAD_DIM // 4
    return (x * cos + pltpu.roll(x, w - quarter, 1) * sin_up
            + pltpu.roll(x, quarter, 1) * sin_dn)


QKV_SUB_ROWS = 256


def _qkv_kernel(row, rotary, kw, q_scale, n_alias, *refs):
    refs = list(refs)
    del refs[-3 - n_alias:-3]
    if rotary:
        (h_ref, mod_ref, n1_ref, w_ref, gm_ref, qg_ref, kg_ref, rt_ref, ct_ref,
         q_ref, k_ref, v_ref) = refs
    else:
        h_ref, mod_ref, n1_ref, w_ref, gm_ref, qg_ref, kg_ref, q_ref, k_ref, v_ref = refs
    gm = gm_ref[...]

    def head_norm(x, gain, width):
        x2 = (x * x).astype(BF16)
        ms = jnp.concatenate(
            [jnp.dot(x2[:, c:c + MXU_TILE], gm, preferred_element_type=F32)
             for c in range(0, width, MXU_TILE)], axis=1)
        return x * lax.rsqrt(ms + NORM_EPS) * gain

    t = h_ref.shape[0]
    sub = min(QKV_SUB_ROWS, t)
    for sb in range(t // sub):
        rs = slice(sb * sub, (sb + 1) * sub)
        a = _mod_norm(h_ref[rs, :], n1_ref[...], _mod(mod_ref, row, SC1), _mod(mod_ref, row, SH1))
        y = jnp.dot(a.astype(BF16), w_ref[...], preferred_element_type=F32)
        q = head_norm(y[:, :D_MODEL], qg_ref[...], D_MODEL)
        k = head_norm(y[:, D_MODEL:D_MODEL + kw], kg_ref[...], kw)
        if rotary:
            g0 = sb * sub // GRID_W
            row_part = jnp.concatenate(
                [jnp.broadcast_to(rt_ref[g0 + r:g0 + r + 1, :], (GRID_W, 3 * LANES))
                 for r in range(sub // GRID_W)], axis=0)
            tab = row_part + ct_ref[rs, :]
            q, k = _rope(q, tab), _rope(k, tab)
        q_ref[rs, :] = (q * q_scale).astype(BF16)
        k_ref[rs, :] = k.astype(BF16)
        v_ref[rs, :] = y[:, D_MODEL + kw:].astype(BF16)


def _qkv_proj(h, mods, layer, row, n1, w, q_gain, k_gain, kw, vw, q_scale, tab,
              kv_rows=None, kv_into=None, kv_row_offset=0):
    n = h.shape[0]
    t = _row_tile(n)
    rotary = tab is not None
    wtot = D_MODEL + kw + vw
    in_specs = [pl.BlockSpec((t, D_MODEL), lambda i: (i, 0)),
                pl.BlockSpec((None, 8, 6 * D_MODEL), lambda i: (layer, 0, 0)),
                _const_spec((1, D_MODEL)),
                _const_spec((D_MODEL, wtot)),
                _const_spec((MXU_TILE, MXU_TILE)),
                _const_spec((1, D_MODEL)),
                _const_spec((1, kw))]
    args = [h, mods, n1.reshape(1, D_MODEL), w, _group_mean_matrix(MXU_TILE),
            jnp.tile(q_gain, D_MODEL // HEAD_DIM).reshape(1, D_MODEL),
            jnp.tile(k_gain, kw // HEAD_DIM).reshape(1, kw)]
    if rotary:
        row_tab, col_tab = tab
        in_specs += [pl.BlockSpec((t // GRID_W, 3 * LANES), lambda i: (i, 0)),
                     _const_spec((t, 3 * LANES))]
        args += [row_tab, col_tab]
    aliases = {}
    off = kv_row_offset // t
    kv_total = n if kv_rows is None else kv_rows
    if kv_into is not None:
        kv_total = kv_into[0].shape[0]
        aliases = {len(args): 1, len(args) + 1: 2}
        in_specs += [pl.BlockSpec(memory_space=pl.ANY)] * 2
        args += list(kv_into)
    return pl.pallas_call(
        functools.partial(_qkv_kernel, row, rotary, kw, q_scale, len(aliases)),
        grid=(n // t,),
        in_specs=in_specs,
        out_specs=[pl.BlockSpec((t, D_MODEL), lambda i: (i, 0)),
                   pl.BlockSpec((t, kw), lambda i: (i + off, 0)),
                   pl.BlockSpec((t, vw), lambda i: (i + off, 0))],
        out_shape=[jax.ShapeDtypeStruct((n, D_MODEL), BF16),
                   jax.ShapeDtypeStruct((kv_total, kw), BF16),
                   jax.ShapeDtypeStruct((kv_total, vw), BF16)],
        input_output_aliases=aliases,
        compiler_params=_cparams(1),
        name="qkv_proj",
    )(*args)


DIFF_TQ = 256
DIFF_TK = 1280


def _diff_attn_kernel(tq, tk, n_tiles, n_chunks, lam_init, q_ref, lamv_ref, sub_ref, k_ref, v_ref,
                      o_ref, qq_ref, s_ref, m_ref, acc_ref):
    lane = lax.broadcasted_iota(jnp.int32, (tq, LANES), 1)
    lv = lamv_ref[...]
    lam = (jnp.exp(jnp.sum(lv[0:1] * lv[1:2], axis=-1, keepdims=True))
           - jnp.exp(jnp.sum(lv[2:3] * lv[3:4], axis=-1, keepdims=True)) + lam_init)
    out_gain = sub_ref[...] * (1.0 - lam_init)

    def rows_of(t):
        return pl.ds(t * tq, tq) if isinstance(t, int) else pl.ds(pl.multiple_of(t * tq, tq), tq)

    def stack_queries(t, slot):
        q = q_ref[rows_of(t), :]
        zero = jnp.zeros_like(q)
        qq_ref[slot, 0:tq, :] = jnp.where(lane < HEAD_DIM, q, zero)
        qq_ref[slot, tq:, :] = jnp.where(lane >= HEAD_DIM, q, zero)

    def scores(slot, c, buf):
        s_ref[buf] = lax.dot_general(qq_ref[slot], k_ref[c * tk:(c + 1) * tk, :],
                                     (((1,), (1,)), ((), ())), preferred_element_type=F32)

    def softmax_pv(c, buf):
        s = s_ref[buf]
        m_prev = m_ref[...]
        m_new = jnp.maximum(m_prev, jnp.max(s, axis=-1, keepdims=True))
        alpha = jnp.exp2(m_prev - m_new)
        p = jnp.exp2(s - jnp.tile(m_new, (1, tk // LANES)))
        v_aug = jnp.concatenate([v_ref[c * tk:(c + 1) * tk, :], jnp.ones((tk, LANES), BF16)], axis=1)
        pv = jnp.dot(p.astype(BF16), v_aug, preferred_element_type=F32)
        acc_ref[...] = acc_ref[...] * jnp.tile(alpha, (1, 2)) + pv
        m_ref[...] = m_new

    def tile(t, t_next, slot, first_buf):
        m_ref[...] = jnp.full(m_ref.shape, NEG_INF, F32)
        acc_ref[...] = jnp.zeros(acc_ref.shape, F32)
        for c in range(n_chunks):
            buf = (first_buf + c) % 2
            if c + 1 < n_chunks:
                scores(slot, c + 1, 1 - buf)
            elif t_next is not None:
                stack_queries(t_next, 1 - slot)
                scores(1 - slot, 0, 1 - buf)
            softmax_pv(c, buf)
        acc = acc_ref[...]
        o = acc[:, :LANES] / acc[:, LANES:LANES + 1]
        d = o[:tq] - lam * o[tq:]
        ms = jnp.mean(d * d, axis=-1, keepdims=True)
        o_ref[rows_of(t), :] = (d * lax.rsqrt(ms + NORM_EPS) * out_gain).astype(BF16)

    stack_queries(0, 0)
    scores(0, 0, 0)
    flip = n_chunks % 2
    if n_tiles % 2 == 0 and n_tiles > 2:
        def pair(tp, carry):
            t0 = 2 * tp
            tile(t0, t0 + 1, 0, 0)
            tile(t0 + 1, jnp.minimum(t0 + 2, n_tiles - 1), 1, flip)
            return carry
        lax.fori_loop(0, n_tiles // 2, pair, 0)
    else:
        for t in range(n_tiles):
            tile(t, t + 1 if t + 1 < n_tiles else None, t % 2, (t * flip) % 2)


def _diff_attn(q, k, v, key_start, n_keys, lamv, subln, lam_init):
    nq = q.shape[0]
    tq = min(DIFF_TQ, nq)
    tk = min(DIFF_TK, n_keys)
    n_chunks = n_keys // tk
    assert nq % tq == 0 and n_keys % tk == 0 and key_start % n_keys == 0
    kb = key_start // n_keys
    return pl.pallas_call(
        functools.partial(_diff_attn_kernel, tq, tk, nq // tq, n_chunks, lam_init),
        grid=(DIFF_HEADS,),
        in_specs=[pl.BlockSpec((nq, LANES), lambda h: (0, h)),
                  _const_spec((8, LANES)),
                  _const_spec((1, LANES)),
                  pl.BlockSpec((n_keys, LANES), lambda h: (kb, h)),
                  pl.BlockSpec((n_keys, LANES), lambda h: (kb, h))],
        out_specs=pl.BlockSpec((nq, LANES), lambda h: (0, h)),
        out_shape=jax.ShapeDtypeStruct((nq, D_MODEL), BF16),
        scratch_shapes=[pltpu.VMEM((2, 2 * tq, LANES), BF16),
                        pltpu.VMEM((2, 2 * tq, tk), F32),
                        pltpu.VMEM((2 * tq, LANES), F32),
                        pltpu.VMEM((2 * tq, 2 * LANES), F32)],
        compiler_params=_cparams(1),
        name="diff_attn",
    )(q, lamv, subln.reshape(1, LANES), k, v)


def _win_attn_kernel(tq, n_lat, with_window, *refs):
    if with_window:
        (q_ref, kp_ref, kc_ref, kn_ref, vp_ref, vc_ref, vn_ref,
         kx_ref, vx_ref, sink_ref, o_ref, s_ref) = refs
    else:
        q_ref, kx_ref, vx_ref, sink_ref, o_ref, s_ref = refs
    i = pl.program_id(0)
    n_ctx = kx_ref.shape[0]
    sub = WINDOW if with_window else tq
    n_win = 3 * WINDOW if with_window else 0
    n_keys = n_win + n_ctx
    low = lax.broadcasted_iota(jnp.int32, (sub, LANES), 1) < HEAD_DIM
    ones = jnp.ones((n_keys, LANES), BF16)
    nt = (((1,), (1,)), ((), ()))
    units = [(sb, kvh) for sb in range(tq // sub) for kvh in range(WIN_KV_HEADS)]

    def keys_or_values(u, p_ref, c_ref, n_ref, x_ref):
        sb, kvh = units[u]
        ks = slice(kvh * LANES, (kvh + 1) * LANES)
        if not with_window:
            return x_ref[:, ks]
        span = jnp.concatenate([p_ref[:, ks], c_ref[:, ks], n_ref[:, ks]], axis=0)
        return jnp.concatenate([span[sb * WINDOW:sb * WINDOW + n_win], x_ref[:, ks]], axis=0)

    def scores(u, buf):
        sb, kvh = units[u]
        r0 = sb * sub
        qs = []
        for pr in range(WIN_GROUP // 2):
            blk = kvh * (WIN_GROUP // 2) + pr
            qp = q_ref[r0:r0 + sub, blk * LANES:(blk + 1) * LANES]
            zero = jnp.zeros_like(qp)
            qs += [jnp.where(low, qp, zero), jnp.where(low, zero, qp)]
        kk = keys_or_values(u, kp_ref, kc_ref, kn_ref, kx_ref) if with_window else \
            keys_or_values(u, None, None, None, kx_ref)
        s_ref[buf] = lax.dot_general(jnp.concatenate(qs, axis=0), kk, nt,
                                     preferred_element_type=F32)

    def softmax_pv(u, buf):
        sb, kvh = units[u]
        r0 = sb * sub
        s = s_ref[buf]
        if with_window:
            qpos = i * tq + r0 + lax.broadcasted_iota(jnp.int32, (sub, n_win), 0)
            kpos = i * tq + r0 - WINDOW + lax.broadcasted_iota(jnp.int32, (sub, n_win), 1)
            valid = (jnp.abs(kpos - qpos) <= WINDOW) & (kpos >= 0) & (kpos < n_lat)
            bias = jnp.where(valid, 0.0, NEG_INF)
            bias = jnp.concatenate([bias] * WIN_GROUP, axis=0)
            s = jnp.concatenate([s[:, :n_win] + bias, s[:, n_win:]], axis=1)
        heads = [kvh * WIN_GROUP + g for g in range(WIN_GROUP)]
        sink = jnp.concatenate(
            [jnp.broadcast_to(sink_ref[hd:hd + 1, :], (sub, LANES)) for hd in heads], axis=0)
        m = jnp.maximum(jnp.max(s, axis=-1, keepdims=True), sink)
        p = jnp.exp(s - jnp.tile(m, (1, n_keys // LANES)))
        vv = keys_or_values(u, vp_ref, vc_ref, vn_ref, vx_ref) if with_window else \
            keys_or_values(u, None, None, None, vx_ref)
        pv = jnp.dot(p.astype(BF16), jnp.concatenate([vv, ones], axis=1),
                     preferred_element_type=F32)
        den = pv[:, LANES:LANES + 1] + jnp.exp(sink[:, 0:1] - m[:, 0:1])
        o = pv[:, :LANES] / den
        for pr in range(WIN_GROUP // 2):
            blk = kvh * (WIN_GROUP // 2) + pr
            lo = o[(2 * pr) * sub:(2 * pr + 1) * sub]
            hi = o[(2 * pr + 1) * sub:(2 * pr + 2) * sub]
            o_ref[r0:r0 + sub, blk * LANES:(blk + 1) * LANES] = jnp.where(low, lo, hi).astype(BF16)

    scores(0, 0)
    for u in range(len(units)):
        if u + 1 < len(units):
            scores(u + 1, (u + 1) % 2)
        softmax_pv(u, u % 2)


def _win_attn(q, k, v, kx, vx, sink_rows, with_window):
    nq = q.shape[0]
    tq = min(256, nq)
    n_ctx = kx.shape[0]
    kvw = WIN_KV_HEADS * LANES
    in_specs = [pl.BlockSpec((tq, D_MODEL), lambda i: (i, 0))]
    args = [q]
    if with_window:
        r = tq // WINDOW
        last = nq // WINDOW - 1
        prev_spec = pl.BlockSpec((WINDOW, kvw), lambda i: (jnp.maximum(i * r - 1, 0), 0))
        cur_spec = pl.BlockSpec((tq, kvw), lambda i: (i, 0))
        next_spec = pl.BlockSpec((WINDOW, kvw), lambda i: (jnp.minimum((i + 1) * r, last), 0))
        in_specs += [prev_spec, cur_spec, next_spec] * 2
        args += [k, k, k, v, v, v]
    in_specs += [_const_spec((n_ctx, kvw)), _const_spec((n_ctx, kvw)),
                 _const_spec((WIN_Q_HEADS, LANES))]
    args += [kx, vx, sink_rows]
    return pl.pallas_call(
        functools.partial(_win_attn_kernel, tq, nq, with_window),
        grid=(nq // tq,),
        in_specs=in_specs,
        out_specs=pl.BlockSpec((tq, D_MODEL), lambda i: (i, 0)),
        out_shape=jax.ShapeDtypeStruct((nq, D_MODEL), BF16),
        scratch_shapes=[pltpu.VMEM((2, WIN_GROUP * (WINDOW if with_window else tq),
                                    (3 * WINDOW if with_window else 0) + n_ctx), F32)],
        compiler_params=_cparams(1),
        name="win_attn",
    )(*args)


FFN_CHUNK = D_FF // 2
FFN_SUB_ROWS = 256


def _ffn_kernel(row, fused, *refs):
    if fused:
        x_ref, wo_ref, h_ref, mod_ref, n2_ref, wgu_ref, wd_ref, o_ref = refs
    else:
        h_ref, mod_ref, n2_ref, wgu_ref, wd_ref, o_ref = refs
    t = h_ref.shape[0]
    sub = min(FFN_SUB_ROWS, t)
    for sb in range(t // sub):
        rs = slice(sb * sub, (sb + 1) * sub)
        h = h_ref[rs, :]
        if fused:
            h = h + _mod(mod_ref, row, G1) * jnp.dot(x_ref[rs, :], wo_ref[...],
                                                     preferred_element_type=F32)
        a = _mod_norm(h, n2_ref[...], _mod(mod_ref, row, SC2), _mod(mod_ref, row, SH2)).astype(BF16)
        acc = None
        for c in range(D_FF // FFN_CHUNK):
            lo = c * FFN_CHUNK
            g = jnp.dot(a, wgu_ref[:, lo:lo + FFN_CHUNK], preferred_element_type=F32)
            u = jnp.dot(a, wgu_ref[:, D_FF + lo:D_FF + lo + FFN_CHUNK], preferred_element_type=F32)
            act = (_silu(g) * u).astype(BF16)
            part = jnp.dot(act, wd_ref[lo:lo + FFN_CHUNK, :], preferred_element_type=F32)
            acc = part if acc is None else acc + part
        o_ref[rs, :] = h + _mod(mod_ref, row, G2) * acc


def _ffn(h, mods, layer, row, n2, wgu, wd, attn=None):
    n = h.shape[0]
    t = _row_tile(n)
    row_spec = pl.BlockSpec((t, D_MODEL), lambda i: (i, 0))
    in_specs = [row_spec,
                pl.BlockSpec((None, 8, 6 * D_MODEL), lambda i: (layer, 0, 0)),
                _const_spec((1, D_MODEL)),
                _const_spec((D_MODEL, 2 * D_FF)),
                _const_spec((D_FF, D_MODEL))]
    args = [h, mods, n2.reshape(1, D_MODEL), wgu, wd]
    if attn is not None:
        in_specs = [row_spec, _const_spec((D_MODEL, D_MODEL))] + in_specs
        args = list(attn) + args
    return pl.pallas_call(
        functools.partial(_ffn_kernel, row, attn is not None),
        grid=(n // t,),
        in_specs=in_specs,
        out_specs=row_spec,
        out_shape=jax.ShapeDtypeStruct((n, D_MODEL), F32),
        compiler_params=_cparams(1),
        name="ffn",
    )(*args)


def _in_proj_kernel(row, mode, *refs):
    if mode == "short":
        h_ref, mod_ref, n1_ref, w_ref, b_out, cu_out = refs
    else:
        h_ref, mod_ref, n1_ref, w_ref, bias_ref, glu_out = refs
    t = h_ref.shape[0]
    sub = min(QKV_SUB_ROWS, t)
    for sb in range(t // sub):
        rs = slice(sb * sub, (sb + 1) * sub)
        a = _mod_norm(h_ref[rs, :], n1_ref[...], _mod(mod_ref, row, SC1), _mod(mod_ref, row, SH1))
        y = jnp.dot(a.astype(BF16), w_ref[...], preferred_element_type=F32)
        if mode == "short":
            b_out[rs, :] = y[:, :D_MODEL]
            cu_out[rs, :] = y[:, D_MODEL:2 * D_MODEL] * y[:, 2 * D_MODEL:]
        else:
            y = y + bias_ref[...]
            g = y[:, D_MODEL:]
            glu_out[rs, :] = y[:, :D_MODEL] * (1.0 / (1.0 + jnp.exp(-g)))


def _in_proj(h, mods, layer, row, n1, w, mode, bias=None):
    n = h.shape[0]
    t = _row_tile(n)
    wtot = w.shape[1]
    in_specs = [pl.BlockSpec((t, D_MODEL), lambda i: (i, 0)),
                pl.BlockSpec((None, 8, 6 * D_MODEL), lambda i: (layer, 0, 0)),
                _const_spec((1, D_MODEL)),
                _const_spec((D_MODEL, wtot))]
    args = [h, mods, n1.reshape(1, D_MODEL), w]
    row_spec = pl.BlockSpec((t, D_MODEL), lambda i: (i, 0))
    row_shape = jax.ShapeDtypeStruct((n, D_MODEL), F32)
    if mode == "short":
        out_specs, out_shape = [row_spec, row_spec], [row_shape, row_shape]
    else:
        in_specs.append(_const_spec((1, wtot)))
        args.append(bias.reshape(1, wtot))
        out_specs, out_shape = row_spec, row_shape
    return pl.pallas_call(
        functools.partial(_in_proj_kernel, row, mode),
        grid=(n // t,),
        in_specs=in_specs,
        out_specs=out_specs,
        out_shape=out_shape,
        compiler_params=_cparams(1),
        name="in_proj_" + mode,
    )(*args)


CONV_ROW_BLOCK = 64
SUBLANES = 8


def _conv_shifts(taps):
    return [divmod(CONV_HALO - taps // 2 + k, SUBLANES) for k in range(taps)]


def _conv_rems(taps):
    return sorted({r for _, r in _conv_shifts(taps) if r})


def _conv_span(taps, t):
    return t + SUBLANES * max(a for a, _ in _conv_shifts(taps))


def _dwconv_tile(xs_ref, sh_ref, u_ref, prev_ref, cur_ref, next_ref, w_ref, taps, t):
    i = pl.program_id(0)
    last = pl.num_programs(0) - 1
    ncb = D_MODEL // LANES
    for cb in range(ncb):
        cs = slice(cb * LANES, (cb + 1) * LANES)
        xs_ref[cb, 0:CONV_HALO, :] = jnp.where(i > 0, prev_ref[:, cs], 0.0)
        xs_ref[cb, CONV_HALO:CONV_HALO + t, :] = cur_ref[:, cs]
        xs_ref[cb, CONV_HALO + t:, :] = jnp.where(i < last, next_ref[:, cs], 0.0)
    shifts = _conv_shifts(taps)
    rems = _conv_rems(taps)
    span = _conv_span(taps, t)
    for j, r in enumerate(rems):
        sh_ref[j] = xs_ref[:, r:r + span, :]

    def block(idx, carry):
        rb, cb = idx // ncb, idx % ncb
        row0 = pl.multiple_of(rb * CONV_ROW_BLOCK, CONV_ROW_BLOCK)
        accs = [None, None]
        for k, (a, r) in enumerate(shifts):
            rows = pl.ds(row0 + a * SUBLANES, CONV_ROW_BLOCK)
            x = xs_ref[cb, rows, :] if r == 0 else sh_ref[rems.index(r), cb, rows, :]
            term = x * w_ref[cb, k:k + 1, :]
            accs[k % 2] = term if accs[k % 2] is None else accs[k % 2] + term
        u_ref[cb, pl.ds(row0, CONV_ROW_BLOCK), :] = accs[0] + accs[1]
        return carry

    lax.fori_loop(0, (t // CONV_ROW_BLOCK) * ncb, block, 0)
    return jnp.concatenate([u_ref[cb] for cb in range(ncb)], axis=1)


def _conv_out_kernel(row, mode, taps, t, *refs):
    if mode == "short":
        (prev_ref, cur_ref, next_ref, cw_ref, b_ref, h_ref, mod_ref, w_ref, o_ref,
         xs_ref, sh_ref, u_ref) = refs
    else:
        (prev_ref, cur_ref, next_ref, cw_ref, dwb_ref, lng_ref, lnb_ref, h_ref, mod_ref,
         w_ref, pb_ref, o_ref, xs_ref, sh_ref, u_ref) = refs
    u = _dwconv_tile(xs_ref, sh_ref, u_ref, prev_ref, cur_ref, next_ref, cw_ref, taps, t)
    if mode == "short":
        y = jnp.dot((b_ref[...] * u).astype(BF16), w_ref[...], preferred_element_type=F32)
    else:
        u = u + dwb_ref[...]
        mu = jnp.mean(u, axis=-1, keepdims=True)
        uc = u - mu
        var = jnp.mean(uc * uc, axis=-1, keepdims=True)
        z = _silu(uc * lax.rsqrt(var + NORM_EPS) * lng_ref[...] + lnb_ref[...])
        y = jnp.dot(z.astype(BF16), w_ref[...], preferred_element_type=F32) + pb_ref[...]
    o_ref[...] = h_ref[...] + _mod(mod_ref, row, G1) * y


def _conv_out(x, h, mods, layer, row, conv_w, w, mode, extra):
    n = h.shape[0]
    t = _row_tile(n)
    taps = conv_w.shape[0]
    r = t // CONV_HALO
    last = n // CONV_HALO - 1
    taps_pad = -(-taps // 8) * 8
    ncb = D_MODEL // LANES
    cw = jnp.zeros((taps_pad, D_MODEL), F32).at[:taps].set(conv_w)
    cw = cw.reshape(taps_pad, ncb, LANES).transpose(1, 0, 2)
    row_spec = pl.BlockSpec((t, D_MODEL), lambda i: (i, 0))
    vec_spec = _const_spec((1, D_MODEL))
    in_specs = [pl.BlockSpec((CONV_HALO, D_MODEL), lambda i: (jnp.maximum(i * r - 1, 0), 0)),
                row_spec,
                pl.BlockSpec((CONV_HALO, D_MODEL), lambda i: (jnp.minimum((i + 1) * r, last), 0)),
                _const_spec((ncb, taps_pad, LANES))]
    args = [x, x, x, cw]
    mod_spec = pl.BlockSpec((None, 8, 6 * D_MODEL), lambda i: (layer, 0, 0))
    if mode == "short":
        (b_gate,) = extra
        in_specs += [row_spec, row_spec, mod_spec, _const_spec((D_MODEL, D_MODEL))]
        args += [b_gate, h, mods, w]
    else:
        dw_b, ln_g, ln_b, pw_b = extra
        in_specs += [vec_spec, vec_spec, vec_spec, row_spec, mod_spec,
                     _const_spec((D_MODEL, D_MODEL)), vec_spec]
        args += [dw_b.reshape(1, D_MODEL), ln_g.reshape(1, D_MODEL), ln_b.reshape(1, D_MODEL),
                 h, mods, w, pw_b.reshape(1, D_MODEL)]
    return pl.pallas_call(
        functools.partial(_conv_out_kernel, row, mode, taps, t),
        grid=(n // t,),
        in_specs=in_specs,
        out_specs=row_spec,
        out_shape=jax.ShapeDtypeStruct((n, D_MODEL), F32),
        scratch_shapes=[pltpu.VMEM((ncb, t + 2 * CONV_HALO, LANES), F32),
                        pltpu.VMEM((len(_conv_rems(taps)), ncb, _conv_span(taps, t), LANES), F32),
                        pltpu.VMEM((ncb, t, LANES), F32)],
        compiler_params=_cparams(1),
        name="conv_out_" + mode,
    )(*args)


def _diff_qkv_weight(w_qkv):
    idx = np.arange(D_MODEL).reshape(2, DIFF_HEADS, HEAD_DIM).transpose(1, 0, 2).reshape(-1)
    cols = np.concatenate([idx, D_MODEL + idx, 2 * D_MODEL + np.arange(D_MODEL)])
    return w_qkv[:, cols].astype(BF16)


def _win_qkv_weight(w_qkv):
    qw = WIN_Q_HEADS * HEAD_DIM
    kvw = WIN_KV_HEADS * HEAD_DIM
    dup = np.repeat(np.arange(kvw).reshape(WIN_KV_HEADS, 1, HEAD_DIM), 2, axis=1).reshape(-1)
    cols = np.concatenate([np.arange(qw), qw + dup, qw + kvw + dup])
    return w_qkv[:, cols].astype(BF16)


def kernel(x, c, ctx, c_ctx, ada_w, ada_b, norm1, norm2, ffn_w_gate_up, ffn_w_down, diff_w_qkv, diff_w_o, diff_q_norm, diff_k_norm, diff_lam_q1, diff_lam_k1, diff_lam_q2, diff_lam_k2, diff_subln, sc_w_in, sc_conv_w, sc_w_out, win_w_qkv, win_w_o, win_q_norm, win_k_norm, win_sink, cf_w_pw1, cf_b_pw1, cf_dw_w, cf_dw_b, cf_ln_g, cf_ln_b, cf_w_pw2, cf_b_pw2):
    n = x.shape[1]
    h, hc = x.reshape(n, D_MODEL), ctx.reshape(ctx.shape[1], D_MODEL)
    mods = _ada_mod(c, c_ctx, ada_w, ada_b)
    tab = _rope_tables(n, _row_tile(n))
    for i in range(DEPTH):
        kind, j = i % N_MIXERS, i // N_MIXERS
        with_ctx = i < DEPTH - 1
        attn_l = attn_c = None
        if kind == 0:
            lam_init = 0.8 - 0.6 * math.exp(-0.3 * i)
            w = _diff_qkv_weight(diff_w_qkv[j])
            w_o = diff_w_o[j].astype(BF16)
            lamv = jnp.zeros((8, LANES), F32).at[0:4, 0:HEAD_DIM].set(
                jnp.stack([diff_lam_q1[j], diff_lam_k1[j], diff_lam_q2[j], diff_lam_k2[j]]))
            q_scale = HEAD_DIM ** -0.5 * LOG2E
            proj = functools.partial(_qkv_proj, mods=mods, layer=i, n1=norm1[i], w=w,
                                     q_gain=diff_q_norm[j], k_gain=diff_k_norm[j],
                                     kw=D_MODEL, vw=D_MODEL, q_scale=q_scale)
            n_ctx = hc.shape[0]
            q_l, k_all, v_all = proj(h, row=0, tab=tab, kv_rows=n + n_ctx)
            q_c, k_all, v_all = proj(hc, row=1, tab=None, kv_into=(k_all, v_all), kv_row_offset=n)
            o_l = _diff_attn(q_l, k_all, v_all, 0, n + n_ctx, lamv, diff_subln[j], lam_init)
            attn_l = (o_l, w_o)
            if with_ctx:
                attn_c = (_diff_attn(q_c, k_all, v_all, n, n_ctx, lamv, diff_subln[j], lam_init), w_o)
        elif kind == 1:
            w_in = sc_w_in[j].astype(BF16)
            w_out = sc_w_out[j].astype(BF16)
            b_l, cu_l = _in_proj(h, mods, i, 0, norm1[i], w_in, "short")
            h = _conv_out(cu_l, h, mods, i, 0, sc_conv_w[j], w_out, "short", (b_l,))
            if with_ctx:
                b_c, cu_c = _in_proj(hc, mods, i, 1, norm1[i], w_in, "short")
                hc = _conv_out(cu_c, hc, mods, i, 1, sc_conv_w[j], w_out, "short", (b_c,))
        elif kind == 2:
            w = _win_qkv_weight(win_w_qkv[j])
            w_o = win_w_o[j].astype(BF16)
            kvw = WIN_KV_HEADS * LANES
            sink_rows = jnp.broadcast_to(win_sink[j][:, None], (WIN_Q_HEADS, LANES))
            proj = functools.partial(_qkv_proj, mods=mods, layer=i, n1=norm1[i], w=w,
                                     q_gain=win_q_norm[j], k_gain=win_k_norm[j],
                                     kw=kvw, vw=kvw, q_scale=HEAD_DIM ** -0.5)
            q_l, k_l, v_l = proj(h, row=0, tab=tab)
            q_c, k_c, v_c = proj(hc, row=1, tab=None)
            o_l = _win_attn(q_l, k_l, v_l, k_c, v_c, sink_rows, True)
            attn_l = (o_l, w_o)
            if with_ctx:
                attn_c = (_win_attn(q_c, None, None, k_c, v_c, sink_rows, False), w_o)
        else:
            w1 = cf_w_pw1[j].astype(BF16)
            w2 = cf_w_pw2[j].astype(BF16)
            extra = (cf_dw_b[j], cf_ln_g[j], cf_ln_b[j], cf_b_pw2[j])
            glu_l = _in_proj(h, mods, i, 0, norm1[i], w1, "glu", cf_b_pw1[j])
            h = _conv_out(glu_l, h, mods, i, 0, cf_dw_w[j], w2, "conf", extra)
            if with_ctx:
                glu_c = _in_proj(hc, mods, i, 1, norm1[i], w1, "glu", cf_b_pw1[j])
                hc = _conv_out(glu_c, hc, mods, i, 1, cf_dw_w[j], w2, "conf", extra)
        wgu = ffn_w_gate_up[i].astype(BF16)
        wd = ffn_w_down[i].astype(BF16)
        h = _ffn(h, mods, i, 0, norm2[i], wgu, wd, attn_l)
        if with_ctx:
            hc = _ffn(hc, mods, i, 1, norm2[i], wgu, wd, attn_c)
    return h.reshape(1, n, D_MODEL)
```

```python
import functools
import math

import jax
import jax.numpy as jnp
import numpy as np
from jax import lax
from jax.experimental import pallas as pl
from jax.experimental.pallas import tpu as pltpu

F32 = jnp.float32
BF16 = jnp.bfloat16

D_MODEL = 1024
DEPTH = 4
N_MIXERS = 4
GRID_W = 64
NORM_EPS = 1e-6
ROPE_THETA = 10000.0
HEAD_DIM = 64
NEG_INF = -1e30
DIFF_HEADS = D_MODEL // (2 * HEAD_DIM)
SHORT_CONV_W = 3
WIN_Q_HEADS = D_MODEL // HEAD_DIM
WIN_KV_HEADS = 4
WIN_GROUP = WIN_Q_HEADS // WIN_KV_HEADS
WINDOW = 128
CONF_CONV_W = 31
D_FF = -(-(8 * D_MODEL) // (3 * 256)) * 256

LANES = 128
MXU_TILE = 256
CONV_HALO = 16
VMEM_LIMIT_BYTES = 56 * 1024 * 1024

LOG2E = math.log2(math.e)
SH1, SC1, G1, SH2, SC2, G2 = range(6)


def _cparams(n_axes):
    return pltpu.CompilerParams(dimension_semantics=("parallel",) * n_axes,
                                vmem_limit_bytes=VMEM_LIMIT_BYTES)


def _const_spec(shape):
    nd = len(shape)
    return pl.BlockSpec(shape, lambda *_: (0,) * nd, pipeline_mode=pl.Buffered(1))


def _mod(mod_ref, row, idx):
    return mod_ref[row:row + 1, idx * D_MODEL:(idx + 1) * D_MODEL]


def _mod_norm(h, gain, scale, shift):
    ms = jnp.mean(h * h, axis=-1, keepdims=True)
    return (h * lax.rsqrt(ms + NORM_EPS)) * (gain * (1.0 + scale)) + shift


def _silu(x):
    return x * (1.0 / (1.0 + jnp.exp(-x)))


def _row_tile(n):
    return 512 if n % 512 == 0 else n


def _ada_kernel(act_ref, w_ref, b_ref, o_ref):
    a = _silu(act_ref[...])
    o_ref[...] = jnp.dot(a.astype(BF16), w_ref[...].astype(BF16),
                         preferred_element_type=F32) + b_ref[...]


def _ada_mod(c, c_ctx, ada_w, ada_b):
    act = jnp.zeros((8, D_MODEL), F32).at[0].set(c[0]).at[1].set(c_ctx)
    cols = 1536
    nc = 6 * D_MODEL // cols
    return pl.pallas_call(
        _ada_kernel,
        grid=(DEPTH, nc),
        in_specs=[pl.BlockSpec((8, D_MODEL), lambda i, j: (0, 0)),
                  pl.BlockSpec((None, D_MODEL, cols), lambda i, j: (i, 0, j)),
                  pl.BlockSpec((None, 1, cols), lambda i, j: (i, 0, j))],
        out_specs=pl.BlockSpec((None, 8, cols), lambda i, j: (i, 0, j)),
        out_shape=jax.ShapeDtypeStruct((DEPTH, 8, 6 * D_MODEL), F32),
        compiler_params=_cparams(2),
        name="ada_mod",
    )(act, ada_w, ada_b.reshape(DEPTH, 1, 6 * D_MODEL))


def _group_mean_matrix(width):
    g = np.arange(width) // HEAD_DIM
    return jnp.asarray((g[:, None] == g[None, :]).astype(np.float32) / HEAD_DIM, BF16)


def _rope_tables(n, t):
    n_freq = HEAD_DIM // 4
    inv_freq = ROPE_THETA ** (-jnp.arange(n_freq, dtype=F32) / n_freq)
    reps = LANES // HEAD_DIM

    def tables(count, is_row):
        ang = jnp.arange(count, dtype=F32)[:, None] * inv_freq[None, :]
        cos, sin, zero = jnp.cos(ang), jnp.sin(ang), jnp.zeros_like(ang)
        halves = [[cos, cos], [-sin, zero], [zero, sin]]
        out = []
        for first, second in halves:
            pair = [first, second, zero, zero] if is_row else [zero, zero, first, second]
            out.append(jnp.tile(jnp.concatenate(pair, axis=-1), (1, reps)))
        return jnp.concatenate(out, axis=-1)

    return tables(n // GRID_W, True), jnp.tile(tables(GRID_W, False), (t // GRID_W, 1))


def _rope(x, tab):
    w = x.shape[-1]
    reps = w // LANES
    cos = jnp.tile(tab[:, 0:LANES], (1, reps))
    sin_up = jnp.tile(tab[:, LANES:2 * LANES], (1, reps))
    sin_dn = jnp.tile(tab[:, 2 * LANES:3 * LANES], (1, reps))
    quarter = HEAD_DIM // 4
    return (x * cos + pltpu.roll(x, w - quarter, 1) * sin_up
            + pltpu.roll(x, quarter, 1) * sin_dn)


QKV_SUB_ROWS = 256


def _qkv_kernel(row, rotary, kw, q_scale, n_alias, *refs):
    refs = list(refs)
    del refs[-3 - n_alias:-3]
    if rotary:
        (h_ref, mod_ref, n1_ref, w_ref, gm_ref, qg_ref, kg_ref, rt_ref, ct_ref,
         q_ref, k_ref, v_ref) = refs
    else:
        h_ref, mod_ref, n1_ref, w_ref, gm_ref, qg_ref, kg_ref, q_ref, k_ref, v_ref = refs
    gm = gm_ref[...]

    def head_norm(x, gain, width):
        x2 = (x * x).astype(BF16)
        ms = jnp.concatenate(
            [jnp.dot(x2[:, c:c + MXU_TILE], gm, preferred_element_type=F32)
             for c in range(0, width, MXU_TILE)], axis=1)
        return x * lax.rsqrt(ms + NORM_EPS) * gain

    t = h_ref.shape[0]
    sub = min(QKV_SUB_ROWS, t)
    for sb in range(t // sub):
        rs = slice(sb * sub, (sb + 1) * sub)
        a = _mod_norm(h_ref[rs, :], n1_ref[...], _mod(mod_ref, row, SC1), _mod(mod_ref, row, SH1))
        y = jnp.dot(a.astype(BF16), w_ref[...], preferred_element_type=F32)
        q = head_norm(y[:, :D_MODEL], qg_ref[...], D_MODEL)
        k = head_norm(y[:, D_MODEL:D_MODEL + kw], kg_ref[...], kw)
        if rotary:
            g0 = sb * sub // GRID_W
            row_part = jnp.concatenate(
                [jnp.broadcast_to(rt_ref[g0 + r:g0 + r + 1, :], (GRID_W, 3 * LANES))
                 for r in range(sub // GRID_W)], axis=0)
            tab = row_part + ct_ref[rs, :]
            q, k = _rope(q, tab), _rope(k, tab)
        q_ref[rs, :] = (q * q_scale).astype(BF16)
        k_ref[rs, :] = k.astype(BF16)
        v_ref[rs, :] = y[:, D_MODEL + kw:].astype(BF16)


def _qkv_proj(h, mods, layer, row, n1, w, q_gain, k_gain, kw, vw, q_scale, tab,
              kv_into=None, kv_row_offset=0):
    n = h.shape[0]
    t = _row_tile(n)
    rotary = tab is not None
    wtot = D_MODEL + kw + vw
    in_specs = [pl.BlockSpec((t, D_MODEL), lambda i: (i, 0)),
                pl.BlockSpec((None, 8, 6 * D_MODEL), lambda i: (layer, 0, 0)),
                _const_spec((1, D_MODEL)),
                _const_spec((D_MODEL, wtot)),
                _const_spec((MXU_TILE, MXU_TILE)),
                _const_spec((1, D_MODEL)),
                _const_spec((1, kw))]
    args = [h, mods, n1.reshape(1, D_MODEL), w, _group_mean_matrix(MXU_TILE),
            jnp.tile(q_gain, D_MODEL // HEAD_DIM).reshape(1, D_MODEL),
            jnp.tile(k_gain, kw // HEAD_DIM).reshape(1, kw)]
    if rotary:
        row_tab, col_tab = tab
        in_specs += [pl.BlockSpec((t // GRID_W, 3 * LANES), lambda i: (i, 0)),
                     _const_spec((t, 3 * LANES))]
        args += [row_tab, col_tab]
    aliases = {}
    off = kv_row_offset // t
    kv_total = n
    if kv_into is not None:
        kv_total = kv_into[0].shape[0]
        aliases = {len(args): 1, len(args) + 1: 2}
        in_specs += [pl.BlockSpec(memory_space=pl.ANY)] * 2
        args += list(kv_into)
    return pl.pallas_call(
        functools.partial(_qkv_kernel, row, rotary, kw, q_scale, len(aliases)),
        grid=(n // t,),
        in_specs=in_specs,
        out_specs=[pl.BlockSpec((t, D_MODEL), lambda i: (i, 0)),
                   pl.BlockSpec((t, kw), lambda i: (i + off, 0)),
                   pl.BlockSpec((t, vw), lambda i: (i + off, 0))],
        out_shape=[jax.ShapeDtypeStruct((n, D_MODEL), BF16),
                   jax.ShapeDtypeStruct((kv_total, kw), BF16),
                   jax.ShapeDtypeStruct((kv_total, vw), BF16)],
        input_output_aliases=aliases,
        compiler_params=_cparams(1),
        name="qkv_proj",
    )(*args)


DIFF_TQ = 256
DIFF_TK = 1280


def _diff_attn_kernel(tq, tk, n_tiles, n_chunks, lam_init, q_ref, lamv_ref, sub_ref, k_ref, v_ref,
                      o_ref, qq_ref, s_ref, m_ref, acc_ref):
    lane = lax.broadcasted_iota(jnp.int32, (tq, LANES), 1)
    lv = lamv_ref[...]
    lam = (jnp.exp(jnp.sum(lv[0:1] * lv[1:2], axis=-1, keepdims=True))
           - jnp.exp(jnp.sum(lv[2:3] * lv[3:4], axis=-1, keepdims=True)) + lam_init)
    out_gain = sub_ref[...] * (1.0 - lam_init)

    def rows_of(t):
        return pl.ds(t * tq, tq) if isinstance(t, int) else pl.ds(pl.multiple_of(t * tq, tq), tq)

    def stack_queries(t, slot):
        q = q_ref[rows_of(t), :]
        zero = jnp.zeros_like(q)
        qq_ref[slot, 0:tq, :] = jnp.where(lane < HEAD_DIM, q, zero)
        qq_ref[slot, tq:, :] = jnp.where(lane >= HEAD_DIM, q, zero)

    def scores(slot, c, buf):
        s_ref[buf] = lax.dot_general(qq_ref[slot], k_ref[c * tk:(c + 1) * tk, :],
                                     (((1,), (1,)), ((), ())), preferred_element_type=F32)

    def softmax_pv(c, buf):
        s = s_ref[buf]
        m_prev = m_ref[...]
        m_new = jnp.maximum(m_prev, jnp.max(s, axis=-1, keepdims=True))
        alpha = jnp.exp2(m_prev - m_new)
        p = jnp.exp2(s - jnp.tile(m_new, (1, tk // LANES)))
        v_aug = jnp.concatenate([v_ref[c * tk:(c + 1) * tk, :], jnp.ones((tk, LANES), BF16)], axis=1)
        pv = jnp.dot(p.astype(BF16), v_aug, preferred_element_type=F32)
        acc_ref[...] = acc_ref[...] * jnp.tile(alpha, (1, 2)) + pv
        m_ref[...] = m_new

    def tile(t, t_next, slot, first_buf):
        m_ref[...] = jnp.full(m_ref.shape, NEG_INF, F32)
        acc_ref[...] = jnp.zeros(acc_ref.shape, F32)
        for c in range(n_chunks):
            buf = (first_buf + c) % 2
            if c + 1 < n_chunks:
                scores(slot, c + 1, 1 - buf)
            elif t_next is not None:
                stack_queries(t_next, 1 - slot)
                scores(1 - slot, 0, 1 - buf)
            softmax_pv(c, buf)
        acc = acc_ref[...]
        o = acc[:, :LANES] / acc[:, LANES:LANES + 1]
        d = o[:tq] - lam * o[tq:]
        ms = jnp.mean(d * d, axis=-1, keepdims=True)
        o_ref[rows_of(t), :] = (d * lax.rsqrt(ms + NORM_EPS) * out_gain).astype(BF16)

    stack_queries(0, 0)
    scores(0, 0, 0)
    flip = n_chunks % 2
    if n_tiles % 2 == 0 and n_tiles > 2:
        def pair(tp, carry):
            t0 = 2 * tp
            tile(t0, t0 + 1, 0, 0)
            tile(t0 + 1, jnp.minimum(t0 + 2, n_tiles - 1), 1, flip)
            return carry
        lax.fori_loop(0, n_tiles // 2, pair, 0)
    else:
        for t in range(n_tiles):
            tile(t, t + 1 if t + 1 < n_tiles else None, t % 2, (t * flip) % 2)


def _diff_attn(q, k, v, key_start, n_keys, lamv, subln, lam_init):
    nq = q.shape[0]
    tq = min(DIFF_TQ, nq)
    tk = min(DIFF_TK, n_keys)
    n_chunks = n_keys // tk
    assert nq % tq == 0 and n_keys % tk == 0 and key_start % n_keys == 0
    kb = key_start // n_keys
    return pl.pallas_call(
        functools.partial(_diff_attn_kernel, tq, tk, nq // tq, n_chunks, lam_init),
        grid=(DIFF_HEADS,),
        in_specs=[pl.BlockSpec((nq, LANES), lambda h: (0, h)),
                  _const_spec((8, LANES)),
                  _const_spec((1, LANES)),
                  pl.BlockSpec((n_keys, LANES), lambda h: (kb, h)),
                  pl.BlockSpec((n_keys, LANES), lambda h: (kb, h))],
        out_specs=pl.BlockSpec((nq, LANES), lambda h: (0, h)),
        out_shape=jax.ShapeDtypeStruct((nq, D_MODEL), BF16),
        scratch_shapes=[pltpu.VMEM((2, 2 * tq, LANES), BF16),
                        pltpu.VMEM((2, 2 * tq, tk), F32),
                        pltpu.VMEM((2 * tq, LANES), F32),
                        pltpu.VMEM((2 * tq, 2 * LANES), F32)],
        compiler_params=_cparams(1),
        name="diff_attn",
    )(q, lamv, subln.reshape(1, LANES), k, v)


def _win_attn_kernel(tq, n_lat, with_window, *refs):
    if with_window:
        (q_ref, kp_ref, kc_ref, kn_ref, vp_ref, vc_ref, vn_ref,
         kx_ref, vx_ref, sink_ref, o_ref, s_ref) = refs
    else:
        q_ref, kx_ref, vx_ref, sink_ref, o_ref, s_ref = refs
    i = pl.program_id(0)
    n_ctx = kx_ref.shape[0]
    sub = WINDOW if with_window else tq
    n_win = 3 * WINDOW if with_window else 0
    n_keys = n_win + n_ctx
    low = lax.broadcasted_iota(jnp.int32, (sub, LANES), 1) < HEAD_DIM
    ones = jnp.ones((n_keys, LANES), BF16)
    nt = (((1,), (1,)), ((), ()))
    units = [(sb, kvh) for sb in range(tq // sub) for kvh in range(WIN_KV_HEADS)]

    def keys_or_values(u, p_ref, c_ref, n_ref, x_ref):
        sb, kvh = units[u]
        ks = slice(kvh * LANES, (kvh + 1) * LANES)
        if not with_window:
            return x_ref[:, ks]
        span = jnp.concatenate([p_ref[:, ks], c_ref[:, ks], n_ref[:, ks]], axis=0)
        return jnp.concatenate([span[sb * WINDOW:sb * WINDOW + n_win], x_ref[:, ks]], axis=0)

    def scores(u, buf):
        sb, kvh = units[u]
        r0 = sb * sub
        qs = []
        for pr in range(WIN_GROUP // 2):
            blk = kvh * (WIN_GROUP // 2) + pr
            qp = q_ref[r0:r0 + sub, blk * LANES:(blk + 1) * LANES]
            zero = jnp.zeros_like(qp)
            qs += [jnp.where(low, qp, zero), jnp.where(low, zero, qp)]
        kk = keys_or_values(u, kp_ref, kc_ref, kn_ref, kx_ref) if with_window else \
            keys_or_values(u, None, None, None, kx_ref)
        s_ref[buf] = lax.dot_general(jnp.concatenate(qs, axis=0), kk, nt,
                                     preferred_element_type=F32)

    def softmax_pv(u, buf):
        sb, kvh = units[u]
        r0 = sb * sub
        s = s_ref[buf]
        if with_window:
            qpos = i * tq + r0 + lax.broadcasted_iota(jnp.int32, (sub, n_win), 0)
            kpos = i * tq + r0 - WINDOW + lax.broadcasted_iota(jnp.int32, (sub, n_win), 1)
            valid = (jnp.abs(kpos - qpos) <= WINDOW) & (kpos >= 0) & (kpos < n_lat)
            bias = jnp.where(valid, 0.0, NEG_INF)
            bias = jnp.concatenate([bias] * WIN_GROUP, axis=0)
            s = jnp.concatenate([s[:, :n_win] + bias, s[:, n_win:]], axis=1)
        heads = [kvh * WIN_GROUP + g for g in range(WIN_GROUP)]
        sink = jnp.concatenate(
            [jnp.broadcast_to(sink_ref[hd:hd + 1, :], (sub, LANES)) for hd in heads], axis=0)
        m = jnp.maximum(jnp.max(s, axis=-1, keepdims=True), sink)
        p = jnp.exp(s - jnp.tile(m, (1, n_keys // LANES)))
        vv = keys_or_values(u, vp_ref, vc_ref, vn_ref, vx_ref) if with_window else \
            keys_or_values(u, None, None, None, vx_ref)
        pv = jnp.dot(p.astype(BF16), jnp.concatenate([vv, ones], axis=1),
                     preferred_element_type=F32)
        den = pv[:, LANES:LANES + 1] + jnp.exp(sink[:, 0:1] - m[:, 0:1])
        o = pv[:, :LANES] / den
        for pr in range(WIN_GROUP // 2):
            blk = kvh * (WIN_GROUP // 2) + pr
            lo = o[(2 * pr) * sub:(2 * pr + 1) * sub]
            hi = o[(2 * pr + 1) * sub:(2 * pr + 2) * sub]
            o_ref[r0:r0 + sub, blk * LANES:(blk + 1) * LANES] = jnp.where(low, lo, hi).astype(BF16)

    scores(0, 0)
    for u in range(len(units)):
        if u + 1 < len(units):
            scores(u + 1, (u + 1) % 2)
        softmax_pv(u, u % 2)


def _win_attn(q, k, v, kx, vx, sink_rows, with_window):
    nq = q.shape[0]
    tq = min(256, nq)
    n_ctx = kx.shape[0]
    kvw = WIN_KV_HEADS * LANES
    in_specs = [pl.BlockSpec((tq, D_MODEL), lambda i: (i, 0))]
    args = [q]
    if with_window:
        r = tq // WINDOW
        last = nq // WINDOW - 1
        prev_spec = pl.BlockSpec((WINDOW, kvw), lambda i: (jnp.maximum(i * r - 1, 0), 0))
        cur_spec = pl.BlockSpec((tq, kvw), lambda i: (i, 0))
        next_spec = pl.BlockSpec((WINDOW, kvw), lambda i: (jnp.minimum((i + 1) * r, last), 0))
        in_specs += [prev_spec, cur_spec, next_spec] * 2
        args += [k, k, k, v, v, v]
    in_specs += [_const_spec((n_ctx, kvw)), _const_spec((n_ctx, kvw)),
                 _const_spec((WIN_Q_HEADS, LANES))]
    args += [kx, vx, sink_rows]
    return pl.pallas_call(
        functools.partial(_win_attn_kernel, tq, nq, with_window),
        grid=(nq // tq,),
        in_specs=in_specs,
        out_specs=pl.BlockSpec((tq, D_MODEL), lambda i: (i, 0)),
        out_shape=jax.ShapeDtypeStruct((nq, D_MODEL), BF16),
        scratch_shapes=[pltpu.VMEM((2, WIN_GROUP * (WINDOW if with_window else tq),
                                    (3 * WINDOW if with_window else 0) + n_ctx), F32)],
        compiler_params=_cparams(1),
        name="win_attn",
    )(*args)


_FFN_SPLIT = -(-(D_FF // MXU_TILE) // 2) * MXU_TILE
FFN_CHUNKS = ((0, _FFN_SPLIT), (_FFN_SPLIT, D_FF))
FFN_SUB_ROWS = 256


def _ffn_kernel(row, fused, *refs):
    if fused:
        x_ref, wo_ref, h_ref, mod_ref, n2_ref, wgu_ref, wd_ref, o_ref = refs
    else:
        h_ref, mod_ref, n2_ref, wgu_ref, wd_ref, o_ref = refs
    t = h_ref.shape[0]
    sub = min(FFN_SUB_ROWS, t)
    for sb in range(t // sub):
        rs = slice(sb * sub, (sb + 1) * sub)
        h = h_ref[rs, :]
        if fused:
            h = h + _mod(mod_ref, row, G1) * jnp.dot(x_ref[rs, :], wo_ref[...],
                                                     preferred_element_type=F32)
        a = _mod_norm(h, n2_ref[...], _mod(mod_ref, row, SC2), _mod(mod_ref, row, SH2)).astype(BF16)
        acc = None
        for lo, hi in FFN_CHUNKS:
            g = jnp.dot(a, wgu_ref[:, lo:hi], preferred_element_type=F32)
            u = jnp.dot(a, wgu_ref[:, D_FF + lo:D_FF + hi], preferred_element_type=F32)
            act = (_silu(g) * u).astype(BF16)
            part = jnp.dot(act, wd_ref[lo:hi, :], preferred_element_type=F32)
            acc = part if acc is None else acc + part
        o_ref[rs, :] = h + _mod(mod_ref, row, G2) * acc


def _ffn(h, mods, layer, row, n2, wgu, wd, attn=None):
    n = h.shape[0]
    t = _row_tile(n)
    row_spec = pl.BlockSpec((t, D_MODEL), lambda i: (i, 0))
    in_specs = [row_spec,
                pl.BlockSpec((None, 8, 6 * D_MODEL), lambda i: (layer, 0, 0)),
                _const_spec((1, D_MODEL)),
                _const_spec((D_MODEL, 2 * D_FF)),
                _const_spec((D_FF, D_MODEL))]
    args = [h, mods, n2.reshape(1, D_MODEL), wgu, wd]
    if attn is not None:
        in_specs = [row_spec, _const_spec((D_MODEL, D_MODEL))] + in_specs
        args = list(attn) + args
    return pl.pallas_call(
        functools.partial(_ffn_kernel, row, attn is not None),
        grid=(n // t,),
        in_specs=in_specs,
        out_specs=row_spec,
        out_shape=jax.ShapeDtypeStruct((n, D_MODEL), F32),
        compiler_params=_cparams(1),
        name="ffn",
    )(*args)


def _in_proj_kernel(row, mode, *refs):
    if mode == "short":
        h_ref, mod_ref, n1_ref, w_ref, b_out, cu_out = refs
    else:
        h_ref, mod_ref, n1_ref, w_ref, bias_ref, glu_out = refs
    t = h_ref.shape[0]
    sub = min(QKV_SUB_ROWS, t)
    for sb in range(t // sub):
        rs = slice(sb * sub, (sb + 1) * sub)
        a = _mod_norm(h_ref[rs, :], n1_ref[...], _mod(mod_ref, row, SC1), _mod(mod_ref, row, SH1))
        y = jnp.dot(a.astype(BF16), w_ref[...], preferred_element_type=F32)
        if mode == "short":
            b_out[rs, :] = y[:, :D_MODEL]
            cu_out[rs, :] = y[:, D_MODEL:2 * D_MODEL] * y[:, 2 * D_MODEL:]
        else:
            y = y + bias_ref[...]
            g = y[:, D_MODEL:]
            glu_out[rs, :] = y[:, :D_MODEL] * (1.0 / (1.0 + jnp.exp(-g)))


def _in_proj(h, mods, layer, row, n1, w, mode, bias=None):
    n = h.shape[0]
    t = _row_tile(n)
    wtot = w.shape[1]
    in_specs = [pl.BlockSpec((t, D_MODEL), lambda i: (i, 0)),
                pl.BlockSpec((None, 8, 6 * D_MODEL), lambda i: (layer, 0, 0)),
                _const_spec((1, D_MODEL)),
                _const_spec((D_MODEL, wtot))]
    args = [h, mods, n1.reshape(1, D_MODEL), w]
    row_spec = pl.BlockSpec((t, D_MODEL), lambda i: (i, 0))
    row_shape = jax.ShapeDtypeStruct((n, D_MODEL), F32)
    if mode == "short":
        out_specs, out_shape = [row_spec, row_spec], [row_shape, row_shape]
    else:
        in_specs.append(_const_spec((1, wtot)))
        args.append(bias.reshape(1, wtot))
        out_specs, out_shape = row_spec, row_shape
    return pl.pallas_call(
        functools.partial(_in_proj_kernel, row, mode),
        grid=(n // t,),
        in_specs=in_specs,
        out_specs=out_specs,
        out_shape=out_shape,
        compiler_params=_cparams(1),
        name="in_proj_" + mode,
    )(*args)


CONV_ROW_BLOCK = 64
SUBLANES = 8


def _conv_shifts(taps):
    return [divmod(CONV_HALO - taps // 2 + k, SUBLANES) for k in range(taps)]


def _conv_rems(taps):
    return sorted({r for _, r in _conv_shifts(taps) if r})


def _conv_span(taps, t):
    return t + SUBLANES * max(a for a, _ in _conv_shifts(taps))


def _dwconv_tile(xs_ref, sh_ref, u_ref, prev_ref, cur_ref, next_ref, w_ref, taps, t):
    i = pl.program_id(0)
    last = pl.num_programs(0) - 1
    ncb = D_MODEL // LANES
    for cb in range(ncb):
        cs = slice(cb * LANES, (cb + 1) * LANES)
        xs_ref[cb, 0:CONV_HALO, :] = jnp.where(i > 0, prev_ref[:, cs], 0.0)
        xs_ref[cb, CONV_HALO:CONV_HALO + t, :] = cur_ref[:, cs]
        xs_ref[cb, CONV_HALO + t:, :] = jnp.where(i < last, next_ref[:, cs], 0.0)
    shifts = _conv_shifts(taps)
    rems = _conv_rems(taps)
    span = _conv_span(taps, t)
    for j, r in enumerate(rems):
        sh_ref[j] = xs_ref[:, r:r + span, :]

    def block(idx, carry):
        rb, cb = idx // ncb, idx % ncb
        row0 = pl.multiple_of(rb * CONV_ROW_BLOCK, CONV_ROW_BLOCK)
        accs = [None, None]
        for k, (a, r) in enumerate(shifts):
            rows = pl.ds(row0 + a * SUBLANES, CONV_ROW_BLOCK)
            x = xs_ref[cb, rows, :] if r == 0 else sh_ref[rems.index(r), cb, rows, :]
            term = x * w_ref[cb, k:k + 1, :]
            accs[k % 2] = term if accs[k % 2] is None else accs[k % 2] + term
        u_ref[cb, pl.ds(row0, CONV_ROW_BLOCK), :] = accs[0] + accs[1]
        return carry

    lax.fori_loop(0, (t // CONV_ROW_BLOCK) * ncb, block, 0)
    return jnp.concatenate([u_ref[cb] for cb in range(ncb)], axis=1)


def _conv_out_kernel(row, mode, taps, t, *refs):
    if mode == "short":
        (prev_ref, cur_ref, next_ref, cw_ref, b_ref, h_ref, mod_ref, w_ref, o_ref,
         xs_ref, sh_ref, u_ref) = refs
    else:
        (prev_ref, cur_ref, next_ref, cw_ref, dwb_ref, lng_ref, lnb_ref, h_ref, mod_ref,
         w_ref, pb_ref, o_ref, xs_ref, sh_ref, u_ref) = refs
    u = _dwconv_tile(xs_ref, sh_ref, u_ref, prev_ref, cur_ref, next_ref, cw_ref, taps, t)
    if mode == "short":
        y = jnp.dot((b_ref[...] * u).astype(BF16), w_ref[...], preferred_element_type=F32)
    else:
        u = u + dwb_ref[...]
        mu = jnp.mean(u, axis=-1, keepdims=True)
        uc = u - mu
        var = jnp.mean(uc * uc, axis=-1, keepdims=True)
        z = _silu(uc * lax.rsqrt(var + NORM_EPS) * lng_ref[...] + lnb_ref[...])
        y = jnp.dot(z.astype(BF16), w_ref[...], preferred_element_type=F32) + pb_ref[...]
    o_ref[...] = h_ref[...] + _mod(mod_ref, row, G1) * y


def _conv_out(x, h, mods, layer, row, conv_w, w, mode, extra):
    n = h.shape[0]
    t = _row_tile(n)
    taps = conv_w.shape[0]
    r = t // CONV_HALO
    last = n // CONV_HALO - 1
    taps_pad = -(-taps // 8) * 8
    ncb = D_MODEL // LANES
    cw = jnp.zeros((taps_pad, D_MODEL), F32).at[:taps].set(conv_w)
    cw = cw.reshape(taps_pad, ncb, LANES).transpose(1, 0, 2)
    row_spec = pl.BlockSpec((t, D_MODEL), lambda i: (i, 0))
    vec_spec = _const_spec((1, D_MODEL))
    in_specs = [pl.BlockSpec((CONV_HALO, D_MODEL), lambda i: (jnp.maximum(i * r - 1, 0), 0)),
                row_spec,
                pl.BlockSpec((CONV_HALO, D_MODEL), lambda i: (jnp.minimum((i + 1) * r, last), 0)),
                _const_spec((ncb, taps_pad, LANES))]
    args = [x, x, x, cw]
    mod_spec = pl.BlockSpec((None, 8, 6 * D_MODEL), lambda i: (layer, 0, 0))
    if mode == "short":
        (b_gate,) = extra
        in_specs += [row_spec, row_spec, mod_spec, _const_spec((D_MODEL, D_MODEL))]
        args += [b_gate, h, mods, w]
    else:
        dw_b, ln_g, ln_b, pw_b = extra
        in_specs += [vec_spec, vec_spec, vec_spec, row_spec, mod_spec,
                     _const_spec((D_MODEL, D_MODEL)), vec_spec]
        args += [dw_b.reshape(1, D_MODEL), ln_g.reshape(1, D_MODEL), ln_b.reshape(1, D_MODEL),
                 h, mods, w, pw_b.reshape(1, D_MODEL)]
    return pl.pallas_call(
        functools.partial(_conv_out_kernel, row, mode, taps, t),
        grid=(n // t,),
        in_specs=in_specs,
        out_specs=row_spec,
        out_shape=jax.ShapeDtypeStruct((n, D_MODEL), F32),
        scratch_shapes=[pltpu.VMEM((ncb, t + 2 * CONV_HALO, LANES), F32),
                        pltpu.VMEM((len(_conv_rems(taps)), ncb, _conv_span(taps, t), LANES), F32),
                        pltpu.VMEM((ncb, t, LANES), F32)],
        compiler_params=_cparams(1),
        name="conv_out_" + mode,
    )(*args)


def _diff_qkv_weight(w_qkv):
    idx = np.arange(D_MODEL).reshape(2, DIFF_HEADS, HEAD_DIM).transpose(1, 0, 2).reshape(-1)
    cols = np.concatenate([idx, D_MODEL + idx, 2 * D_MODEL + np.arange(D_MODEL)])
    return w_qkv[:, cols].astype(BF16)


def _win_qkv_weight(w_qkv):
    qw = WIN_Q_HEADS * HEAD_DIM
    kvw = WIN_KV_HEADS * HEAD_DIM
    dup = np.repeat(np.arange(kvw).reshape(WIN_KV_HEADS, 1, HEAD_DIM), 2, axis=1).reshape(-1)
    cols = np.concatenate([np.arange(qw), qw + dup, qw + kvw + dup])
    return w_qkv[:, cols].astype(BF16)


def kernel(x, c, ctx, c_ctx, ada_w, ada_b, norm1, norm2, ffn_w_gate_up, ffn_w_down, diff_w_qkv, diff_w_o, diff_q_norm, diff_k_norm, diff_lam_q1, diff_lam_k1, diff_lam_q2, diff_lam_k2, diff_subln, sc_w_in, sc_conv_w, sc_w_out, win_w_qkv, win_w_o, win_q_norm, win_k_norm, win_sink, cf_w_pw1, cf_b_pw1, cf_dw_w, cf_dw_b, cf_ln_g, cf_ln_b, cf_w_pw2, cf_b_pw2):
    n = x.shape[1]
    h, hc = x.reshape(n, D_MODEL), ctx.reshape(ctx.shape[1], D_MODEL)
    mods = _ada_mod(c, c_ctx, ada_w, ada_b)
    tab = _rope_tables(n, _row_tile(n))
    for i in range(DEPTH):
        kind, j = i % N_MIXERS, i // N_MIXERS
        with_ctx = i < DEPTH - 1
        attn_l = attn_c = None
        if kind == 0:
            lam_init = 0.8 - 0.6 * math.exp(-0.3 * i)
            w = _diff_qkv_weight(diff_w_qkv[j])
            w_o = diff_w_o[j].astype(BF16)
            lamv = jnp.zeros((8, LANES), F32).at[0:4, 0:HEAD_DIM].set(
                jnp.stack([diff_lam_q1[j], diff_lam_k1[j], diff_lam_q2[j], diff_lam_k2[j]]))
            q_scale = HEAD_DIM ** -0.5 * LOG2E
            proj = functools.partial(_qkv_proj, mods=mods, layer=i, n1=norm1[i], w=w,
                                     q_gain=diff_q_norm[j], k_gain=diff_k_norm[j],
                                     kw=D_MODEL, vw=D_MODEL, q_scale=q_scale)
            n_ctx = hc.shape[0]
            kv0 = (jnp.zeros((n + n_ctx, D_MODEL), BF16), jnp.zeros((n + n_ctx, D_MODEL), BF16))
            q_l, k_all, v_all = proj(h, row=0, tab=tab, kv_into=kv0)
            q_c, k_all, v_all = proj(hc, row=1, tab=None, kv_into=(k_all, v_all), kv_row_offset=n)
            o_l = _diff_attn(q_l, k_all, v_all, 0, n + n_ctx, lamv, diff_subln[j], lam_init)
            attn_l = (o_l, w_o)
            if with_ctx:
                attn_c = (_diff_attn(q_c, k_all, v_all, n, n_ctx, lamv, diff_subln[j], lam_init), w_o)
        elif kind == 1:
            w_in = sc_w_in[j].astype(BF16)
            w_out = sc_w_out[j].astype(BF16)
            b_l, cu_l = _in_proj(h, mods, i, 0, norm1[i], w_in, "short")
            h = _conv_out(cu_l, h, mods, i, 0, sc_conv_w[j], w_out, "short", (b_l,))
            if with_ctx:
                b_c, cu_c = _in_proj(hc, mods, i, 1, norm1[i], w_in, "short")
                hc = _conv_out(cu_c, hc, mods, i, 1, sc_conv_w[j], w_out, "short", (b_c,))
        elif kind == 2:
            w = _win_qkv_weight(win_w_qkv[j])
            w_o = win_w_o[j].astype(BF16)
            kvw = WIN_KV_HEADS * LANES
            sink_rows = jnp.broadcast_to(win_sink[j][:, None], (WIN_Q_HEADS, LANES))
            proj = functools.partial(_qkv_proj, mods=mods, layer=i, n1=norm1[i], w=w,
                                     q_gain=win_q_norm[j], k_gain=win_k_norm[j],
                                     kw=kvw, vw=kvw, q_scale=HEAD_DIM ** -0.5)
            q_l, k_l, v_l = proj(h, row=0, tab=tab)
            q_c, k_c, v_c = proj(hc, row=1, tab=None)
            o_l = _win_attn(q_l, k_l, v_l, k_c, v_c, sink_rows, True)
            attn_l = (o_l, w_o)
            if with_ctx:
                attn_c = (_win_attn(q_c, None, None, k_c, v_c, sink_rows, False), w_o)
        else:
            w1 = cf_w_pw1[j].astype(BF16)
            w2 = cf_w_pw2[j].astype(BF16)
            extra = (cf_dw_b[j], cf_ln_g[j], cf_ln_b[j], cf_b_pw2[j])
            glu_l = _in_proj(h, mods, i, 0, norm1[i], w1, "glu", cf_b_pw1[j])
            h = _conv_out(glu_l, h, mods, i, 0, cf_dw_w[j], w2, "conf", extra)
            if with_ctx:
                glu_c = _in_proj(hc, mods, i, 1, norm1[i], w1, "glu", cf_b_pw1[j])
                hc = _conv_out(glu_c, hc, mods, i, 1, cf_dw_w[j], w2, "conf", extra)
        wgu = ffn_w_gate_up[i].astype(BF16)
        wd = ffn_w_down[i].astype(BF16)
        h = _ffn(h, mods, i, 0, norm2[i], wgu, wd, attn_l)
        if with_ctx:
            hc = _ffn(hc, mods, i, 1, norm2[i], wgu, wd, attn_c)
    return h.reshape(1, n, D_MODEL)
```

```python
import functools
import math

import jax
import jax.numpy as jnp
import numpy as np
from jax import lax
from jax.experimental import pallas as pl
from jax.experimental.pallas import tpu as pltpu

F32 = jnp.float32
BF16 = jnp.bfloat16

D_MODEL = 1024
DEPTH = 4
N_MIXERS = 4
GRID_W = 64
NORM_EPS = 1e-6
ROPE_THETA = 10000.0
HEAD_DIM = 64
NEG_INF = -1e30
DIFF_HEADS = D_MODEL // (2 * HEAD_DIM)
SHORT_CONV_W = 3
WIN_Q_HEADS = D_MODEL // HEAD_DIM
WIN_KV_HEADS = 4
WIN_GROUP = WIN_Q_HEADS // WIN_KV_HEADS
WINDOW = 128
CONF_CONV_W = 31
D_FF = -(-(8 * D_MODEL) // (3 * 256)) * 256

LANES = 128
MXU_TILE = 256
CONV_HALO = 16
VMEM_LIMIT_BYTES = 56 * 1024 * 1024

LOG2E = math.log2(math.e)
SH1, SC1, G1, SH2, SC2, G2 = range(6)


def _cparams(n_axes):
    return pltpu.CompilerParams(dimension_semantics=("parallel",) * n_axes,
                                vmem_limit_bytes=VMEM_LIMIT_BYTES)


def _const_spec(shape):
    nd = len(shape)
    return pl.BlockSpec(shape, lambda *_: (0,) * nd, pipeline_mode=pl.Buffered(1))


def _mod(mod_ref, row, idx):
    return mod_ref[row:row + 1, idx * D_MODEL:(idx + 1) * D_MODEL]


def _mod_norm(h, gain, scale, shift):
    ms = jnp.mean(h * h, axis=-1, keepdims=True)
    return (h * lax.rsqrt(ms + NORM_EPS)) * (gain * (1.0 + scale)) + shift


def _silu(x):
    return x * (1.0 / (1.0 + jnp.exp(-x)))


def _row_tile(n):
    return 512 if n % 512 == 0 else n


def _ada_kernel(act_ref, w_ref, b_ref, o_ref):
    a = _silu(act_ref[...])
    o_ref[...] = jnp.dot(a.astype(BF16), w_ref[...].astype(BF16),
                         preferred_element_type=F32) + b_ref[...]


def _ada_mod(c, c_ctx, ada_w, ada_b):
    act = jnp.zeros((8, D_MODEL), F32).at[0].set(c[0]).at[1].set(c_ctx)
    cols = 1536
    nc = 6 * D_MODEL // cols
    return pl.pallas_call(
        _ada_kernel,
        grid=(DEPTH, nc),
        in_specs=[pl.BlockSpec((8, D_MODEL), lambda i, j: (0, 0)),
                  pl.BlockSpec((None, D_MODEL, cols), lambda i, j: (i, 0, j)),
                  pl.BlockSpec((None, 1, cols), lambda i, j: (i, 0, j))],
        out_specs=pl.BlockSpec((None, 8, cols), lambda i, j: (i, 0, j)),
        out_shape=jax.ShapeDtypeStruct((DEPTH, 8, 6 * D_MODEL), F32),
        compiler_params=_cparams(2),
        name="ada_mod",
    )(act, ada_w, ada_b.reshape(DEPTH, 1, 6 * D_MODEL))


def _group_mean_matrix(width):
    g = np.arange(width) // HEAD_DIM
    return jnp.asarray((g[:, None] == g[None, :]).astype(np.float32) / HEAD_DIM, BF16)


def _rope_tables(n, t):
    n_freq = HEAD_DIM // 4
    inv_freq = ROPE_THETA ** (-jnp.arange(n_freq, dtype=F32) / n_freq)
    reps = LANES // HEAD_DIM

    def tables(count, is_row):
        ang = jnp.arange(count, dtype=F32)[:, None] * inv_freq[None, :]
        cos, sin, zero = jnp.cos(ang), jnp.sin(ang), jnp.zeros_like(ang)
        halves = [[cos, cos], [-sin, zero], [zero, sin]]
        out = []
        for first, second in halves:
            pair = [first, second, zero, zero] if is_row else [zero, zero, first, second]
            out.append(jnp.tile(jnp.concatenate(pair, axis=-1), (1, reps)))
        return jnp.concatenate(out, axis=-1)

    return tables(n // GRID_W, True), jnp.tile(tables(GRID_W, False), (t // GRID_W, 1))


def _rope(x, tab):
    w = x.shape[-1]
    reps = w // LANES
    cos = jnp.tile(tab[:, 0:LANES], (1, reps))
    sin_up = jnp.tile(tab[:, LANES:2 * LANES], (1, reps))
    sin_dn = jnp.tile(tab[:, 2 * LANES:3 * LANES], (1, reps))
    quarter = HEAD_DIM // 4
    return (x * cos + pltpu.roll(x, w - quarter, 1) * sin_up
            + pltpu.roll(x, quarter, 1) * sin_dn)


QKV_SUB_ROWS = 256


def _qkv_kernel(row, rotary, kw, q_scale, n_alias, *refs):
    refs = list(refs)
    del refs[-3 - n_alias:-3]
    if rotary:
        (h_ref, mod_ref, n1_ref, w_ref, gm_ref, qg_ref, kg_ref, rt_ref, ct_ref,
         q_ref, k_ref, v_ref) = refs
    else:
        h_ref, mod_ref, n1_ref, w_ref, gm_ref, qg_ref, kg_ref, q_ref, k_ref, v_ref = refs
    gm = gm_ref[...]

    def head_norm(x, gain, width):
        x2 = (x * x).astype(BF16)
        ms = jnp.concatenate(
            [jnp.dot(x2[:, c:c + MXU_TILE], gm, preferred_element_type=F32)
             for c in range(0, width, MXU_TILE)], axis=1)
        return x * lax.rsqrt(ms + NORM_EPS) * gain

    t = h_ref.shape[0]
    sub = min(QKV_SUB_ROWS, t)
    for sb in range(t // sub):
        rs = slice(sb * sub, (sb + 1) * sub)
        a = _mod_norm(h_ref[rs, :], n1_ref[...], _mod(mod_ref, row, SC1), _mod(mod_ref, row, SH1))
        y = jnp.dot(a.astype(BF16), w_ref[...], preferred_element_type=F32)
        q = head_norm(y[:, :D_MODEL], qg_ref[...], D_MODEL)
        k = head_norm(y[:, D_MODEL:D_MODEL + kw], kg_ref[...], kw)
        if rotary:
            g0 = sb * sub // GRID_W
            row_part = jnp.concatenate(
                [jnp.broadcast_to(rt_ref[g0 + r:g0 + r + 1, :], (GRID_W, 3 * LANES))
                 for r in range(sub // GRID_W)], axis=0)
            tab = row_part + ct_ref[rs, :]
            q, k = _rope(q, tab), _rope(k, tab)
        q_ref[rs, :] = (q * q_scale).astype(BF16)
        k_ref[rs, :] = k.astype(BF16)
        v_ref[rs, :] = y[:, D_MODEL + kw:].astype(BF16)


def _qkv_proj(h, mods, layer, row, n1, w, q_gain, k_gain, kw, vw, q_scale, tab,
              kv_into=None, kv_row_offset=0):
    n = h.shape[0]
    t = _row_tile(n)
    rotary = tab is not None
    wtot = D_MODEL + kw + vw
    in_specs = [pl.BlockSpec((t, D_MODEL), lambda i: (i, 0)),
                pl.BlockSpec((None, 8, 6 * D_MODEL), lambda i: (layer, 0, 0)),
                _const_spec((1, D_MODEL)),
                _const_spec((D_MODEL, wtot)),
                _const_spec((MXU_TILE, MXU_TILE)),
                _const_spec((1, D_MODEL)),
                _const_spec((1, kw))]
    args = [h, mods, n1.reshape(1, D_MODEL), w, _group_mean_matrix(MXU_TILE),
            jnp.tile(q_gain, D_MODEL // HEAD_DIM).reshape(1, D_MODEL),
            jnp.tile(k_gain, kw // HEAD_DIM).reshape(1, kw)]
    if rotary:
        row_tab, col_tab = tab
        in_specs += [pl.BlockSpec((t // GRID_W, 3 * LANES), lambda i: (i, 0)),
                     _const_spec((t, 3 * LANES))]
        args += [row_tab, col_tab]
    aliases = {}
    off = kv_row_offset // t
    kv_total = n
    if kv_into is not None:
        kv_total = kv_into[0].shape[0]
        aliases = {len(args): 1, len(args) + 1: 2}
        in_specs += [pl.BlockSpec(memory_space=pl.ANY)] * 2
        args += list(kv_into)
    return pl.pallas_call(
        functools.partial(_qkv_kernel, row, rotary, kw, q_scale, len(aliases)),
        grid=(n // t,),
        in_specs=in_specs,
        out_specs=[pl.BlockSpec((t, D_MODEL), lambda i: (i, 0)),
                   pl.BlockSpec((t, kw), lambda i: (i + off, 0)),
                   pl.BlockSpec((t, vw), lambda i: (i + off, 0))],
        out_shape=[jax.ShapeDtypeStruct((n, D_MODEL), BF16),
                   jax.ShapeDtypeStruct((kv_total, kw), BF16),
                   jax.ShapeDtypeStruct((kv_total, vw), BF16)],
        input_output_aliases=aliases,
        compiler_params=_cparams(1),
        name="qkv_proj",
    )(*args)


DIFF_TQ = 256
DIFF_TK = 1280
DIFF_VT_ROWS = LANES + 16


def _diff_attn_kernel(tq, tk, n_tiles, n_chunks, lam_init, q_ref, lamv_ref, sub_ref, k_ref, v_ref,
                      o_ref, qq_ref, vt_ref, s_ref, m_ref, acc_ref):
    n_keys = n_chunks * tk
    lane = lax.broadcasted_iota(jnp.int32, (tq, LANES), 1)
    lv = lamv_ref[...]
    lam = (jnp.exp(jnp.sum(lv[0:1] * lv[1:2], axis=-1, keepdims=True))
           - jnp.exp(jnp.sum(lv[2:3] * lv[3:4], axis=-1, keepdims=True)) + lam_init)
    out_gain = sub_ref[...] * (1.0 - lam_init)

    for c in range(n_chunks):
        ks = slice(c * tk, (c + 1) * tk)
        vt_ref[0:LANES, ks] = v_ref[ks, :].astype(F32).T.astype(BF16)
    vt_ref[LANES:, :] = jnp.ones((DIFF_VT_ROWS - LANES, n_keys), BF16)

    def rows_of(t):
        return pl.ds(t * tq, tq) if isinstance(t, int) else pl.ds(pl.multiple_of(t * tq, tq), tq)

    def stack_queries(t, slot):
        q = q_ref[rows_of(t), :]
        zero = jnp.zeros_like(q)
        qq_ref[slot, 0:tq, :] = jnp.where(lane < HEAD_DIM, q, zero)
        qq_ref[slot, tq:, :] = jnp.where(lane >= HEAD_DIM, q, zero)

    def scores(slot, c, buf):
        s_ref[buf] = lax.dot_general(k_ref[c * tk:(c + 1) * tk, :], qq_ref[slot],
                                     (((1,), (1,)), ((), ())), preferred_element_type=F32)

    def softmax_pv(c, buf):
        s = s_ref[buf]
        m_prev = m_ref[...]
        m_new = jnp.maximum(m_prev, jnp.max(s, axis=0, keepdims=True))
        alpha = jnp.exp2(m_prev - m_new)
        p = jnp.exp2(s - m_new).astype(BF16)
        pv = jnp.dot(vt_ref[:, c * tk:(c + 1) * tk], p, preferred_element_type=F32)
        acc_ref[...] = acc_ref[...] * alpha + pv
        m_ref[...] = m_new

    def tile(t, t_next, slot, first_buf):
        m_ref[...] = jnp.full(m_ref.shape, NEG_INF, F32)
        acc_ref[...] = jnp.zeros(acc_ref.shape, F32)
        for c in range(n_chunks):
            buf = (first_buf + c) % 2
            if c + 1 < n_chunks:
                scores(slot, c + 1, 1 - buf)
            elif t_next is not None:
                stack_queries(t_next, 1 - slot)
                scores(1 - slot, 0, 1 - buf)
            softmax_pv(c, buf)
        acc = acc_ref[...]
        o_t = acc[0:LANES, :] / acc[LANES:LANES + 1, :]
        d = (o_t[:, :tq] - lam * o_t[:, tq:]).T
        ms = jnp.mean(d * d, axis=-1, keepdims=True)
        o_ref[rows_of(t), :] = (d * lax.rsqrt(ms + NORM_EPS) * out_gain).astype(BF16)

    stack_queries(0, 0)
    scores(0, 0, 0)
    flip = n_chunks % 2
    if n_tiles % 2 == 0 and n_tiles > 2:
        def pair(tp, carry):
            t0 = 2 * tp
            tile(t0, t0 + 1, 0, 0)
            tile(t0 + 1, jnp.minimum(t0 + 2, n_tiles - 1), 1, flip)
            return carry
        lax.fori_loop(0, n_tiles // 2, pair, 0)
    else:
        for t in range(n_tiles):
            tile(t, t + 1 if t + 1 < n_tiles else None, t % 2, (t * flip) % 2)


def _diff_attn(q, k, v, key_start, n_keys, lamv, subln, lam_init):
    nq = q.shape[0]
    tq = min(DIFF_TQ, nq)
    tk = min(DIFF_TK, n_keys)
    n_chunks = n_keys // tk
    assert nq % tq == 0 and n_keys % tk == 0 and key_start % n_keys == 0
    kb = key_start // n_keys
    return pl.pallas_call(
        functools.partial(_diff_attn_kernel, tq, tk, nq // tq, n_chunks, lam_init),
        grid=(DIFF_HEADS,),
        in_specs=[pl.BlockSpec((nq, LANES), lambda h: (0, h)),
                  _const_spec((8, LANES)),
                  _const_spec((1, LANES)),
                  pl.BlockSpec((n_keys, LANES), lambda h: (kb, h)),
                  pl.BlockSpec((n_keys, LANES), lambda h: (kb, h))],
        out_specs=pl.BlockSpec((nq, LANES), lambda h: (0, h)),
        out_shape=jax.ShapeDtypeStruct((nq, D_MODEL), BF16),
        scratch_shapes=[pltpu.VMEM((2, 2 * tq, LANES), BF16),
                        pltpu.VMEM((DIFF_VT_ROWS, n_keys), BF16),
                        pltpu.VMEM((2, tk, 2 * tq), F32),
                        pltpu.VMEM((1, 2 * tq), F32),
                        pltpu.VMEM((DIFF_VT_ROWS, 2 * tq), F32)],
        compiler_params=_cparams(1),
        name="diff_attn",
    )(q, lamv, subln.reshape(1, LANES), k, v)


def _win_attn_kernel(tq, n_lat, with_window, *refs):
    if with_window:
        (q_ref, kp_ref, kc_ref, kn_ref, vp_ref, vc_ref, vn_ref,
         kx_ref, vx_ref, sink_ref, o_ref, s_ref) = refs
    else:
        q_ref, kx_ref, vx_ref, sink_ref, o_ref, s_ref = refs
    i = pl.program_id(0)
    n_ctx = kx_ref.shape[0]
    sub = WINDOW if with_window else tq
    n_win = 3 * WINDOW if with_window else 0
    n_keys = n_win + n_ctx
    low = lax.broadcasted_iota(jnp.int32, (sub, LANES), 1) < HEAD_DIM
    ones = jnp.ones((n_keys, LANES), BF16)
    nt = (((1,), (1,)), ((), ()))
    units = [(sb, kvh) for sb in range(tq // sub) for kvh in range(WIN_KV_HEADS)]

    def keys_or_values(u, p_ref, c_ref, n_ref, x_ref):
        sb, kvh = units[u]
        ks = slice(kvh * LANES, (kvh + 1) * LANES)
        if not with_window:
            return x_ref[:, ks]
        span = jnp.concatenate([p_ref[:, ks], c_ref[:, ks], n_ref[:, ks]], axis=0)
        return jnp.concatenate([span[sb * WINDOW:sb * WINDOW + n_win], x_ref[:, ks]], axis=0)

    def scores(u, buf):
        sb, kvh = units[u]
        r0 = sb * sub
        qs = []
        for pr in range(WIN_GROUP // 2):
            blk = kvh * (WIN_GROUP // 2) + pr
            qp = q_ref[r0:r0 + sub, blk * LANES:(blk + 1) * LANES]
            zero = jnp.zeros_like(qp)
            qs += [jnp.where(low, qp, zero), jnp.where(low, zero, qp)]
        kk = keys_or_values(u, kp_ref, kc_ref, kn_ref, kx_ref) if with_window else \
            keys_or_values(u, None, None, None, kx_ref)
        s_ref[buf] = lax.dot_general(jnp.concatenate(qs, axis=0), kk, nt,
                                     preferred_element_type=F32)

    def softmax_pv(u, buf):
        sb, kvh = units[u]
        r0 = sb * sub
        s = s_ref[buf]
        if with_window:
            qpos = i * tq + r0 + lax.broadcasted_iota(jnp.int32, (sub, n_win), 0)
            kpos = i * tq + r0 - WINDOW + lax.broadcasted_iota(jnp.int32, (sub, n_win), 1)
            valid = (jnp.abs(kpos - qpos) <= WINDOW) & (kpos >= 0) & (kpos < n_lat)
            bias = jnp.where(valid, 0.0, NEG_INF)
            bias = jnp.concatenate([bias] * WIN_GROUP, axis=0)
            s = jnp.concatenate([s[:, :n_win] + bias, s[:, n_win:]], axis=1)
        heads = [kvh * WIN_GROUP + g for g in range(WIN_GROUP)]
        sink = jnp.concatenate(
            [jnp.broadcast_to(sink_ref[hd:hd + 1, :], (sub, LANES)) for hd in heads], axis=0)
        m = jnp.maximum(jnp.max(s, axis=-1, keepdims=True), sink)
        p = jnp.exp(s - jnp.tile(m, (1, n_keys // LANES)))
        vv = keys_or_values(u, vp_ref, vc_ref, vn_ref, vx_ref) if with_window else \
            keys_or_values(u, None, None, None, vx_ref)
        pv = jnp.dot(p.astype(BF16), jnp.concatenate([vv, ones], axis=1),
                     preferred_element_type=F32)
        den = pv[:, LANES:LANES + 1] + jnp.exp(sink[:, 0:1] - m[:, 0:1])
        o = pv[:, :LANES] / den
        for pr in range(WIN_GROUP // 2):
            blk = kvh * (WIN_GROUP // 2) + pr
            lo = o[(2 * pr) * sub:(2 * pr + 1) * sub]
            hi = o[(2 * pr + 1) * sub:(2 * pr + 2) * sub]
            o_ref[r0:r0 + sub, blk * LANES:(blk + 1) * LANES] = jnp.where(low, lo, hi).astype(BF16)

    scores(0, 0)
    for u in range(len(units)):
        if u + 1 < len(units):
            scores(u + 1, (u + 1) % 2)
        softmax_pv(u, u % 2)


def _win_attn(q, k, v, kx, vx, sink_rows, with_window):
    nq = q.shape[0]
    tq = min(256, nq)
    n_ctx = kx.shape[0]
    kvw = WIN_KV_HEADS * LANES
    in_specs = [pl.BlockSpec((tq, D_MODEL), lambda i: (i, 0))]
    args = [q]
    if with_window:
        r = tq // WINDOW
        last = nq // WINDOW - 1
        prev_spec = pl.BlockSpec((WINDOW, kvw), lambda i: (jnp.maximum(i * r - 1, 0), 0))
        cur_spec = pl.BlockSpec((tq, kvw), lambda i: (i, 0))
        next_spec = pl.BlockSpec((WINDOW, kvw), lambda i: (jnp.minimum((i + 1) * r, last), 0))
        in_specs += [prev_spec, cur_spec, next_spec] * 2
        args += [k, k, k, v, v, v]
    in_specs += [_const_spec((n_ctx, kvw)), _const_spec((n_ctx, kvw)),
                 _const_spec((WIN_Q_HEADS, LANES))]
    args += [kx, vx, sink_rows]
    return pl.pallas_call(
        functools.partial(_win_attn_kernel, tq, nq, with_window),
        grid=(nq // tq,),
        in_specs=in_specs,
        out_specs=pl.BlockSpec((tq, D_MODEL), lambda i: (i, 0)),
        out_shape=jax.ShapeDtypeStruct((nq, D_MODEL), BF16),
        scratch_shapes=[pltpu.VMEM((2, WIN_GROUP * (WINDOW if with_window else tq),
                                    (3 * WINDOW if with_window else 0) + n_ctx), F32)],
        compiler_params=_cparams(1),
        name="win_attn",
    )(*args)


_FFN_SPLIT = -(-(D_FF // MXU_TILE) // 2) * MXU_TILE
FFN_CHUNKS = ((0, _FFN_SPLIT), (_FFN_SPLIT, D_FF))
FFN_SUB_ROWS = 256


def _ffn_kernel(row, fused, *refs):
    if fused:
        x_ref, wo_ref, h_ref, mod_ref, n2_ref, wgu_ref, wd_ref, o_ref = refs
    else:
        h_ref, mod_ref, n2_ref, wgu_ref, wd_ref, o_ref = refs
    t = h_ref.shape[0]
    sub = min(FFN_SUB_ROWS, t)
    for sb in range(t // sub):
        rs = slice(sb * sub, (sb + 1) * sub)
        h = h_ref[rs, :]
        if fused:
            h = h + _mod(mod_ref, row, G1) * jnp.dot(x_ref[rs, :], wo_ref[...],
                                                     preferred_element_type=F32)
        a = _mod_norm(h, n2_ref[...], _mod(mod_ref, row, SC2), _mod(mod_ref, row, SH2)).astype(BF16)
        acc = None
        for lo, hi in FFN_CHUNKS:
            g = jnp.dot(a, wgu_ref[:, lo:hi], preferred_element_type=F32)
            u = jnp.dot(a, wgu_ref[:, D_FF + lo:D_FF + hi], preferred_element_type=F32)
            act = (_silu(g) * u).astype(BF16)
            part = jnp.dot(act, wd_ref[lo:hi, :], preferred_element_type=F32)
            acc = part if acc is None else acc + part
        o_ref[rs, :] = h + _mod(mod_ref, row, G2) * acc


def _ffn(h, mods, layer, row, n2, wgu, wd, attn=None):
    n = h.shape[0]
    t = _row_tile(n)
    row_spec = pl.BlockSpec((t, D_MODEL), lambda i: (i, 0))
    in_specs = [row_spec,
                pl.BlockSpec((None, 8, 6 * D_MODEL), lambda i: (layer, 0, 0)),
                _const_spec((1, D_MODEL)),
                _const_spec((D_MODEL, 2 * D_FF)),
                _const_spec((D_FF, D_MODEL))]
    args = [h, mods, n2.reshape(1, D_MODEL), wgu, wd]
    if attn is not None:
        in_specs = [row_spec, _const_spec((D_MODEL, D_MODEL))] + in_specs
        args = list(attn) + args
    return pl.pallas_call(
        functools.partial(_ffn_kernel, row, attn is not None),
        grid=(n // t,),
        in_specs=in_specs,
        out_specs=row_spec,
        out_shape=jax.ShapeDtypeStruct((n, D_MODEL), F32),
        compiler_params=_cparams(1),
        name="ffn",
    )(*args)


def _in_proj_kernel(row, mode, *refs):
    if mode == "short":
        h_ref, mod_ref, n1_ref, w_ref, b_out, cu_out = refs
    else:
        h_ref, mod_ref, n1_ref, w_ref, bias_ref, glu_out = refs
    t = h_ref.shape[0]
    sub = min(QKV_SUB_ROWS, t)
    for sb in range(t // sub):
        rs = slice(sb * sub, (sb + 1) * sub)
        a = _mod_norm(h_ref[rs, :], n1_ref[...], _mod(mod_ref, row, SC1), _mod(mod_ref, row, SH1))
        y = jnp.dot(a.astype(BF16), w_ref[...], preferred_element_type=F32)
        if mode == "short":
            b_out[rs, :] = y[:, :D_MODEL]
            cu_out[rs, :] = y[:, D_MODEL:2 * D_MODEL] * y[:, 2 * D_MODEL:]
        else:
            y = y + bias_ref[...]
            g = y[:, D_MODEL:]
            glu_out[rs, :] = y[:, :D_MODEL] * (1.0 / (1.0 + jnp.exp(-g)))


def _in_proj(h, mods, layer, row, n1, w, mode, bias=None):
    n = h.shape[0]
    t = _row_tile(n)
    wtot = w.shape[1]
    in_specs = [pl.BlockSpec((t, D_MODEL), lambda i: (i, 0)),
                pl.BlockSpec((None, 8, 6 * D_MODEL), lambda i: (layer, 0, 0)),
                _const_spec((1, D_MODEL)),
                _const_spec((D_MODEL, wtot))]
    args = [h, mods, n1.reshape(1, D_MODEL), w]
    row_spec = pl.BlockSpec((t, D_MODEL), lambda i: (i, 0))
    row_shape = jax.ShapeDtypeStruct((n, D_MODEL), F32)
    if mode == "short":
        out_specs, out_shape = [row_spec, row_spec], [row_shape, row_shape]
    else:
        in_specs.append(_const_spec((1, wtot)))
        args.append(bias.reshape(1, wtot))
        out_specs, out_shape = row_spec, row_shape
    return pl.pallas_call(
        functools.partial(_in_proj_kernel, row, mode),
        grid=(n // t,),
        in_specs=in_specs,
        out_specs=out_specs,
        out_shape=out_shape,
        compiler_params=_cparams(1),
        name="in_proj_" + mode,
    )(*args)


CONV_ROW_BLOCK = 64
SUBLANES = 8


def _conv_shifts(taps):
    return [divmod(CONV_HALO - taps // 2 + k, SUBLANES) for k in range(taps)]


def _conv_rems(taps):
    return sorted({r for _, r in _conv_shifts(taps) if r})


def _conv_span(taps, t):
    return t + SUBLANES * max(a for a, _ in _conv_shifts(taps))


def _dwconv_tile(xs_ref, sh_ref, u_ref, prev_ref, cur_ref, next_ref, w_ref, taps, t):
    i = pl.program_id(0)
    last = pl.num_programs(0) - 1
    ncb = D_MODEL // LANES
    for cb in range(ncb):
        cs = slice(cb * LANES, (cb + 1) * LANES)
        xs_ref[cb, 0:CONV_HALO, :] = jnp.where(i > 0, prev_ref[:, cs], 0.0)
        xs_ref[cb, CONV_HALO:CONV_HALO + t, :] = cur_ref[:, cs]
        xs_ref[cb, CONV_HALO + t:, :] = jnp.where(i < last, next_ref[:, cs], 0.0)
    shifts = _conv_shifts(taps)
    rems = _conv_rems(taps)
    span = _conv_span(taps, t)
    for j, r in enumerate(rems):
        sh_ref[j] = xs_ref[:, r:r + span, :]

    def block(idx, carry):
        rb, cb = idx // ncb, idx % ncb
        row0 = pl.multiple_of(rb * CONV_ROW_BLOCK, CONV_ROW_BLOCK)
        accs = [None, None]
        for k, (a, r) in enumerate(shifts):
            rows = pl.ds(row0 + a * SUBLANES, CONV_ROW_BLOCK)
            x = xs_ref[cb, rows, :] if r == 0 else sh_ref[rems.index(r), cb, rows, :]
            term = x * w_ref[cb, k:k + 1, :]
            accs[k % 2] = term if accs[k % 2] is None else accs[k % 2] + term
        u_ref[cb, pl.ds(row0, CONV_ROW_BLOCK), :] = accs[0] + accs[1]
        return carry

    lax.fori_loop(0, (t // CONV_ROW_BLOCK) * ncb, block, 0)
    return jnp.concatenate([u_ref[cb] for cb in range(ncb)], axis=1)


def _conv_out_kernel(row, mode, taps, t, *refs):
    if mode == "short":
        (prev_ref, cur_ref, next_ref, cw_ref, b_ref, h_ref, mod_ref, w_ref, o_ref,
         xs_ref, sh_ref, u_ref) = refs
    else:
        (prev_ref, cur_ref, next_ref, cw_ref, dwb_ref, lng_ref, lnb_ref, h_ref, mod_ref,
         w_ref, pb_ref, o_ref, xs_ref, sh_ref, u_ref) = refs
    u = _dwconv_tile(xs_ref, sh_ref, u_ref, prev_ref, cur_ref, next_ref, cw_ref, taps, t)
    if mode == "short":
        y = jnp.dot((b_ref[...] * u).astype(BF16), w_ref[...], preferred_element_type=F32)
    else:
        u = u + dwb_ref[...]
        mu = jnp.mean(u, axis=-1, keepdims=True)
        uc = u - mu
        var = jnp.mean(uc * uc, axis=-1, keepdims=True)
        z = _silu(uc * lax.rsqrt(var + NORM_EPS) * lng_ref[...] + lnb_ref[...])
        y = jnp.dot(z.astype(BF16), w_ref[...], preferred_element_type=F32) + pb_ref[...]
    o_ref[...] = h_ref[...] + _mod(mod_ref, row, G1) * y


def _conv_out(x, h, mods, layer, row, conv_w, w, mode, extra):
    n = h.shape[0]
    t = _row_tile(n)
    taps = conv_w.shape[0]
    r = t // CONV_HALO
    last = n // CONV_HALO - 1
    taps_pad = -(-taps // 8) * 8
    ncb = D_MODEL // LANES
    cw = jnp.zeros((taps_pad, D_MODEL), F32).at[:taps].set(conv_w)
    cw = cw.reshape(taps_pad, ncb, LANES).transpose(1, 0, 2)
    row_spec = pl.BlockSpec((t, D_MODEL), lambda i: (i, 0))
    vec_spec = _const_spec((1, D_MODEL))
    in_specs = [pl.BlockSpec((CONV_HALO, D_MODEL), lambda i: (jnp.maximum(i * r - 1, 0), 0)),
                row_spec,
                pl.BlockSpec((CONV_HALO, D_MODEL), lambda i: (jnp.minimum((i + 1) * r, last), 0)),
                _const_spec((ncb, taps_pad, LANES))]
    args = [x, x, x, cw]
    mod_spec = pl.BlockSpec((None, 8, 6 * D_MODEL), lambda i: (layer, 0, 0))
    if mode == "short":
        (b_gate,) = extra
        in_specs += [row_spec, row_spec, mod_spec, _const_spec((D_MODEL, D_MODEL))]
        args += [b_gate, h, mods, w]
    else:
        dw_b, ln_g, ln_b, pw_b = extra
        in_specs += [vec_spec, vec_spec, vec_spec, row_spec, mod_spec,
                     _const_spec((D_MODEL, D_MODEL)), vec_spec]
        args += [dw_b.reshape(1, D_MODEL), ln_g.reshape(1, D_MODEL), ln_b.reshape(1, D_MODEL),
                 h, mods, w, pw_b.reshape(1, D_MODEL)]
    return pl.pallas_call(
        functools.partial(_conv_out_kernel, row, mode, taps, t),
        grid=(n // t,),
        in_specs=in_specs,
        out_specs=row_spec,
        out_shape=jax.ShapeDtypeStruct((n, D_MODEL), F32),
        scratch_shapes=[pltpu.VMEM((ncb, t + 2 * CONV_HALO, LANES), F32),
                        pltpu.VMEM((len(_conv_rems(taps)), ncb, _conv_span(taps, t), LANES), F32),
                        pltpu.VMEM((ncb, t, LANES), F32)],
        compiler_params=_cparams(1),
        name="conv_out_" + mode,
    )(*args)


def _diff_qkv_weight(w_qkv):
    idx = np.arange(D_MODEL).reshape(2, DIFF_HEADS, HEAD_DIM).transpose(1, 0, 2).reshape(-1)
    cols = np.concatenate([idx, D_MODEL + idx, 2 * D_MODEL + np.arange(D_MODEL)])
    return w_qkv[:, cols].astype(BF16)


def _win_qkv_weight(w_qkv):
    qw = WIN_Q_HEADS * HEAD_DIM
    kvw = WIN_KV_HEADS * HEAD_DIM
    dup = np.repeat(np.arange(kvw).reshape(WIN_KV_HEADS, 1, HEAD_DIM), 2, axis=1).reshape(-1)
    cols = np.concatenate([np.arange(qw), qw + dup, qw + kvw + dup])
    return w_qkv[:, cols].astype(BF16)


def kernel(x, c, ctx, c_ctx, ada_w, ada_b, norm1, norm2, ffn_w_gate_up, ffn_w_down, diff_w_qkv, diff_w_o, diff_q_norm, diff_k_norm, diff_lam_q1, diff_lam_k1, diff_lam_q2, diff_lam_k2, diff_subln, sc_w_in, sc_conv_w, sc_w_out, win_w_qkv, win_w_o, win_q_norm, win_k_norm, win_sink, cf_w_pw1, cf_b_pw1, cf_dw_w, cf_dw_b, cf_ln_g, cf_ln_b, cf_w_pw2, cf_b_pw2):
    n = x.shape[1]
    h, hc = x.reshape(n, D_MODEL), ctx.reshape(ctx.shape[1], D_MODEL)
    mods = _ada_mod(c, c_ctx, ada_w, ada_b)
    tab = _rope_tables(n, _row_tile(n))
    for i in range(DEPTH):
        kind, j = i % N_MIXERS, i // N_MIXERS
        with_ctx = i < DEPTH - 1
        attn_l = attn_c = None
        if kind == 0:
            lam_init = 0.8 - 0.6 * math.exp(-0.3 * i)
            w = _diff_qkv_weight(diff_w_qkv[j])
            w_o = diff_w_o[j].astype(BF16)
            lamv = jnp.zeros((8, LANES), F32).at[0:4, 0:HEAD_DIM].set(
                jnp.stack([diff_lam_q1[j], diff_lam_k1[j], diff_lam_q2[j], diff_lam_k2[j]]))
            q_scale = HEAD_DIM ** -0.5 * LOG2E
            proj = functools.partial(_qkv_proj, mods=mods, layer=i, n1=norm1[i], w=w,
                                     q_gain=diff_q_norm[j], k_gain=diff_k_norm[j],
                                     kw=D_MODEL, vw=D_MODEL, q_scale=q_scale)
            n_ctx = hc.shape[0]
            kv0 = (jnp.zeros((n + n_ctx, D_MODEL), BF16), jnp.zeros((n + n_ctx, D_MODEL), BF16))
            q_l, k_all, v_all = proj(h, row=0, tab=tab, kv_into=kv0)
            q_c, k_all, v_all = proj(hc, row=1, tab=None, kv_into=(k_all, v_all), kv_row_offset=n)
            o_l = _diff_attn(q_l, k_all, v_all, 0, n + n_ctx, lamv, diff_subln[j], lam_init)
            attn_l = (o_l, w_o)
            if with_ctx:
                attn_c = (_diff_attn(q_c, k_all, v_all, n, n_ctx, lamv, diff_subln[j], lam_init), w_o)
        elif kind == 1:
            w_in = sc_w_in[j].astype(BF16)
            w_out = sc_w_out[j].astype(BF16)
            b_l, cu_l = _in_proj(h, mods, i, 0, norm1[i], w_in, "short")
            h = _conv_out(cu_l, h, mods, i, 0, sc_conv_w[j], w_out, "short", (b_l,))
            if with_ctx:
                b_c, cu_c = _in_proj(hc, mods, i, 1, norm1[i], w_in, "short")
                hc = _conv_out(cu_c, hc, mods, i, 1, sc_conv_w[j], w_out, "short", (b_c,))
        elif kind == 2:
            w = _win_qkv_weight(win_w_qkv[j])
            w_o = win_w_o[j].astype(BF16)
            kvw = WIN_KV_HEADS * LANES
            sink_rows = jnp.broadcast_to(win_sink[j][:, None], (WIN_Q_HEADS, LANES))
            proj = functools.partial(_qkv_proj, mods=mods, layer=i, n1=norm1[i], w=w,
                                     q_gain=win_q_norm[j], k_gain=win_k_norm[j],
                                     kw=kvw, vw=kvw, q_scale=HEAD_DIM ** -0.5)
            q_l, k_l, v_l = proj(h, row=0, tab=tab)
            q_c, k_c, v_c = proj(hc, row=1, tab=None)
            o_l = _win_attn(q_l, k_l, v_l, k_c, v_c, sink_rows, True)
            attn_l = (o_l, w_o)
            if with_ctx:
                attn_c = (_win_attn(q_c, None, None, k_c, v_c, sink_rows, False), w_o)
        else:
            w1 = cf_w_pw1[j].astype(BF16)
            w2 = cf_w_pw2[j].astype(BF16)
            extra = (cf_dw_b[j], cf_ln_g[j], cf_ln_b[j], cf_b_pw2[j])
            glu_l = _in_proj(h, mods, i, 0, norm1[i], w1, "glu", cf_b_pw1[j])
            h = _conv_out(glu_l, h, mods, i, 0, cf_dw_w[j], w2, "conf", extra)
            if with_ctx:
                glu_c = _in_proj(hc, mods, i, 1, norm1[i], w1, "glu", cf_b_pw1[j])
                hc = _conv_out(glu_c, hc, mods, i, 1, cf_dw_w[j], w2, "conf", extra)
        wgu = ffn_w_gate_up[i].astype(BF16)
        wd = ffn_w_down[i].astype(BF16)
        h = _ffn(h, mods, i, 0, norm2[i], wgu, wd, attn_l)
        if with_ctx:
            hc = _ffn(hc, mods, i, 1, norm2[i], wgu, wd, attn_c)
    return h.reshape(1, n, D_MODEL)
```

```python
import functools
import math

import jax
import jax.numpy as jnp
import numpy as np
from jax import lax
from jax.experimental import pallas as pl
from jax.experimental.pallas import tpu as pltpu

F32 = jnp.float32
BF16 = jnp.bfloat16

D_MODEL = 1024
DEPTH = 4
N_MIXERS = 4
GRID_W = 64
NORM_EPS = 1e-6
ROPE_THETA = 10000.0
HEAD_DIM = 64
NEG_INF = -1e30
DIFF_HEADS = D_MODEL // (2 * HEAD_DIM)
SHORT_CONV_W = 3
WIN_Q_HEADS = D_MODEL // HEAD_DIM
WIN_KV_HEADS = 4
WIN_GROUP = WIN_Q_HEADS // WIN_KV_HEADS
WINDOW = 128
CONF_CONV_W = 31
D_FF = -(-(8 * D_MODEL) // (3 * 256)) * 256

LANES = 128
MXU_TILE = 256
CONV_HALO = 16
VMEM_LIMIT_BYTES = 56 * 1024 * 1024

LOG2E = math.log2(math.e)
SH1, SC1, G1, SH2, SC2, G2 = range(6)


def _cparams(n_axes):
    return pltpu.CompilerParams(dimension_semantics=("parallel",) * n_axes,
                                vmem_limit_bytes=VMEM_LIMIT_BYTES)


def _const_spec(shape):
    nd = len(shape)
    return pl.BlockSpec(shape, lambda *_: (0,) * nd, pipeline_mode=pl.Buffered(1))


def _layer_spec(shape, layer):
    nd = len(shape)
    return pl.BlockSpec((None,) + tuple(shape), lambda *_: (layer,) + (0,) * nd,
                        pipeline_mode=pl.Buffered(1))


def _mod(mod_ref, row, idx):
    return mod_ref[row:row + 1, idx * D_MODEL:(idx + 1) * D_MODEL]


def _mod_norm(h, gain, scale, shift):
    ms = jnp.mean(h * h, axis=-1, keepdims=True)
    return (h * lax.rsqrt(ms + NORM_EPS)) * (gain * (1.0 + scale)) + shift


def _silu(x):
    return x * (1.0 / (1.0 + jnp.exp(-x)))


def _row_tile(n):
    return 512 if n % 512 == 0 else n


def _ada_kernel(act_ref, w_ref, b_ref, o_ref):
    a = _silu(act_ref[...])
    o_ref[...] = jnp.dot(a.astype(BF16), w_ref[...].astype(BF16),
                         preferred_element_type=F32) + b_ref[...]


def _ada_mod(c, c_ctx, ada_w, ada_b):
    act = jnp.zeros((8, D_MODEL), F32).at[0].set(c[0]).at[1].set(c_ctx)
    cols = 1536
    nc = 6 * D_MODEL // cols
    return pl.pallas_call(
        _ada_kernel,
        grid=(DEPTH, nc),
        in_specs=[pl.BlockSpec((8, D_MODEL), lambda i, j: (0, 0)),
                  pl.BlockSpec((None, D_MODEL, cols), lambda i, j: (i, 0, j)),
                  pl.BlockSpec((None, 1, cols), lambda i, j: (i, 0, j))],
        out_specs=pl.BlockSpec((None, 8, cols), lambda i, j: (i, 0, j)),
        out_shape=jax.ShapeDtypeStruct((DEPTH, 8, 6 * D_MODEL), F32),
        compiler_params=_cparams(2),
        name="ada_mod",
    )(act, ada_w, ada_b.reshape(DEPTH, 1, 6 * D_MODEL))


def _group_mean_matrix(width):
    g = np.arange(width) // HEAD_DIM
    return jnp.asarray((g[:, None] == g[None, :]).astype(np.float32) / HEAD_DIM, BF16)


def _rope_tables(n, t):
    n_freq = HEAD_DIM // 4
    inv_freq = ROPE_THETA ** (-jnp.arange(n_freq, dtype=F32) / n_freq)
    reps = LANES // HEAD_DIM

    def tables(count, is_row):
        ang = jnp.arange(count, dtype=F32)[:, None] * inv_freq[None, :]
        cos, sin, zero = jnp.cos(ang), jnp.sin(ang), jnp.zeros_like(ang)
        halves = [[cos, cos], [-sin, zero], [zero, sin]]
        out = []
        for first, second in halves:
            pair = [first, second, zero, zero] if is_row else [zero, zero, first, second]
            out.append(jnp.tile(jnp.concatenate(pair, axis=-1), (1, reps)))
        return jnp.concatenate(out, axis=-1)

    return tables(n // GRID_W, True), jnp.tile(tables(GRID_W, False), (t // GRID_W, 1))


def _rope(x, tab):
    w = x.shape[-1]
    reps = w // LANES
    cos = jnp.tile(tab[:, 0:LANES], (1, reps))
    sin_up = jnp.tile(tab[:, LANES:2 * LANES], (1, reps))
    sin_dn = jnp.tile(tab[:, 2 * LANES:3 * LANES], (1, reps))
    quarter = HEAD_DIM // 4
    return (x * cos + pltpu.roll(x, w - quarter, 1) * sin_up
            + pltpu.roll(x, quarter, 1) * sin_dn)


QKV_SUB_ROWS = 256


def _qkv_kernel(row, rotary, kw, q_scale, n_alias, *refs):
    refs = list(refs)
    del refs[-3 - n_alias:-3]
    if rotary:
        (h_ref, mod_ref, n1_ref, w_ref, gm_ref, qg_ref, kg_ref, rt_ref, ct_ref,
         q_ref, k_ref, v_ref) = refs
    else:
        h_ref, mod_ref, n1_ref, w_ref, gm_ref, qg_ref, kg_ref, q_ref, k_ref, v_ref = refs
    gm = gm_ref[...]

    def head_norm(x, gain, width):
        x2 = (x * x).astype(BF16)
        ms = jnp.concatenate(
            [jnp.dot(x2[:, c:c + MXU_TILE], gm, preferred_element_type=F32)
             for c in range(0, width, MXU_TILE)], axis=1)
        return x * lax.rsqrt(ms + NORM_EPS) * gain

    t = h_ref.shape[0]
    sub = min(QKV_SUB_ROWS, t)
    for sb in range(t // sub):
        rs = slice(sb * sub, (sb + 1) * sub)
        a = _mod_norm(h_ref[rs, :], n1_ref[...], _mod(mod_ref, row, SC1), _mod(mod_ref, row, SH1))
        y = jnp.dot(a.astype(BF16), w_ref[...], preferred_element_type=F32)
        q = head_norm(y[:, :D_MODEL], qg_ref[...], D_MODEL)
        k = head_norm(y[:, D_MODEL:D_MODEL + kw], kg_ref[...], kw)
        if rotary:
            g0 = sb * sub // GRID_W
            row_part = jnp.concatenate(
                [jnp.broadcast_to(rt_ref[g0 + r:g0 + r + 1, :], (GRID_W, 3 * LANES))
                 for r in range(sub // GRID_W)], axis=0)
            tab = row_part + ct_ref[rs, :]
            q, k = _rope(q, tab), _rope(k, tab)
        q_ref[rs, :] = (q * q_scale).astype(BF16)
        k_ref[rs, :] = k.astype(BF16)
        v_ref[rs, :] = y[:, D_MODEL + kw:].astype(BF16)


def _qkv_proj(h, mods, layer, row, n1, w, q_gain, k_gain, kw, vw, q_scale, tab,
              kv_into=None, kv_row_offset=0):
    n = h.shape[0]
    t = _row_tile(n)
    rotary = tab is not None
    wtot = D_MODEL + kw + vw
    in_specs = [pl.BlockSpec((t, D_MODEL), lambda i: (i, 0)),
                pl.BlockSpec((None, 8, 6 * D_MODEL), lambda i: (layer, 0, 0)),
                _const_spec((1, D_MODEL)),
                _const_spec((D_MODEL, wtot)),
                _const_spec((MXU_TILE, MXU_TILE)),
                _const_spec((1, D_MODEL)),
                _const_spec((1, kw))]
    args = [h, mods, n1.reshape(1, D_MODEL), w, _group_mean_matrix(MXU_TILE),
            jnp.tile(q_gain, D_MODEL // HEAD_DIM).reshape(1, D_MODEL),
            jnp.tile(k_gain, kw // HEAD_DIM).reshape(1, kw)]
    if rotary:
        row_tab, col_tab = tab
        in_specs += [pl.BlockSpec((t // GRID_W, 3 * LANES), lambda i: (i, 0)),
                     _const_spec((t, 3 * LANES))]
        args += [row_tab, col_tab]
    aliases = {}
    off = kv_row_offset // t
    kv_total = n
    if kv_into is not None:
        kv_total = kv_into[0].shape[0]
        aliases = {len(args): 1, len(args) + 1: 2}
        in_specs += [pl.BlockSpec(memory_space=pl.ANY)] * 2
        args += list(kv_into)
    return pl.pallas_call(
        functools.partial(_qkv_kernel, row, rotary, kw, q_scale, len(aliases)),
        grid=(n // t,),
        in_specs=in_specs,
        out_specs=[pl.BlockSpec((t, D_MODEL), lambda i: (i, 0)),
                   pl.BlockSpec((t, kw), lambda i: (i + off, 0)),
                   pl.BlockSpec((t, vw), lambda i: (i + off, 0))],
        out_shape=[jax.ShapeDtypeStruct((n, D_MODEL), BF16),
                   jax.ShapeDtypeStruct((kv_total, kw), BF16),
                   jax.ShapeDtypeStruct((kv_total, vw), BF16)],
        input_output_aliases=aliases,
        compiler_params=_cparams(1),
        name="qkv_proj",
    )(*args)


DIFF_TQ = 256
DIFF_TK = 1280
DIFF_VT_ROWS = LANES + 16


def _diff_attn_kernel(tq, tk, n_tiles, n_chunks, lam_init, q_ref, lamv_ref, sub_ref, k_ref, v_ref,
                      o_ref, qq_ref, vt_ref, s_ref, m_ref, acc_ref):
    n_keys = n_chunks * tk
    lane = lax.broadcasted_iota(jnp.int32, (tq, LANES), 1)
    lv = lamv_ref[...]
    lam = (jnp.exp(jnp.sum(lv[0:1] * lv[1:2], axis=-1, keepdims=True))
           - jnp.exp(jnp.sum(lv[2:3] * lv[3:4], axis=-1, keepdims=True)) + lam_init)
    out_gain = sub_ref[...] * (1.0 - lam_init)

    for c in range(n_chunks):
        ks = slice(c * tk, (c + 1) * tk)
        vt_ref[0:LANES, ks] = v_ref[ks, :].astype(F32).T.astype(BF16)
    vt_ref[LANES:, :] = jnp.ones((DIFF_VT_ROWS - LANES, n_keys), BF16)

    def rows_of(t):
        return pl.ds(t * tq, tq) if isinstance(t, int) else pl.ds(pl.multiple_of(t * tq, tq), tq)

    def stack_queries(t, slot):
        q = q_ref[rows_of(t), :]
        zero = jnp.zeros_like(q)
        qq_ref[slot, 0:tq, :] = jnp.where(lane < HEAD_DIM, q, zero)
        qq_ref[slot, tq:, :] = jnp.where(lane >= HEAD_DIM, q, zero)

    def scores(slot, c, buf):
        s_ref[buf] = lax.dot_general(k_ref[c * tk:(c + 1) * tk, :], qq_ref[slot],
                                     (((1,), (1,)), ((), ())), preferred_element_type=F32)

    def softmax_pv(c, buf):
        s = s_ref[buf]
        m_prev = m_ref[...]
        m_new = jnp.maximum(m_prev, jnp.max(s, axis=0, keepdims=True))
        alpha = jnp.exp2(m_prev - m_new)
        p = jnp.exp2(s - m_new).astype(BF16)
        pv = jnp.dot(vt_ref[:, c * tk:(c + 1) * tk], p, preferred_element_type=F32)
        acc_ref[...] = acc_ref[...] * alpha + pv
        m_ref[...] = m_new

    def tile(t, t_next, slot, first_buf):
        m_ref[...] = jnp.full(m_ref.shape, NEG_INF, F32)
        acc_ref[...] = jnp.zeros(acc_ref.shape, F32)
        for c in range(n_chunks):
            buf = (first_buf + c) % 2
            if c + 1 < n_chunks:
                scores(slot, c + 1, 1 - buf)
            elif t_next is not None:
                stack_queries(t_next, 1 - slot)
                scores(1 - slot, 0, 1 - buf)
            softmax_pv(c, buf)
        acc = acc_ref[...]
        o_t = acc[0:LANES, :] / acc[LANES:LANES + 1, :]
        d = (o_t[:, :tq] - lam * o_t[:, tq:]).T
        ms = jnp.mean(d * d, axis=-1, keepdims=True)
        o_ref[rows_of(t), :] = (d * lax.rsqrt(ms + NORM_EPS) * out_gain).astype(BF16)

    stack_queries(0, 0)
    scores(0, 0, 0)
    flip = n_chunks % 2
    if n_tiles % 2 == 0 and n_tiles > 2:
        def pair(tp, carry):
            t0 = 2 * tp
            tile(t0, t0 + 1, 0, 0)
            tile(t0 + 1, jnp.minimum(t0 + 2, n_tiles - 1), 1, flip)
            return carry
        lax.fori_loop(0, n_tiles // 2, pair, 0)
    else:
        for t in range(n_tiles):
            tile(t, t + 1 if t + 1 < n_tiles else None, t % 2, (t * flip) % 2)


def _diff_attn(q, k, v, key_start, n_keys, lamv, subln, lam_init):
    nq = q.shape[0]
    tq = min(DIFF_TQ, nq)
    tk = min(DIFF_TK, n_keys)
    n_chunks = n_keys // tk
    assert nq % tq == 0 and n_keys % tk == 0 and key_start % n_keys == 0
    kb = key_start // n_keys
    return pl.pallas_call(
        functools.partial(_diff_attn_kernel, tq, tk, nq // tq, n_chunks, lam_init),
        grid=(DIFF_HEADS,),
        in_specs=[pl.BlockSpec((nq, LANES), lambda h: (0, h)),
                  _const_spec((8, LANES)),
                  _const_spec((1, LANES)),
                  pl.BlockSpec((n_keys, LANES), lambda h: (kb, h)),
                  pl.BlockSpec((n_keys, LANES), lambda h: (kb, h))],
        out_specs=pl.BlockSpec((nq, LANES), lambda h: (0, h)),
        out_shape=jax.ShapeDtypeStruct((nq, D_MODEL), BF16),
        scratch_shapes=[pltpu.VMEM((2, 2 * tq, LANES), BF16),
                        pltpu.VMEM((DIFF_VT_ROWS, n_keys), BF16),
                        pltpu.VMEM((2, tk, 2 * tq), F32),
                        pltpu.VMEM((1, 2 * tq), F32),
                        pltpu.VMEM((DIFF_VT_ROWS, 2 * tq), F32)],
        compiler_params=_cparams(1),
        name="diff_attn",
    )(q, lamv, subln.reshape(1, LANES), k, v)


def _win_attn_kernel(tq, n_lat, with_window, *refs):
    if with_window:
        (q_ref, kp_ref, kc_ref, kn_ref, vp_ref, vc_ref, vn_ref,
         kx_ref, vx_ref, sink_ref, o_ref, s_ref) = refs
    else:
        q_ref, kx_ref, vx_ref, sink_ref, o_ref, s_ref = refs
    i = pl.program_id(0)
    n_ctx = kx_ref.shape[0]
    sub = WINDOW if with_window else tq
    n_win = 3 * WINDOW if with_window else 0
    n_keys = n_win + n_ctx
    low = lax.broadcasted_iota(jnp.int32, (sub, LANES), 1) < HEAD_DIM
    ones = jnp.ones((n_keys, LANES), BF16)
    nt = (((1,), (1,)), ((), ()))
    units = [(sb, kvh) for sb in range(tq // sub) for kvh in range(WIN_KV_HEADS)]

    def keys_or_values(u, p_ref, c_ref, n_ref, x_ref):
        sb, kvh = units[u]
        ks = slice(kvh * LANES, (kvh + 1) * LANES)
        if not with_window:
            return x_ref[:, ks]
        span = jnp.concatenate([p_ref[:, ks], c_ref[:, ks], n_ref[:, ks]], axis=0)
        return jnp.concatenate([span[sb * WINDOW:sb * WINDOW + n_win], x_ref[:, ks]], axis=0)

    def scores(u, buf):
        sb, kvh = units[u]
        r0 = sb * sub
        qs = []
        for pr in range(WIN_GROUP // 2):
            blk = kvh * (WIN_GROUP // 2) + pr
            qp = q_ref[r0:r0 + sub, blk * LANES:(blk + 1) * LANES]
            zero = jnp.zeros_like(qp)
            qs += [jnp.where(low, qp, zero), jnp.where(low, zero, qp)]
        kk = keys_or_values(u, kp_ref, kc_ref, kn_ref, kx_ref) if with_window else \
            keys_or_values(u, None, None, None, kx_ref)
        s_ref[buf] = lax.dot_general(jnp.concatenate(qs, axis=0), kk, nt,
                                     preferred_element_type=F32)

    def softmax_pv(u, buf):
        sb, kvh = units[u]
        r0 = sb * sub
        s = s_ref[buf]
        if with_window:
            qpos = i * tq + r0 + lax.broadcasted_iota(jnp.int32, (sub, n_win), 0)
            kpos = i * tq + r0 - WINDOW + lax.broadcasted_iota(jnp.int32, (sub, n_win), 1)
            valid = (jnp.abs(kpos - qpos) <= WINDOW) & (kpos >= 0) & (kpos < n_lat)
            bias = jnp.where(valid, 0.0, NEG_INF)
            bias = jnp.concatenate([bias] * WIN_GROUP, axis=0)
            s = jnp.concatenate([s[:, :n_win] + bias, s[:, n_win:]], axis=1)
        heads = [kvh * WIN_GROUP + g for g in range(WIN_GROUP)]
        sink = jnp.concatenate(
            [jnp.broadcast_to(sink_ref[hd:hd + 1, :], (sub, LANES)) for hd in heads], axis=0)
        m = jnp.maximum(jnp.max(s, axis=-1, keepdims=True), sink)
        p = jnp.exp(s - jnp.tile(m, (1, n_keys // LANES)))
        vv = keys_or_values(u, vp_ref, vc_ref, vn_ref, vx_ref) if with_window else \
            keys_or_values(u, None, None, None, vx_ref)
        pv = jnp.dot(p.astype(BF16), jnp.concatenate([vv, ones], axis=1),
                     preferred_element_type=F32)
        den = pv[:, LANES:LANES + 1] + jnp.exp(sink[:, 0:1] - m[:, 0:1])
        o = pv[:, :LANES] / den
        for pr in range(WIN_GROUP // 2):
            blk = kvh * (WIN_GROUP // 2) + pr
            lo = o[(2 * pr) * sub:(2 * pr + 1) * sub]
            hi = o[(2 * pr + 1) * sub:(2 * pr + 2) * sub]
            o_ref[r0:r0 + sub, blk * LANES:(blk + 1) * LANES] = jnp.where(low, lo, hi).astype(BF16)

    scores(0, 0)
    for u in range(len(units)):
        if u + 1 < len(units):
            scores(u + 1, (u + 1) % 2)
        softmax_pv(u, u % 2)


def _win_attn(q, k, v, kx, vx, sink_rows, with_window):
    nq = q.shape[0]
    tq = min(256, nq)
    n_ctx = kx.shape[0]
    kvw = WIN_KV_HEADS * LANES
    in_specs = [pl.BlockSpec((tq, D_MODEL), lambda i: (i, 0))]
    args = [q]
    if with_window:
        r = tq // WINDOW
        last = nq // WINDOW - 1
        prev_spec = pl.BlockSpec((WINDOW, kvw), lambda i: (jnp.maximum(i * r - 1, 0), 0))
        cur_spec = pl.BlockSpec((tq, kvw), lambda i: (i, 0))
        next_spec = pl.BlockSpec((WINDOW, kvw), lambda i: (jnp.minimum((i + 1) * r, last), 0))
        in_specs += [prev_spec, cur_spec, next_spec] * 2
        args += [k, k, k, v, v, v]
    in_specs += [_const_spec((n_ctx, kvw)), _const_spec((n_ctx, kvw)),
                 _const_spec((WIN_Q_HEADS, LANES))]
    args += [kx, vx, sink_rows]
    return pl.pallas_call(
        functools.partial(_win_attn_kernel, tq, nq, with_window),
        grid=(nq // tq,),
        in_specs=in_specs,
        out_specs=pl.BlockSpec((tq, D_MODEL), lambda i: (i, 0)),
        out_shape=jax.ShapeDtypeStruct((nq, D_MODEL), BF16),
        scratch_shapes=[pltpu.VMEM((2, WIN_GROUP * (WINDOW if with_window else tq),
                                    (3 * WINDOW if with_window else 0) + n_ctx), F32)],
        compiler_params=_cparams(1),
        name="win_attn",
    )(*args)


_FFN_SPLIT = -(-(D_FF // MXU_TILE) // 2) * MXU_TILE
FFN_CHUNKS = ((0, _FFN_SPLIT), (_FFN_SPLIT, D_FF))
FFN_SUB_ROWS = 256


def _ffn_kernel(row, fused, *refs):
    if fused:
        x_ref, wo_ref, h_ref, mod_ref, n2_ref, wgu_ref, wd_ref, o_ref = refs
    else:
        h_ref, mod_ref, n2_ref, wgu_ref, wd_ref, o_ref = refs
    t = h_ref.shape[0]
    sub = min(FFN_SUB_ROWS, t)
    for sb in range(t // sub):
        rs = slice(sb * sub, (sb + 1) * sub)
        h = h_ref[rs, :]
        if fused:
            h = h + _mod(mod_ref, row, G1) * jnp.dot(x_ref[rs, :], wo_ref[...],
                                                     preferred_element_type=F32)
        a = _mod_norm(h, n2_ref[...], _mod(mod_ref, row, SC2), _mod(mod_ref, row, SH2)).astype(BF16)
        acc = None
        for lo, hi in FFN_CHUNKS:
            g = jnp.dot(a, wgu_ref[:, lo:hi], preferred_element_type=F32)
            u = jnp.dot(a, wgu_ref[:, D_FF + lo:D_FF + hi], preferred_element_type=F32)
            act = (_silu(g) * u).astype(BF16)
            part = jnp.dot(act, wd_ref[lo:hi, :], preferred_element_type=F32)
            acc = part if acc is None else acc + part
        o_ref[rs, :] = h + _mod(mod_ref, row, G2) * acc


def _ffn(h, mods, layer, row, n2, wgu, wd, attn=None):
    n = h.shape[0]
    t = _row_tile(n)
    row_spec = pl.BlockSpec((t, D_MODEL), lambda i: (i, 0))
    in_specs = [row_spec,
                pl.BlockSpec((None, 8, 6 * D_MODEL), lambda i: (layer, 0, 0)),
                _const_spec((1, D_MODEL)),
                _layer_spec((D_MODEL, 2 * D_FF), layer),
                _layer_spec((D_FF, D_MODEL), layer)]
    args = [h, mods, n2.reshape(1, D_MODEL), wgu, wd]
    if attn is not None:
        in_specs = [row_spec, _const_spec((D_MODEL, D_MODEL))] + in_specs
        args = list(attn) + args
    return pl.pallas_call(
        functools.partial(_ffn_kernel, row, attn is not None),
        grid=(n // t,),
        in_specs=in_specs,
        out_specs=row_spec,
        out_shape=jax.ShapeDtypeStruct((n, D_MODEL), F32),
        compiler_params=_cparams(1),
        name="ffn",
    )(*args)


def _in_proj_kernel(row, mode, *refs):
    if mode == "short":
        h_ref, mod_ref, n1_ref, w_ref, b_out, cu_out = refs
    else:
        h_ref, mod_ref, n1_ref, w_ref, bias_ref, glu_out = refs
    t = h_ref.shape[0]
    sub = min(QKV_SUB_ROWS, t)
    for sb in range(t // sub):
        rs = slice(sb * sub, (sb + 1) * sub)
        a = _mod_norm(h_ref[rs, :], n1_ref[...], _mod(mod_ref, row, SC1), _mod(mod_ref, row, SH1))
        y = jnp.dot(a.astype(BF16), w_ref[...], preferred_element_type=F32)
        if mode == "short":
            b_out[rs, :] = y[:, :D_MODEL]
            cu_out[rs, :] = y[:, D_MODEL:2 * D_MODEL] * y[:, 2 * D_MODEL:]
        else:
            y = y + bias_ref[...]
            g = y[:, D_MODEL:]
            glu_out[rs, :] = y[:, :D_MODEL] * (1.0 / (1.0 + jnp.exp(-g)))


def _in_proj(h, mods, layer, row, n1, w, mode, bias=None):
    n = h.shape[0]
    t = _row_tile(n)
    wtot = w.shape[1]
    in_specs = [pl.BlockSpec((t, D_MODEL), lambda i: (i, 0)),
                pl.BlockSpec((None, 8, 6 * D_MODEL), lambda i: (layer, 0, 0)),
                _const_spec((1, D_MODEL)),
                _const_spec((D_MODEL, wtot))]
    args = [h, mods, n1.reshape(1, D_MODEL), w]
    row_spec = pl.BlockSpec((t, D_MODEL), lambda i: (i, 0))
    row_shape = jax.ShapeDtypeStruct((n, D_MODEL), F32)
    if mode == "short":
        out_specs, out_shape = [row_spec, row_spec], [row_shape, row_shape]
    else:
        in_specs.append(_const_spec((1, wtot)))
        args.append(bias.reshape(1, wtot))
        out_specs, out_shape = row_spec, row_shape
    return pl.pallas_call(
        functools.partial(_in_proj_kernel, row, mode),
        grid=(n // t,),
        in_specs=in_specs,
        out_specs=out_specs,
        out_shape=out_shape,
        compiler_params=_cparams(1),
        name="in_proj_" + mode,
    )(*args)


CONV_ROW_BLOCK = 64
SUBLANES = 8


def _conv_shifts(taps):
    return [divmod(CONV_HALO - taps // 2 + k, SUBLANES) for k in range(taps)]


def _conv_rems(taps):
    return sorted({r for _, r in _conv_shifts(taps) if r})


def _conv_span(taps, t):
    return t + SUBLANES * max(a for a, _ in _conv_shifts(taps))


def _dwconv_tile(xs_ref, sh_ref, u_ref, prev_ref, cur_ref, next_ref, w_ref, taps, t):
    i = pl.program_id(0)
    last = pl.num_programs(0) - 1
    ncb = D_MODEL // LANES
    for cb in range(ncb):
        cs = slice(cb * LANES, (cb + 1) * LANES)
        xs_ref[cb, 0:CONV_HALO, :] = jnp.where(i > 0, prev_ref[:, cs], 0.0)
        xs_ref[cb, CONV_HALO:CONV_HALO + t, :] = cur_ref[:, cs]
        xs_ref[cb, CONV_HALO + t:, :] = jnp.where(i < last, next_ref[:, cs], 0.0)
    shifts = _conv_shifts(taps)
    rems = _conv_rems(taps)
    span = _conv_span(taps, t)
    for j, r in enumerate(rems):
        sh_ref[j] = xs_ref[:, r:r + span, :]

    def block(idx, carry):
        rb, cb = idx // ncb, idx % ncb
        row0 = pl.multiple_of(rb * CONV_ROW_BLOCK, CONV_ROW_BLOCK)
        accs = [None, None]
        for k, (a, r) in enumerate(shifts):
            rows = pl.ds(row0 + a * SUBLANES, CONV_ROW_BLOCK)
            x = xs_ref[cb, rows, :] if r == 0 else sh_ref[rems.index(r), cb, rows, :]
            term = x * w_ref[cb, k:k + 1, :]
            accs[k % 2] = term if accs[k % 2] is None else accs[k % 2] + term
        u_ref[cb, pl.ds(row0, CONV_ROW_BLOCK), :] = accs[0] + accs[1]
        return carry

    lax.fori_loop(0, (t // CONV_ROW_BLOCK) * ncb, block, 0)
    return jnp.concatenate([u_ref[cb] for cb in range(ncb)], axis=1)


def _conv_out_kernel(row, mode, taps, t, *refs):
    if mode == "short":
        (prev_ref, cur_ref, next_ref, cw_ref, b_ref, h_ref, mod_ref, w_ref, o_ref,
         xs_ref, sh_ref, u_ref) = refs
    else:
        (prev_ref, cur_ref, next_ref, cw_ref, dwb_ref, lng_ref, lnb_ref, h_ref, mod_ref,
         w_ref, pb_ref, o_ref, xs_ref, sh_ref, u_ref) = refs
    u = _dwconv_tile(xs_ref, sh_ref, u_ref, prev_ref, cur_ref, next_ref, cw_ref, taps, t)
    if mode == "short":
        y = jnp.dot((b_ref[...] * u).astype(BF16), w_ref[...], preferred_element_type=F32)
    else:
        u = u + dwb_ref[...]
        mu = jnp.mean(u, axis=-1, keepdims=True)
        uc = u - mu
        var = jnp.mean(uc * uc, axis=-1, keepdims=True)
        z = _silu(uc * lax.rsqrt(var + NORM_EPS) * lng_ref[...] + lnb_ref[...])
        y = jnp.dot(z.astype(BF16), w_ref[...], preferred_element_type=F32) + pb_ref[...]
    o_ref[...] = h_ref[...] + _mod(mod_ref, row, G1) * y


def _conv_out(x, h, mods, layer, row, conv_w, w, mode, extra):
    n = h.shape[0]
    t = _row_tile(n)
    taps = conv_w.shape[0]
    r = t // CONV_HALO
    last = n // CONV_HALO - 1
    taps_pad = -(-taps // 8) * 8
    ncb = D_MODEL // LANES
    cw = jnp.zeros((taps_pad, D_MODEL), F32).at[:taps].set(conv_w)
    cw = cw.reshape(taps_pad, ncb, LANES).transpose(1, 0, 2)
    row_spec = pl.BlockSpec((t, D_MODEL), lambda i: (i, 0))
    vec_spec = _const_spec((1, D_MODEL))
    in_specs = [pl.BlockSpec((CONV_HALO, D_MODEL), lambda i: (jnp.maximum(i * r - 1, 0), 0)),
                row_spec,
                pl.BlockSpec((CONV_HALO, D_MODEL), lambda i: (jnp.minimum((i + 1) * r, last), 0)),
                _const_spec((ncb, taps_pad, LANES))]
    args = [x, x, x, cw]
    mod_spec = pl.BlockSpec((None, 8, 6 * D_MODEL), lambda i: (layer, 0, 0))
    if mode == "short":
        (b_gate,) = extra
        in_specs += [row_spec, row_spec, mod_spec, _const_spec((D_MODEL, D_MODEL))]
        args += [b_gate, h, mods, w]
    else:
        dw_b, ln_g, ln_b, pw_b = extra
        in_specs += [vec_spec, vec_spec, vec_spec, row_spec, mod_spec,
                     _const_spec((D_MODEL, D_MODEL)), vec_spec]
        args += [dw_b.reshape(1, D_MODEL), ln_g.reshape(1, D_MODEL), ln_b.reshape(1, D_MODEL),
                 h, mods, w, pw_b.reshape(1, D_MODEL)]
    return pl.pallas_call(
        functools.partial(_conv_out_kernel, row, mode, taps, t),
        grid=(n // t,),
        in_specs=in_specs,
        out_specs=row_spec,
        out_shape=jax.ShapeDtypeStruct((n, D_MODEL), F32),
        scratch_shapes=[pltpu.VMEM((ncb, t + 2 * CONV_HALO, LANES), F32),
                        pltpu.VMEM((len(_conv_rems(taps)), ncb, _conv_span(taps, t), LANES), F32),
                        pltpu.VMEM((ncb, t, LANES), F32)],
        compiler_params=_cparams(1),
        name="conv_out_" + mode,
    )(*args)


def _diff_qkv_weight(w_qkv):
    w = w_qkv.astype(BF16)

    def head_major(cols):
        d_in = cols.shape[0]
        return cols.reshape(d_in, 2, DIFF_HEADS, HEAD_DIM).transpose(0, 2, 1, 3).reshape(d_in, D_MODEL)

    return jnp.concatenate([head_major(w[:, :D_MODEL]), head_major(w[:, D_MODEL:2 * D_MODEL]),
                            w[:, 2 * D_MODEL:]], axis=1)


def _win_qkv_weight(w_qkv):
    qw = WIN_Q_HEADS * HEAD_DIM
    kvw = WIN_KV_HEADS * HEAD_DIM
    w = w_qkv.astype(BF16)

    def twice(cols):
        d_in = cols.shape[0]
        heads = cols.reshape(d_in, WIN_KV_HEADS, 1, HEAD_DIM)
        return jnp.broadcast_to(heads, (d_in, WIN_KV_HEADS, 2, HEAD_DIM)).reshape(d_in, 2 * kvw)

    return jnp.concatenate([w[:, :qw], twice(w[:, qw:qw + kvw]), twice(w[:, qw + kvw:])], axis=1)


def kernel(x, c, ctx, c_ctx, ada_w, ada_b, norm1, norm2, ffn_w_gate_up, ffn_w_down, diff_w_qkv, diff_w_o, diff_q_norm, diff_k_norm, diff_lam_q1, diff_lam_k1, diff_lam_q2, diff_lam_k2, diff_subln, sc_w_in, sc_conv_w, sc_w_out, win_w_qkv, win_w_o, win_q_norm, win_k_norm, win_sink, cf_w_pw1, cf_b_pw1, cf_dw_w, cf_dw_b, cf_ln_g, cf_ln_b, cf_w_pw2, cf_b_pw2):
    n = x.shape[1]
    h, hc = x.reshape(n, D_MODEL), ctx.reshape(ctx.shape[1], D_MODEL)
    mods = _ada_mod(c, c_ctx, ada_w, ada_b)
    wgu, wd = ffn_w_gate_up.astype(BF16), ffn_w_down.astype(BF16)
    tab = _rope_tables(n, _row_tile(n))
    for i in range(DEPTH):
        kind, j = i % N_MIXERS, i // N_MIXERS
        with_ctx = i < DEPTH - 1
        attn_l = attn_c = None
        if kind == 0:
            lam_init = 0.8 - 0.6 * math.exp(-0.3 * i)
            w = _diff_qkv_weight(diff_w_qkv[j])
            w_o = diff_w_o[j].astype(BF16)
            lamv = jnp.zeros((8, LANES), F32).at[0:4, 0:HEAD_DIM].set(
                jnp.stack([diff_lam_q1[j], diff_lam_k1[j], diff_lam_q2[j], diff_lam_k2[j]]))
            q_scale = HEAD_DIM ** -0.5 * LOG2E
            proj = functools.partial(_qkv_proj, mods=mods, layer=i, n1=norm1[i], w=w,
                                     q_gain=diff_q_norm[j], k_gain=diff_k_norm[j],
                                     kw=D_MODEL, vw=D_MODEL, q_scale=q_scale)
            n_ctx = hc.shape[0]
            kv0 = (jnp.zeros((n + n_ctx, D_MODEL), BF16), jnp.zeros((n + n_ctx, D_MODEL), BF16))
            q_l, k_all, v_all = proj(h, row=0, tab=tab, kv_into=kv0)
            q_c, k_all, v_all = proj(hc, row=1, tab=None, kv_into=(k_all, v_all), kv_row_offset=n)
            o_l = _diff_attn(q_l, k_all, v_all, 0, n + n_ctx, lamv, diff_subln[j], lam_init)
            attn_l = (o_l, w_o)
            if with_ctx:
                attn_c = (_diff_attn(q_c, k_all, v_all, n, n_ctx, lamv, diff_subln[j], lam_init), w_o)
        elif kind == 1:
            w_in = sc_w_in[j].astype(BF16)
            w_out = sc_w_out[j].astype(BF16)
            b_l, cu_l = _in_proj(h, mods, i, 0, norm1[i], w_in, "short")
            h = _conv_out(cu_l, h, mods, i, 0, sc_conv_w[j], w_out, "short", (b_l,))
            if with_ctx:
                b_c, cu_c = _in_proj(hc, mods, i, 1, norm1[i], w_in, "short")
                hc = _conv_out(cu_c, hc, mods, i, 1, sc_conv_w[j], w_out, "short", (b_c,))
        elif kind == 2:
            w = _win_qkv_weight(win_w_qkv[j])
            w_o = win_w_o[j].astype(BF16)
            kvw = WIN_KV_HEADS * LANES
            sink_rows = jnp.broadcast_to(win_sink[j][:, None], (WIN_Q_HEADS, LANES))
            proj = functools.partial(_qkv_proj, mods=mods, layer=i, n1=norm1[i], w=w,
                                     q_gain=win_q_norm[j], k_gain=win_k_norm[j],
                                     kw=kvw, vw=kvw, q_scale=HEAD_DIM ** -0.5)
            q_l, k_l, v_l = proj(h, row=0, tab=tab)
            q_c, k_c, v_c = proj(hc, row=1, tab=None)
            o_l = _win_attn(q_l, k_l, v_l, k_c, v_c, sink_rows, True)
            attn_l = (o_l, w_o)
            if with_ctx:
                attn_c = (_win_attn(q_c, None, None, k_c, v_c, sink_rows, False), w_o)
        else:
            w1 = cf_w_pw1[j].astype(BF16)
            w2 = cf_w_pw2[j].astype(BF16)
            extra = (cf_dw_b[j], cf_ln_g[j], cf_ln_b[j], cf_b_pw2[j])
            glu_l = _in_proj(h, mods, i, 0, norm1[i], w1, "glu", cf_b_pw1[j])
            h = _conv_out(glu_l, h, mods, i, 0, cf_dw_w[j], w2, "conf", extra)
            if with_ctx:
                glu_c = _in_proj(hc, mods, i, 1, norm1[i], w1, "glu", cf_b_pw1[j])
                hc = _conv_out(glu_c, hc, mods, i, 1, cf_dw_w[j], w2, "conf", extra)
        h = _ffn(h, mods, i, 0, norm2[i], wgu, wd, attn_l)
        if with_ctx:
            hc = _ffn(hc, mods, i, 1, norm2[i], wgu, wd, attn_c)
    return h.reshape(1, n, D_MODEL)
```

```python
import functools
import math

import jax
import jax.numpy as jnp
import numpy as np
from jax import lax
from jax.experimental import pallas as pl
from jax.experimental.pallas import tpu as pltpu

F32 = jnp.float32
BF16 = jnp.bfloat16

D_MODEL = 1024
DEPTH = 4
N_MIXERS = 4
GRID_W = 64
NORM_EPS = 1e-6
ROPE_THETA = 10000.0
HEAD_DIM = 64
NEG_INF = -1e30
DIFF_HEADS = D_MODEL // (2 * HEAD_DIM)
SHORT_CONV_W = 3
WIN_Q_HEADS = D_MODEL // HEAD_DIM
WIN_KV_HEADS = 4
WIN_GROUP = WIN_Q_HEADS // WIN_KV_HEADS
WINDOW = 128
CONF_CONV_W = 31
D_FF = -(-(8 * D_MODEL) // (3 * 256)) * 256

LANES = 128
MXU_TILE = 256
CONV_HALO = 16
VMEM_LIMIT_BYTES = 56 * 1024 * 1024

LOG2E = math.log2(math.e)
SH1, SC1, G1, SH2, SC2, G2 = range(6)


def _cparams(n_axes):
    return pltpu.CompilerParams(dimension_semantics=("parallel",) * n_axes,
                                vmem_limit_bytes=VMEM_LIMIT_BYTES)


def _const_spec(shape):
    nd = len(shape)
    return pl.BlockSpec(shape, lambda *_: (0,) * nd, pipeline_mode=pl.Buffered(1))


def _layer_spec(shape, layer):
    nd = len(shape)
    return pl.BlockSpec((None,) + tuple(shape), lambda *_: (layer,) + (0,) * nd,
                        pipeline_mode=pl.Buffered(1))


def _mod(mod_ref, row, idx):
    return mod_ref[row:row + 1, idx * D_MODEL:(idx + 1) * D_MODEL]


def _mod_norm(h, gain, scale, shift):
    ms = jnp.mean(h * h, axis=-1, keepdims=True)
    return (h * lax.rsqrt(ms + NORM_EPS)) * (gain * (1.0 + scale)) + shift


def _silu(x):
    return x * (1.0 / (1.0 + jnp.exp(-x)))


def _row_tile(n):
    return 512 if n % 512 == 0 else n


def _ada_kernel(act_ref, w_ref, b_ref, o_ref):
    a = _silu(act_ref[...])
    o_ref[...] = jnp.dot(a.astype(BF16), w_ref[...].astype(BF16),
                         preferred_element_type=F32) + b_ref[...]


def _ada_mod(c, c_ctx, ada_w, ada_b):
    act = jnp.zeros((8, D_MODEL), F32).at[0].set(c[0]).at[1].set(c_ctx)
    cols = 1536
    nc = 6 * D_MODEL // cols
    return pl.pallas_call(
        _ada_kernel,
        grid=(DEPTH, nc),
        in_specs=[pl.BlockSpec((8, D_MODEL), lambda i, j: (0, 0)),
                  pl.BlockSpec((None, D_MODEL, cols), lambda i, j: (i, 0, j)),
                  pl.BlockSpec((None, 1, cols), lambda i, j: (i, 0, j))],
        out_specs=pl.BlockSpec((None, 8, cols), lambda i, j: (i, 0, j)),
        out_shape=jax.ShapeDtypeStruct((DEPTH, 8, 6 * D_MODEL), F32),
        compiler_params=_cparams(2),
        name="ada_mod",
    )(act, ada_w, ada_b.reshape(DEPTH, 1, 6 * D_MODEL))


def _group_mean_matrix(width):
    g = np.arange(width) // HEAD_DIM
    return jnp.asarray((g[:, None] == g[None, :]).astype(np.float32) / HEAD_DIM, BF16)


def _rope_tables(n, t):
    n_freq = HEAD_DIM // 4
    inv_freq = ROPE_THETA ** (-jnp.arange(n_freq, dtype=F32) / n_freq)
    reps = LANES // HEAD_DIM

    def tables(count, is_row):
        ang = jnp.arange(count, dtype=F32)[:, None] * inv_freq[None, :]
        cos, sin, zero = jnp.cos(ang), jnp.sin(ang), jnp.zeros_like(ang)
        halves = [[cos, cos], [-sin, zero], [zero, sin]]
        out = []
        for first, second in halves:
            pair = [first, second, zero, zero] if is_row else [zero, zero, first, second]
            out.append(jnp.tile(jnp.concatenate(pair, axis=-1), (1, reps)))
        return jnp.concatenate(out, axis=-1)

    return tables(n // GRID_W, True), jnp.tile(tables(GRID_W, False), (t // GRID_W, 1))


def _rope(x, tab):
    w = x.shape[-1]
    reps = w // LANES
    cos = jnp.tile(tab[:, 0:LANES], (1, reps))
    sin_up = jnp.tile(tab[:, LANES:2 * LANES], (1, reps))
    sin_dn = jnp.tile(tab[:, 2 * LANES:3 * LANES], (1, reps))
    quarter = HEAD_DIM // 4
    return (x * cos + pltpu.roll(x, w - quarter, 1) * sin_up
            + pltpu.roll(x, quarter, 1) * sin_dn)


QKV_SUB_ROWS = 256


def _qkv_kernel(row, rotary, kw, q_scale, *refs):
    if rotary:
        (h_ref, mod_ref, n1_ref, w_ref, gm_ref, qg_ref, kg_ref, rt_ref, ct_ref,
         q_ref, k_ref, v_ref) = refs
    else:
        h_ref, mod_ref, n1_ref, w_ref, gm_ref, qg_ref, kg_ref, q_ref, k_ref, v_ref = refs
    gm = gm_ref[...]

    def head_norm(x, gain, width):
        x2 = (x * x).astype(BF16)
        ms = jnp.concatenate(
            [jnp.dot(x2[:, c:c + MXU_TILE], gm, preferred_element_type=F32)
             for c in range(0, width, MXU_TILE)], axis=1)
        return x * lax.rsqrt(ms + NORM_EPS) * gain

    t = h_ref.shape[0]
    sub = min(QKV_SUB_ROWS, t)
    for sb in range(t // sub):
        rs = slice(sb * sub, (sb + 1) * sub)
        a = _mod_norm(h_ref[rs, :], n1_ref[...], _mod(mod_ref, row, SC1), _mod(mod_ref, row, SH1))
        y = jnp.dot(a.astype(BF16), w_ref[...], preferred_element_type=F32)
        q = head_norm(y[:, :D_MODEL], qg_ref[...], D_MODEL)
        k = head_norm(y[:, D_MODEL:D_MODEL + kw], kg_ref[...], kw)
        if rotary:
            g0 = sb * sub // GRID_W
            row_part = jnp.concatenate(
                [jnp.broadcast_to(rt_ref[g0 + r:g0 + r + 1, :], (GRID_W, 3 * LANES))
                 for r in range(sub // GRID_W)], axis=0)
            tab = row_part + ct_ref[rs, :]
            q, k = _rope(q, tab), _rope(k, tab)
        q_ref[rs, :] = (q * q_scale).astype(BF16)
        k_ref[rs, :] = k.astype(BF16)
        v_ref[rs, :] = y[:, D_MODEL + kw:].astype(BF16)


def _qkv_proj(h, mods, layer, row, n1, w, q_gain, k_gain, kw, vw, q_scale, tab):
    n = h.shape[0]
    t = _row_tile(n)
    rotary = tab is not None
    wtot = D_MODEL + kw + vw
    in_specs = [pl.BlockSpec((t, D_MODEL), lambda i: (i, 0)),
                pl.BlockSpec((None, 8, 6 * D_MODEL), lambda i: (layer, 0, 0)),
                _const_spec((1, D_MODEL)),
                _const_spec((D_MODEL, wtot)),
                _const_spec((MXU_TILE, MXU_TILE)),
                _const_spec((1, D_MODEL)),
                _const_spec((1, kw))]
    args = [h, mods, n1.reshape(1, D_MODEL), w, _group_mean_matrix(MXU_TILE),
            jnp.tile(q_gain, D_MODEL // HEAD_DIM).reshape(1, D_MODEL),
            jnp.tile(k_gain, kw // HEAD_DIM).reshape(1, kw)]
    if rotary:
        row_tab, col_tab = tab
        in_specs += [pl.BlockSpec((t // GRID_W, 3 * LANES), lambda i: (i, 0)),
                     _const_spec((t, 3 * LANES))]
        args += [row_tab, col_tab]
    return pl.pallas_call(
        functools.partial(_qkv_kernel, row, rotary, kw, q_scale),
        grid=(n // t,),
        in_specs=in_specs,
        out_specs=[pl.BlockSpec((t, D_MODEL), lambda i: (i, 0)),
                   pl.BlockSpec((t, kw), lambda i: (i, 0)),
                   pl.BlockSpec((t, vw), lambda i: (i, 0))],
        out_shape=[jax.ShapeDtypeStruct((n, D_MODEL), BF16),
                   jax.ShapeDtypeStruct((n, kw), BF16),
                   jax.ShapeDtypeStruct((n, vw), BF16)],
        compiler_params=_cparams(1),
        name="qkv_proj",
    )(*args)


DIFF_TQ = 256
DIFF_TK = 1280
DIFF_VT_ROWS = LANES + 16


def _diff_attn_kernel(tq, tk, n_tiles, n_chunks, src_rows, lam_init, q_ref, lamv_ref, sub_ref, *refs):
    n_src = len(src_rows)
    kv_refs = refs[:2 * n_src]
    o_ref, qq_ref, k_ref, vt_ref, s_ref, m_ref, acc_ref = refs[2 * n_src:]
    n_keys = n_chunks * tk
    lane = lax.broadcasted_iota(jnp.int32, (tq, LANES), 1)
    lv = lamv_ref[...]
    lam = (jnp.exp(jnp.sum(lv[0:1] * lv[1:2], axis=-1, keepdims=True))
           - jnp.exp(jnp.sum(lv[2:3] * lv[3:4], axis=-1, keepdims=True)) + lam_init)
    out_gain = sub_ref[...] * (1.0 - lam_init)

    off = 0
    for i_src, rows in enumerate(src_rows):
        ks_ref, vs_ref = kv_refs[2 * i_src], kv_refs[2 * i_src + 1]
        for r0 in range(0, rows, tk):
            r1 = min(r0 + tk, rows)
            k_ref[off + r0:off + r1, :] = ks_ref[r0:r1, :]
            vt_ref[0:LANES, off + r0:off + r1] = vs_ref[r0:r1, :].astype(F32).T.astype(BF16)
        off += rows
    vt_ref[LANES:, :] = jnp.ones((DIFF_VT_ROWS - LANES, n_keys), BF16)

    def rows_of(t):
        return pl.ds(t * tq, tq) if isinstance(t, int) else pl.ds(pl.multiple_of(t * tq, tq), tq)

    def stack_queries(t, slot):
        q = q_ref[rows_of(t), :]
        zero = jnp.zeros_like(q)
        qq_ref[slot, 0:tq, :] = jnp.where(lane < HEAD_DIM, q, zero)
        qq_ref[slot, tq:, :] = jnp.where(lane >= HEAD_DIM, q, zero)

    def scores(slot, c, buf):
        s_ref[buf] = lax.dot_general(k_ref[c * tk:(c + 1) * tk, :], qq_ref[slot],
                                     (((1,), (1,)), ((), ())), preferred_element_type=F32)

    def softmax_pv(c, buf):
        s = s_ref[buf]
        m_prev = m_ref[...]
        m_new = jnp.maximum(m_prev, jnp.max(s, axis=0, keepdims=True))
        alpha = jnp.exp2(m_prev - m_new)
        p = jnp.exp2(s - m_new).astype(BF16)
        pv = jnp.dot(vt_ref[:, c * tk:(c + 1) * tk], p, preferred_element_type=F32)
        acc_ref[...] = acc_ref[...] * alpha + pv
        m_ref[...] = m_new

    def tile(t, t_next, slot, first_buf):
        m_ref[...] = jnp.full(m_ref.shape, NEG_INF, F32)
        acc_ref[...] = jnp.zeros(acc_ref.shape, F32)
        for c in range(n_chunks):
            buf = (first_buf + c) % 2
            if c + 1 < n_chunks:
                scores(slot, c + 1, 1 - buf)
            elif t_next is not None:
                stack_queries(t_next, 1 - slot)
                scores(1 - slot, 0, 1 - buf)
            softmax_pv(c, buf)
        acc = acc_ref[...]
        o_t = acc[0:LANES, :] / acc[LANES:LANES + 1, :]
        d = (o_t[:, :tq] - lam * o_t[:, tq:]).T
        ms = jnp.mean(d * d, axis=-1, keepdims=True)
        o_ref[rows_of(t), :] = (d * lax.rsqrt(ms + NORM_EPS) * out_gain).astype(BF16)

    stack_queries(0, 0)
    scores(0, 0, 0)
    flip = n_chunks % 2
    if n_tiles % 2 == 0 and n_tiles > 2:
        def pair(tp, carry):
            t0 = 2 * tp
            tile(t0, t0 + 1, 0, 0)
            tile(t0 + 1, jnp.minimum(t0 + 2, n_tiles - 1), 1, flip)
            return carry
        lax.fori_loop(0, n_tiles // 2, pair, 0)
    else:
        for t in range(n_tiles):
            tile(t, t + 1 if t + 1 < n_tiles else None, t % 2, (t * flip) % 2)


def _diff_attn(q, kv_sources, lamv, subln, lam_init):
    nq = q.shape[0]
    tq = min(DIFF_TQ, nq)
    src_rows = tuple(k.shape[0] for k, _ in kv_sources)
    n_keys = sum(src_rows)
    tk = min(DIFF_TK, n_keys)
    n_chunks = n_keys // tk
    assert nq % tq == 0 and n_keys % tk == 0
    in_specs = [pl.BlockSpec((nq, LANES), lambda h: (0, h)),
                _const_spec((8, LANES)),
                _const_spec((1, LANES))]
    args = [q, lamv, subln.reshape(1, LANES)]
    for k, v in kv_sources:
        in_specs += [pl.BlockSpec((k.shape[0], LANES), lambda h: (0, h))] * 2
        args += [k, v]
    return pl.pallas_call(
        functools.partial(_diff_attn_kernel, tq, tk, nq // tq, n_chunks, src_rows, lam_init),
        grid=(DIFF_HEADS,),
        in_specs=in_specs,
        out_specs=pl.BlockSpec((nq, LANES), lambda h: (0, h)),
        out_shape=jax.ShapeDtypeStruct((nq, D_MODEL), BF16),
        scratch_shapes=[pltpu.VMEM((2, 2 * tq, LANES), BF16),
                        pltpu.VMEM((n_keys, LANES), BF16),
                        pltpu.VMEM((DIFF_VT_ROWS, n_keys), BF16),
                        pltpu.VMEM((2, tk, 2 * tq), F32),
                        pltpu.VMEM((1, 2 * tq), F32),
                        pltpu.VMEM((DIFF_VT_ROWS, 2 * tq), F32)],
        compiler_params=_cparams(1),
        name="diff_attn",
    )(*args)


def _win_attn_kernel(tq, n_lat, with_window, *refs):
    if with_window:
        (q_ref, kp_ref, kc_ref, kn_ref, vp_ref, vc_ref, vn_ref,
         kx_ref, vx_ref, sink_ref, o_ref, s_ref) = refs
    else:
        q_ref, kx_ref, vx_ref, sink_ref, o_ref, s_ref = refs
    i = pl.program_id(0)
    n_ctx = kx_ref.shape[0]
    sub = WINDOW if with_window else tq
    n_win = 3 * WINDOW if with_window else 0
    n_keys = n_win + n_ctx
    low = lax.broadcasted_iota(jnp.int32, (sub, LANES), 1) < HEAD_DIM
    ones = jnp.ones((n_keys, LANES), BF16)
    nt = (((1,), (1,)), ((), ()))
    units = [(sb, kvh) for sb in range(tq // sub) for kvh in range(WIN_KV_HEADS)]

    def keys_or_values(u, p_ref, c_ref, n_ref, x_ref):
        sb, kvh = units[u]
        ks = slice(kvh * LANES, (kvh + 1) * LANES)
        if not with_window:
            return x_ref[:, ks]
        span = jnp.concatenate([p_ref[:, ks], c_ref[:, ks], n_ref[:, ks]], axis=0)
        return jnp.concatenate([span[sb * WINDOW:sb * WINDOW + n_win], x_ref[:, ks]], axis=0)

    def scores(u, buf):
        sb, kvh = units[u]
        r0 = sb * sub
        qs = []
        for pr in range(WIN_GROUP // 2):
            blk = kvh * (WIN_GROUP // 2) + pr
            qp = q_ref[r0:r0 + sub, blk * LANES:(blk + 1) * LANES]
            zero = jnp.zeros_like(qp)
            qs += [jnp.where(low, qp, zero), jnp.where(low, zero, qp)]
        kk = keys_or_values(u, kp_ref, kc_ref, kn_ref, kx_ref) if with_window else \
            keys_or_values(u, None, None, None, kx_ref)
        s_ref[buf] = lax.dot_general(jnp.concatenate(qs, axis=0), kk, nt,
                                     preferred_element_type=F32)

    def softmax_pv(u, buf):
        sb, kvh = units[u]
        r0 = sb * sub
        s = s_ref[buf]
        if with_window:
            qpos = i * tq + r0 + lax.broadcasted_iota(jnp.int32, (sub, n_win), 0)
            kpos = i * tq + r0 - WINDOW + lax.broadcasted_iota(jnp.int32, (sub, n_win), 1)
            valid = (jnp.abs(kpos - qpos) <= WINDOW) & (kpos >= 0) & (kpos < n_lat)
            bias = jnp.where(valid, 0.0, NEG_INF)
            bias = jnp.concatenate([bias] * WIN_GROUP, axis=0)
            s = jnp.concatenate([s[:, :n_win] + bias, s[:, n_win:]], axis=1)
        heads = [kvh * WIN_GROUP + g for g in range(WIN_GROUP)]
        sink = jnp.concatenate(
            [jnp.broadcast_to(sink_ref[hd:hd + 1, :], (sub, LANES)) for hd in heads], axis=0)
        m = jnp.maximum(jnp.max(s, axis=-1, keepdims=True), sink)
        p = jnp.exp(s - jnp.tile(m, (1, n_keys // LANES)))
        vv = keys_or_values(u, vp_ref, vc_ref, vn_ref, vx_ref) if with_window else \
            keys_or_values(u, None, None, None, vx_ref)
        pv = jnp.dot(p.astype(BF16), jnp.concatenate([vv, ones], axis=1),
                     preferred_element_type=F32)
        den = pv[:, LANES:LANES + 1] + jnp.exp(sink[:, 0:1] - m[:, 0:1])
        o = pv[:, :LANES] / den
        for pr in range(WIN_GROUP // 2):
            blk = kvh * (WIN_GROUP // 2) + pr
            lo = o[(2 * pr) * sub:(2 * pr + 1) * sub]
            hi = o[(2 * pr + 1) * sub:(2 * pr + 2) * sub]
            o_ref[r0:r0 + sub, blk * LANES:(blk + 1) * LANES] = jnp.where(low, lo, hi).astype(BF16)

    scores(0, 0)
    for u in range(len(units)):
        if u + 1 < len(units):
            scores(u + 1, (u + 1) % 2)
        softmax_pv(u, u % 2)


def _win_attn(q, k, v, kx, vx, sink_rows, with_window):
    nq = q.shape[0]
    tq = min(256, nq)
    n_ctx = kx.shape[0]
    kvw = WIN_KV_HEADS * LANES
    in_specs = [pl.BlockSpec((tq, D_MODEL), lambda i: (i, 0))]
    args = [q]
    if with_window:
        r = tq // WINDOW
        last = nq // WINDOW - 1
        prev_spec = pl.BlockSpec((WINDOW, kvw), lambda i: (jnp.maximum(i * r - 1, 0), 0))
        cur_spec = pl.BlockSpec((tq, kvw), lambda i: (i, 0))
        next_spec = pl.BlockSpec((WINDOW, kvw), lambda i: (jnp.minimum((i + 1) * r, last), 0))
        in_specs += [prev_spec, cur_spec, next_spec] * 2
        args += [k, k, k, v, v, v]
    in_specs += [_const_spec((n_ctx, kvw)), _const_spec((n_ctx, kvw)),
                 _const_spec((WIN_Q_HEADS, LANES))]
    args += [kx, vx, sink_rows]
    return pl.pallas_call(
        functools.partial(_win_attn_kernel, tq, nq, with_window),
        grid=(nq // tq,),
        in_specs=in_specs,
        out_specs=pl.BlockSpec((tq, D_MODEL), lambda i: (i, 0)),
        out_shape=jax.ShapeDtypeStruct((nq, D_MODEL), BF16),
        scratch_shapes=[pltpu.VMEM((2, WIN_GROUP * (WINDOW if with_window else tq),
                                    (3 * WINDOW if with_window else 0) + n_ctx), F32)],
        compiler_params=_cparams(1),
        name="win_attn",
    )(*args)


_FFN_SPLIT = -(-(D_FF // MXU_TILE) // 2) * MXU_TILE
FFN_CHUNKS = ((0, _FFN_SPLIT), (_FFN_SPLIT, D_FF))
FFN_SUB_ROWS = 256


def _ffn_kernel(row, fused, *refs):
    if fused:
        x_ref, wo_ref, h_ref, mod_ref, n2_ref, wgu_ref, wd_ref, o_ref = refs
    else:
        h_ref, mod_ref, n2_ref, wgu_ref, wd_ref, o_ref = refs
    t = h_ref.shape[0]
    sub = min(FFN_SUB_ROWS, t)
    for sb in range(t // sub):
        rs = slice(sb * sub, (sb + 1) * sub)
        h = h_ref[rs, :]
        if fused:
            h = h + _mod(mod_ref, row, G1) * jnp.dot(x_ref[rs, :], wo_ref[...],
                                                     preferred_element_type=F32)
        a = _mod_norm(h, n2_ref[...], _mod(mod_ref, row, SC2), _mod(mod_ref, row, SH2)).astype(BF16)
        acc = None
        for lo, hi in FFN_CHUNKS:
            g = jnp.dot(a, wgu_ref[:, lo:hi], preferred_element_type=F32)
            u = jnp.dot(a, wgu_ref[:, D_FF + lo:D_FF + hi], preferred_element_type=F32)
            act = (_silu(g) * u).astype(BF16)
            part = jnp.dot(act, wd_ref[lo:hi, :], preferred_element_type=F32)
            acc = part if acc is None else acc + part
        o_ref[rs, :] = h + _mod(mod_ref, row, G2) * acc


def _ffn(h, mods, layer, row, n2, wgu, wd, attn=None):
    n = h.shape[0]
    t = _row_tile(n)
    row_spec = pl.BlockSpec((t, D_MODEL), lambda i: (i, 0))
    in_specs = [row_spec,
                pl.BlockSpec((None, 8, 6 * D_MODEL), lambda i: (layer, 0, 0)),
                _const_spec((1, D_MODEL)),
                _layer_spec((D_MODEL, 2 * D_FF), layer),
                _layer_spec((D_FF, D_MODEL), layer)]
    args = [h, mods, n2.reshape(1, D_MODEL), wgu, wd]
    if attn is not None:
        in_specs = [row_spec, _const_spec((D_MODEL, D_MODEL))] + in_specs
        args = list(attn) + args
    return pl.pallas_call(
        functools.partial(_ffn_kernel, row, attn is not None),
        grid=(n // t,),
        in_specs=in_specs,
        out_specs=row_spec,
        out_shape=jax.ShapeDtypeStruct((n, D_MODEL), F32),
        compiler_params=_cparams(1),
        name="ffn",
    )(*args)


def _in_proj_kernel(row, mode, *refs):
    if mode == "short":
        h_ref, mod_ref, n1_ref, w_ref, b_out, cu_out = refs
    else:
        h_ref, mod_ref, n1_ref, w_ref, bias_ref, glu_out = refs
    t = h_ref.shape[0]
    sub = min(QKV_SUB_ROWS, t)
    for sb in range(t // sub):
        rs = slice(sb * sub, (sb + 1) * sub)
        a = _mod_norm(h_ref[rs, :], n1_ref[...], _mod(mod_ref, row, SC1), _mod(mod_ref, row, SH1))
        y = jnp.dot(a.astype(BF16), w_ref[...], preferred_element_type=F32)
        if mode == "short":
            b_out[rs, :] = y[:, :D_MODEL]
            cu_out[rs, :] = y[:, D_MODEL:2 * D_MODEL] * y[:, 2 * D_MODEL:]
        else:
            y = y + bias_ref[...]
            g = y[:, D_MODEL:]
            glu_out[rs, :] = y[:, :D_MODEL] * (1.0 / (1.0 + jnp.exp(-g)))


def _in_proj(h, mods, layer, row, n1, w, mode, bias=None):
    n = h.shape[0]
    t = _row_tile(n)
    wtot = w.shape[1]
    in_specs = [pl.BlockSpec((t, D_MODEL), lambda i: (i, 0)),
                pl.BlockSpec((None, 8, 6 * D_MODEL), lambda i: (layer, 0, 0)),
                _const_spec((1, D_MODEL)),
                _const_spec((D_MODEL, wtot))]
    args = [h, mods, n1.reshape(1, D_MODEL), w]
    row_spec = pl.BlockSpec((t, D_MODEL), lambda i: (i, 0))
    row_shape = jax.ShapeDtypeStruct((n, D_MODEL), F32)
    if mode == "short":
        out_specs, out_shape = [row_spec, row_spec], [row_shape, row_shape]
    else:
        in_specs.append(_const_spec((1, wtot)))
        args.append(bias.reshape(1, wtot))
        out_specs, out_shape = row_spec, row_shape
    return pl.pallas_call(
        functools.partial(_in_proj_kernel, row, mode),
        grid=(n // t,),
        in_specs=in_specs,
        out_specs=out_specs,
        out_shape=out_shape,
        compiler_params=_cparams(1),
        name="in_proj_" + mode,
    )(*args)


CONV_ROW_BLOCK = 64
SUBLANES = 8


def _conv_shifts(taps):
    return [divmod(CONV_HALO - taps // 2 + k, SUBLANES) for k in range(taps)]


def _conv_rems(taps):
    return sorted({r for _, r in _conv_shifts(taps) if r})


def _conv_span(taps, t):
    return t + SUBLANES * max(a for a, _ in _conv_shifts(taps))


def _dwconv_tile(xs_ref, sh_ref, u_ref, prev_ref, cur_ref, next_ref, w_ref, taps, t):
    i = pl.program_id(0)
    last = pl.num_programs(0) - 1
    ncb = D_MODEL // LANES
    for cb in range(ncb):
        cs = slice(cb * LANES, (cb + 1) * LANES)
        xs_ref[cb, 0:CONV_HALO, :] = jnp.where(i > 0, prev_ref[:, cs], 0.0)
        xs_ref[cb, CONV_HALO:CONV_HALO + t, :] = cur_ref[:, cs]
        xs_ref[cb, CONV_HALO + t:, :] = jnp.where(i < last, next_ref[:, cs], 0.0)
    shifts = _conv_shifts(taps)
    rems = _conv_rems(taps)
    span = _conv_span(taps, t)
    for j, r in enumerate(rems):
        sh_ref[j] = xs_ref[:, r:r + span, :]

    def block(idx, carry):
        rb, cb = idx // ncb, idx % ncb
        row0 = pl.multiple_of(rb * CONV_ROW_BLOCK, CONV_ROW_BLOCK)
        accs = [None, None]
        for k, (a, r) in enumerate(shifts):
            rows = pl.ds(row0 + a * SUBLANES, CONV_ROW_BLOCK)
            x = xs_ref[cb, rows, :] if r == 0 else sh_ref[rems.index(r), cb, rows, :]
            term = x * w_ref[cb, k:k + 1, :]
            accs[k % 2] = term if accs[k % 2] is None else accs[k % 2] + term
        u_ref[cb, pl.ds(row0, CONV_ROW_BLOCK), :] = accs[0] + accs[1]
        return carry

    lax.fori_loop(0, (t // CONV_ROW_BLOCK) * ncb, block, 0)
    return jnp.concatenate([u_ref[cb] for cb in range(ncb)], axis=1)


def _conv_out_kernel(row, mode, taps, t, *refs):
    if mode == "short":
        (prev_ref, cur_ref, next_ref, cw_ref, b_ref, h_ref, mod_ref, w_ref, o_ref,
         xs_ref, sh_ref, u_ref) = refs
    else:
        (prev_ref, cur_ref, next_ref, cw_ref, dwb_ref, lng_ref, lnb_ref, h_ref, mod_ref,
         w_ref, pb_ref, o_ref, xs_ref, sh_ref, u_ref) = refs
    u = _dwconv_tile(xs_ref, sh_ref, u_ref, prev_ref, cur_ref, next_ref, cw_ref, taps, t)
    if mode == "short":
        y = jnp.dot((b_ref[...] * u).astype(BF16), w_ref[...], preferred_element_type=F32)
    else:
        u = u + dwb_ref[...]
        mu = jnp.mean(u, axis=-1, keepdims=True)
        uc = u - mu
        var = jnp.mean(uc * uc, axis=-1, keepdims=True)
        z = _silu(uc * lax.rsqrt(var + NORM_EPS) * lng_ref[...] + lnb_ref[...])
        y = jnp.dot(z.astype(BF16), w_ref[...], preferred_element_type=F32) + pb_ref[...]
    o_ref[...] = h_ref[...] + _mod(mod_ref, row, G1) * y


def _conv_out(x, h, mods, layer, row, conv_w, w, mode, extra):
    n = h.shape[0]
    t = _row_tile(n)
    taps = conv_w.shape[0]
    r = t // CONV_HALO
    last = n // CONV_HALO - 1
    taps_pad = -(-taps // 8) * 8
    ncb = D_MODEL // LANES
    cw = jnp.zeros((taps_pad, D_MODEL), F32).at[:taps].set(conv_w)
    cw = cw.reshape(taps_pad, ncb, LANES).transpose(1, 0, 2)
    row_spec = pl.BlockSpec((t, D_MODEL), lambda i: (i, 0))
    vec_spec = _const_spec((1, D_MODEL))
    in_specs = [pl.BlockSpec((CONV_HALO, D_MODEL), lambda i: (jnp.maximum(i * r - 1, 0), 0)),
                row_spec,
                pl.BlockSpec((CONV_HALO, D_MODEL), lambda i: (jnp.minimum((i + 1) * r, last), 0)),
                _const_spec((ncb, taps_pad, LANES))]
    args = [x, x, x, cw]
    mod_spec = pl.BlockSpec((None, 8, 6 * D_MODEL), lambda i: (layer, 0, 0))
    if mode == "short":
        (b_gate,) = extra
        in_specs += [row_spec, row_spec, mod_spec, _const_spec((D_MODEL, D_MODEL))]
        args += [b_gate, h, mods, w]
    else:
        dw_b, ln_g, ln_b, pw_b = extra
        in_specs += [vec_spec, vec_spec, vec_spec, row_spec, mod_spec,
                     _const_spec((D_MODEL, D_MODEL)), vec_spec]
        args += [dw_b.reshape(1, D_MODEL), ln_g.reshape(1, D_MODEL), ln_b.reshape(1, D_MODEL),
                 h, mods, w, pw_b.reshape(1, D_MODEL)]
    return pl.pallas_call(
        functools.partial(_conv_out_kernel, row, mode, taps, t),
        grid=(n // t,),
        in_specs=in_specs,
        out_specs=row_spec,
        out_shape=jax.ShapeDtypeStruct((n, D_MODEL), F32),
        scratch_shapes=[pltpu.VMEM((ncb, t + 2 * CONV_HALO, LANES), F32),
                        pltpu.VMEM((len(_conv_rems(taps)), ncb, _conv_span(taps, t), LANES), F32),
                        pltpu.VMEM((ncb, t, LANES), F32)],
        compiler_params=_cparams(1),
        name="conv_out_" + mode,
    )(*args)


def _diff_qkv_weight(w_qkv):
    w = w_qkv.astype(BF16)

    def head_major(cols):
        d_in = cols.shape[0]
        return cols.reshape(d_in, 2, DIFF_HEADS, HEAD_DIM).transpose(0, 2, 1, 3).reshape(d_in, D_MODEL)

    return jnp.concatenate([head_major(w[:, :D_MODEL]), head_major(w[:, D_MODEL:2 * D_MODEL]),
                            w[:, 2 * D_MODEL:]], axis=1)


def _win_qkv_weight(w_qkv):
    qw = WIN_Q_HEADS * HEAD_DIM
    kvw = WIN_KV_HEADS * HEAD_DIM
    w = w_qkv.astype(BF16)

    def twice(cols):
        d_in = cols.shape[0]
        heads = cols.reshape(d_in, WIN_KV_HEADS, 1, HEAD_DIM)
        return jnp.broadcast_to(heads, (d_in, WIN_KV_HEADS, 2, HEAD_DIM)).reshape(d_in, 2 * kvw)

    return jnp.concatenate([w[:, :qw], twice(w[:, qw:qw + kvw]), twice(w[:, qw + kvw:])], axis=1)


def kernel(x, c, ctx, c_ctx, ada_w, ada_b, norm1, norm2, ffn_w_gate_up, ffn_w_down, diff_w_qkv, diff_w_o, diff_q_norm, diff_k_norm, diff_lam_q1, diff_lam_k1, diff_lam_q2, diff_lam_k2, diff_subln, sc_w_in, sc_conv_w, sc_w_out, win_w_qkv, win_w_o, win_q_norm, win_k_norm, win_sink, cf_w_pw1, cf_b_pw1, cf_dw_w, cf_dw_b, cf_ln_g, cf_ln_b, cf_w_pw2, cf_b_pw2):
    n = x.shape[1]
    h, hc = x.reshape(n, D_MODEL), ctx.reshape(ctx.shape[1], D_MODEL)
    mods = _ada_mod(c, c_ctx, ada_w, ada_b)
    wgu, wd = ffn_w_gate_up.astype(BF16), ffn_w_down.astype(BF16)
    tab = _rope_tables(n, _row_tile(n))
    for i in range(DEPTH):
        kind, j = i % N_MIXERS, i // N_MIXERS
        with_ctx = i < DEPTH - 1
        attn_l = attn_c = None
        if kind == 0:
            lam_init = 0.8 - 0.6 * math.exp(-0.3 * i)
            w = _diff_qkv_weight(diff_w_qkv[j])
            w_o = diff_w_o[j].astype(BF16)
            lamv = jnp.zeros((8, LANES), F32).at[0:4, 0:HEAD_DIM].set(
                jnp.stack([diff_lam_q1[j], diff_lam_k1[j], diff_lam_q2[j], diff_lam_k2[j]]))
            q_scale = HEAD_DIM ** -0.5 * LOG2E
            proj = functools.partial(_qkv_proj, mods=mods, layer=i, n1=norm1[i], w=w,
                                     q_gain=diff_q_norm[j], k_gain=diff_k_norm[j],
                                     kw=D_MODEL, vw=D_MODEL, q_scale=q_scale)
            q_l, k_l, v_l = proj(h, row=0, tab=tab)
            q_c, k_c, v_c = proj(hc, row=1, tab=None)
            attn_l = (_diff_attn(q_l, [(k_l, v_l), (k_c, v_c)], lamv, diff_subln[j], lam_init), w_o)
            if with_ctx:
                attn_c = (_diff_attn(q_c, [(k_c, v_c)], lamv, diff_subln[j], lam_init), w_o)
        elif kind == 1:
            w_in = sc_w_in[j].astype(BF16)
            w_out = sc_w_out[j].astype(BF16)
            b_l, cu_l = _in_proj(h, mods, i, 0, norm1[i], w_in, "short")
            h = _conv_out(cu_l, h, mods, i, 0, sc_conv_w[j], w_out, "short", (b_l,))
            if with_ctx:
                b_c, cu_c = _in_proj(hc, mods, i, 1, norm1[i], w_in, "short")
                hc = _conv_out(cu_c, hc, mods, i, 1, sc_conv_w[j], w_out, "short", (b_c,))
        elif kind == 2:
            w = _win_qkv_weight(win_w_qkv[j])
            w_o = win_w_o[j].astype(BF16)
            kvw = WIN_KV_HEADS * LANES
            sink_rows = jnp.broadcast_to(win_sink[j][:, None], (WIN_Q_HEADS, LANES))
            proj = functools.partial(_qkv_proj, mods=mods, layer=i, n1=norm1[i], w=w,
                                     q_gain=win_q_norm[j], k_gain=win_k_norm[j],
                                     kw=kvw, vw=kvw, q_scale=HEAD_DIM ** -0.5)
            q_l, k_l, v_l = proj(h, row=0, tab=tab)
            q_c, k_c, v_c = proj(hc, row=1, tab=None)
            o_l = _win_attn(q_l, k_l, v_l, k_c, v_c, sink_rows, True)
            attn_l = (o_l, w_o)
            if with_ctx:
                attn_c = (_win_attn(q_c, None, None, k_c, v_c, sink_rows, False), w_o)
        else:
            w1 = cf_w_pw1[j].astype(BF16)
            w2 = cf_w_pw2[j].astype(BF16)
            extra = (cf_dw_b[j], cf_ln_g[j], cf_ln_b[j], cf_b_pw2[j])
            glu_l = _in_proj(h, mods, i, 0, norm1[i], w1, "glu", cf_b_pw1[j])
            h = _conv_out(glu_l, h, mods, i, 0, cf_dw_w[j], w2, "conf", extra)
            if with_ctx:
                glu_c = _in_proj(hc, mods, i, 1, norm1[i], w1, "glu", cf_b_pw1[j])
                hc = _conv_out(glu_c, hc, mods, i, 1, cf_dw_w[j], w2, "conf", extra)
        h = _ffn(h, mods, i, 0, norm2[i], wgu, wd, attn_l)
        if with_ctx:
            hc = _ffn(hc, mods, i, 1, norm2[i], wgu, wd, attn_c)
    return h.reshape(1, n, D_MODEL)
```

```python
import functools
import math

import jax
import jax.numpy as jnp
import numpy as np
from jax import lax
from jax.experimental import pallas as pl
from jax.experimental.pallas import tpu as pltpu

F32 = jnp.float32
BF16 = jnp.bfloat16

D_MODEL = 1024
DEPTH = 4
N_MIXERS = 4
GRID_W = 64
NORM_EPS = 1e-6
ROPE_THETA = 10000.0
HEAD_DIM = 64
NEG_INF = -1e30
DIFF_HEADS = D_MODEL // (2 * HEAD_DIM)
SHORT_CONV_W = 3
WIN_Q_HEADS = D_MODEL // HEAD_DIM
WIN_KV_HEADS = 4
WIN_GROUP = WIN_Q_HEADS // WIN_KV_HEADS
WINDOW = 128
CONF_CONV_W = 31
D_FF = -(-(8 * D_MODEL) // (3 * 256)) * 256

LANES = 128
MXU_TILE = 256
CONV_HALO = 16
VMEM_LIMIT_BYTES = 56 * 1024 * 1024

LOG2E = math.log2(math.e)
SH1, SC1, G1, SH2, SC2, G2 = range(6)


def _cparams(n_axes):
    return pltpu.CompilerParams(dimension_semantics=("parallel",) * n_axes,
                                vmem_limit_bytes=VMEM_LIMIT_BYTES)


def _const_spec(shape):
    nd = len(shape)
    return pl.BlockSpec(shape, lambda *_: (0,) * nd, pipeline_mode=pl.Buffered(1))


def _layer_spec(shape, layer):
    nd = len(shape)
    return pl.BlockSpec((None,) + tuple(shape), lambda *_: (layer,) + (0,) * nd,
                        pipeline_mode=pl.Buffered(1))


def _mod(mod_ref, row, idx):
    return mod_ref[row:row + 1, idx * D_MODEL:(idx + 1) * D_MODEL]


def _mod_norm(h, gain, scale, shift):
    ms = jnp.mean(h * h, axis=-1, keepdims=True)
    return (h * lax.rsqrt(ms + NORM_EPS)) * (gain * (1.0 + scale)) + shift


def _silu(x):
    return x * (1.0 / (1.0 + jnp.exp(-x)))


def _row_tile(n):
    return 512 if n % 512 == 0 else n


def _ada_kernel(act_ref, w_ref, b_ref, o_ref):
    a = _silu(act_ref[...])
    o_ref[...] = jnp.dot(a.astype(BF16), w_ref[...].astype(BF16),
                         preferred_element_type=F32) + b_ref[...]


def _ada_mod(c, c_ctx, ada_w, ada_b):
    act = jnp.zeros((8, D_MODEL), F32).at[0].set(c[0]).at[1].set(c_ctx)
    cols = 1536
    nc = 6 * D_MODEL // cols
    return pl.pallas_call(
        _ada_kernel,
        grid=(DEPTH, nc),
        in_specs=[pl.BlockSpec((8, D_MODEL), lambda i, j: (0, 0)),
                  pl.BlockSpec((None, D_MODEL, cols), lambda i, j: (i, 0, j)),
                  pl.BlockSpec((None, 1, cols), lambda i, j: (i, 0, j))],
        out_specs=pl.BlockSpec((None, 8, cols), lambda i, j: (i, 0, j)),
        out_shape=jax.ShapeDtypeStruct((DEPTH, 8, 6 * D_MODEL), F32),
        compiler_params=_cparams(2),
        name="ada_mod",
    )(act, ada_w, ada_b.reshape(DEPTH, 1, 6 * D_MODEL))


def _group_mean_matrix(width):
    g = np.arange(width) // HEAD_DIM
    return jnp.asarray((g[:, None] == g[None, :]).astype(np.float32) / HEAD_DIM, BF16)


def _rope_tables(n, t):
    n_freq = HEAD_DIM // 4
    inv_freq = ROPE_THETA ** (-jnp.arange(n_freq, dtype=F32) / n_freq)
    reps = LANES // HEAD_DIM

    def tables(count, is_row):
        ang = jnp.arange(count, dtype=F32)[:, None] * inv_freq[None, :]
        cos, sin, zero = jnp.cos(ang), jnp.sin(ang), jnp.zeros_like(ang)
        halves = [[cos, cos], [-sin, zero], [zero, sin]]
        out = []
        for first, second in halves:
            pair = [first, second, zero, zero] if is_row else [zero, zero, first, second]
            out.append(jnp.tile(jnp.concatenate(pair, axis=-1), (1, reps)))
        return jnp.concatenate(out, axis=-1)

    return tables(n // GRID_W, True), jnp.tile(tables(GRID_W, False), (t // GRID_W, 1))


def _rope(x, tab):
    w = x.shape[-1]
    reps = w // LANES
    cos = jnp.tile(tab[:, 0:LANES], (1, reps))
    sin_up = jnp.tile(tab[:, LANES:2 * LANES], (1, reps))
    sin_dn = jnp.tile(tab[:, 2 * LANES:3 * LANES], (1, reps))
    quarter = HEAD_DIM // 4
    return (x * cos + pltpu.roll(x, w - quarter, 1) * sin_up
            + pltpu.roll(x, quarter, 1) * sin_dn)


QKV_SUB_ROWS = 256


def _qkv_kernel(row, rotary, kw, q_scale, *refs):
    if rotary:
        (h_ref, mod_ref, n1_ref, w_ref, gm_ref, qg_ref, kg_ref, rt_ref, ct_ref,
         q_ref, k_ref, v_ref) = refs
    else:
        h_ref, mod_ref, n1_ref, w_ref, gm_ref, qg_ref, kg_ref, q_ref, k_ref, v_ref = refs
    gm = gm_ref[...]

    def head_norm(x, gain, width):
        x2 = (x * x).astype(BF16)
        ms = jnp.concatenate(
            [jnp.dot(x2[:, c:c + MXU_TILE], gm, preferred_element_type=F32)
             for c in range(0, width, MXU_TILE)], axis=1)
        return x * lax.rsqrt(ms + NORM_EPS) * gain

    t = h_ref.shape[0]
    sub = min(QKV_SUB_ROWS, t)
    for sb in range(t // sub):
        rs = slice(sb * sub, (sb + 1) * sub)
        a = _mod_norm(h_ref[rs, :], n1_ref[...], _mod(mod_ref, row, SC1), _mod(mod_ref, row, SH1))
        y = jnp.dot(a.astype(BF16), w_ref[...], preferred_element_type=F32)
        q = head_norm(y[:, :D_MODEL], qg_ref[...], D_MODEL)
        k = head_norm(y[:, D_MODEL:D_MODEL + kw], kg_ref[...], kw)
        if rotary:
            g0 = sb * sub // GRID_W
            row_part = jnp.concatenate(
                [jnp.broadcast_to(rt_ref[g0 + r:g0 + r + 1, :], (GRID_W, 3 * LANES))
                 for r in range(sub // GRID_W)], axis=0)
            tab = row_part + ct_ref[rs, :]
            q, k = _rope(q, tab), _rope(k, tab)
        q_ref[rs, :] = (q * q_scale).astype(BF16)
        k_ref[rs, :] = k.astype(BF16)
        v_ref[rs, :] = y[:, D_MODEL + kw:].astype(BF16)


def _qkv_proj(h, mods, layer, row, n1, w, q_gain, k_gain, kw, vw, q_scale, tab):
    n = h.shape[0]
    t = _row_tile(n)
    rotary = tab is not None
    wtot = D_MODEL + kw + vw
    in_specs = [pl.BlockSpec((t, D_MODEL), lambda i: (i, 0)),
                pl.BlockSpec((None, 8, 6 * D_MODEL), lambda i: (layer, 0, 0)),
                _const_spec((1, D_MODEL)),
                _const_spec((D_MODEL, wtot)),
                _const_spec((MXU_TILE, MXU_TILE)),
                _const_spec((1, D_MODEL)),
                _const_spec((1, kw))]
    args = [h, mods, n1.reshape(1, D_MODEL), w, _group_mean_matrix(MXU_TILE),
            jnp.tile(q_gain, D_MODEL // HEAD_DIM).reshape(1, D_MODEL),
            jnp.tile(k_gain, kw // HEAD_DIM).reshape(1, kw)]
    if rotary:
        row_tab, col_tab = tab
        in_specs += [pl.BlockSpec((t // GRID_W, 3 * LANES), lambda i: (i, 0)),
                     _const_spec((t, 3 * LANES))]
        args += [row_tab, col_tab]
    return pl.pallas_call(
        functools.partial(_qkv_kernel, row, rotary, kw, q_scale),
        grid=(n // t,),
        in_specs=in_specs,
        out_specs=[pl.BlockSpec((t, D_MODEL), lambda i: (i, 0)),
                   pl.BlockSpec((t, kw), lambda i: (i, 0)),
                   pl.BlockSpec((t, vw), lambda i: (i, 0))],
        out_shape=[jax.ShapeDtypeStruct((n, D_MODEL), BF16),
                   jax.ShapeDtypeStruct((n, kw), BF16),
                   jax.ShapeDtypeStruct((n, vw), BF16)],
        compiler_params=_cparams(1),
        name="qkv_proj",
    )(*args)


DIFF_TQ = 256
DIFF_TK = 1280
DIFF_VT_ROWS = LANES + 16


def _diff_attn_kernel(tq, tk, n_tiles, n_chunks, src_rows, lam_init, q_ref, lamv_ref, sub_ref, *refs):
    n_src = len(src_rows)
    kv_refs = refs[:2 * n_src]
    o_ref, qq_ref, k_ref, vt_ref, s_ref, m_ref, acc_ref = refs[2 * n_src:]
    n_keys = n_chunks * tk
    lane = lax.broadcasted_iota(jnp.int32, (tq, LANES), 1)
    lv = lamv_ref[...]
    lam = (jnp.exp(jnp.sum(lv[0:1] * lv[1:2], axis=-1, keepdims=True))
           - jnp.exp(jnp.sum(lv[2:3] * lv[3:4], axis=-1, keepdims=True)) + lam_init)
    out_gain = sub_ref[...] * (1.0 - lam_init)

    off = 0
    for i_src, rows in enumerate(src_rows):
        ks_ref, vs_ref = kv_refs[2 * i_src], kv_refs[2 * i_src + 1]
        for r0 in range(0, rows, tk):
            r1 = min(r0 + tk, rows)
            k_ref[off + r0:off + r1, :] = ks_ref[r0:r1, :]
            vt_ref[0:LANES, off + r0:off + r1] = vs_ref[r0:r1, :].astype(F32).T.astype(BF16)
        off += rows
    vt_ref[LANES:, :] = jnp.ones((DIFF_VT_ROWS - LANES, n_keys), BF16)

    def rows_of(t):
        return pl.ds(t * tq, tq) if isinstance(t, int) else pl.ds(pl.multiple_of(t * tq, tq), tq)

    def stack_queries(t, slot):
        q = q_ref[rows_of(t), :]
        zero = jnp.zeros_like(q)
        qq_ref[slot, 0:tq, :] = jnp.where(lane < HEAD_DIM, q, zero)
        qq_ref[slot, tq:, :] = jnp.where(lane >= HEAD_DIM, q, zero)

    def scores(slot, c, buf):
        s_ref[buf] = lax.dot_general(k_ref[c * tk:(c + 1) * tk, :], qq_ref[slot],
                                     (((1,), (1,)), ((), ())), preferred_element_type=F32)

    def softmax_pv(c, buf):
        s = s_ref[buf]
        m_prev = m_ref[...]
        m_new = jnp.maximum(m_prev, jnp.max(s, axis=0, keepdims=True))
        alpha = jnp.exp2(m_prev - m_new)
        p = jnp.exp2(s - m_new).astype(BF16)
        pv = jnp.dot(vt_ref[:, c * tk:(c + 1) * tk], p, preferred_element_type=F32)
        acc_ref[...] = acc_ref[...] * alpha + pv
        m_ref[...] = m_new

    def tile(t, t_next, slot, first_buf):
        m_ref[...] = jnp.full(m_ref.shape, NEG_INF, F32)
        acc_ref[...] = jnp.zeros(acc_ref.shape, F32)
        for c in range(n_chunks):
            buf = (first_buf + c) % 2
            if c + 1 < n_chunks:
                scores(slot, c + 1, 1 - buf)
            elif t_next is not None:
                stack_queries(t_next, 1 - slot)
                scores(1 - slot, 0, 1 - buf)
            softmax_pv(c, buf)
        acc = acc_ref[...]
        o_t = acc[0:LANES, :] / acc[LANES:LANES + 1, :]
        d = (o_t[:, :tq] - lam * o_t[:, tq:]).T
        ms = jnp.mean(d * d, axis=-1, keepdims=True)
        o_ref[rows_of(t), :] = (d * lax.rsqrt(ms + NORM_EPS) * out_gain).astype(BF16)

    stack_queries(0, 0)
    scores(0, 0, 0)
    flip = n_chunks % 2
    if n_tiles % 2 == 0 and n_tiles > 2:
        def pair(tp, carry):
            t0 = 2 * tp
            tile(t0, t0 + 1, 0, 0)
            tile(t0 + 1, jnp.minimum(t0 + 2, n_tiles - 1), 1, flip)
            return carry
        lax.fori_loop(0, n_tiles // 2, pair, 0)
    else:
        for t in range(n_tiles):
            tile(t, t + 1 if t + 1 < n_tiles else None, t % 2, (t * flip) % 2)


def _diff_attn(q, kv_sources, lamv, subln, lam_init):
    nq = q.shape[0]
    tq = min(DIFF_TQ, nq)
    src_rows = tuple(k.shape[0] for k, _ in kv_sources)
    n_keys = sum(src_rows)
    tk = min(DIFF_TK, n_keys)
    n_chunks = n_keys // tk
    assert nq % tq == 0 and n_keys % tk == 0
    in_specs = [pl.BlockSpec((nq, LANES), lambda h: (0, h)),
                _const_spec((8, LANES)),
                _const_spec((1, LANES))]
    args = [q, lamv, subln.reshape(1, LANES)]
    for k, v in kv_sources:
        in_specs += [pl.BlockSpec((k.shape[0], LANES), lambda h: (0, h))] * 2
        args += [k, v]
    return pl.pallas_call(
        functools.partial(_diff_attn_kernel, tq, tk, nq // tq, n_chunks, src_rows, lam_init),
        grid=(DIFF_HEADS,),
        in_specs=in_specs,
        out_specs=pl.BlockSpec((nq, LANES), lambda h: (0, h)),
        out_shape=jax.ShapeDtypeStruct((nq, D_MODEL), BF16),
        scratch_shapes=[pltpu.VMEM((2, 2 * tq, LANES), BF16),
                        pltpu.VMEM((n_keys, LANES), BF16),
                        pltpu.VMEM((DIFF_VT_ROWS, n_keys), BF16),
                        pltpu.VMEM((2, tk, 2 * tq), F32),
                        pltpu.VMEM((1, 2 * tq), F32),
                        pltpu.VMEM((DIFF_VT_ROWS, 2 * tq), F32)],
        compiler_params=_cparams(1),
        name="diff_attn",
    )(*args)


WIN_VT_ROWS = LANES + 16


def _win_attn_kernel(tq, n_lat, with_window, *refs):
    if with_window:
        (q_ref, kp_ref, kc_ref, kn_ref, vp_ref, vc_ref, vn_ref,
         kx_ref, vx_ref, sink_ref, o_ref, s_ref) = refs
    else:
        q_ref, kx_ref, vx_ref, sink_ref, o_ref, s_ref = refs
    i = pl.program_id(0)
    n_ctx = kx_ref.shape[0]
    sub = WINDOW if with_window else tq
    n_win = 3 * WINDOW if with_window else 0
    n_keys = n_win + n_ctx
    low = lax.broadcasted_iota(jnp.int32, (sub, LANES), 1) < HEAD_DIM
    nt = (((1,), (1,)), ((), ()))
    units = [(sb, kvh) for sb in range(tq // sub) for kvh in range(WIN_KV_HEADS)]

    def keys_or_values(u, p_ref, c_ref, n_ref, x_ref):
        sb, kvh = units[u]
        ks = slice(kvh * LANES, (kvh + 1) * LANES)
        if not with_window:
            return x_ref[:, ks]
        span = jnp.concatenate([p_ref[:, ks], c_ref[:, ks], n_ref[:, ks]], axis=0)
        return jnp.concatenate([span[sb * WINDOW:sb * WINDOW + n_win], x_ref[:, ks]], axis=0)

    def scores(u, buf):
        sb, kvh = units[u]
        r0 = sb * sub
        qs = []
        for pr in range(WIN_GROUP // 2):
            blk = kvh * (WIN_GROUP // 2) + pr
            qp = q_ref[r0:r0 + sub, blk * LANES:(blk + 1) * LANES]
            zero = jnp.zeros_like(qp)
            qs += [jnp.where(low, qp, zero), jnp.where(low, zero, qp)]
        kk = keys_or_values(u, kp_ref, kc_ref, kn_ref, kx_ref) if with_window else \
            keys_or_values(u, None, None, None, kx_ref)
        s_ref[buf] = lax.dot_general(kk, jnp.concatenate(qs, axis=0), nt,
                                     preferred_element_type=F32)

    def softmax_pv(u, buf):
        sb, kvh = units[u]
        r0 = sb * sub
        s = s_ref[buf]
        if with_window:
            kpos = i * tq + r0 - WINDOW + lax.broadcasted_iota(jnp.int32, (n_win, sub), 0)
            qpos = i * tq + r0 + lax.broadcasted_iota(jnp.int32, (n_win, sub), 1)
            valid = (jnp.abs(kpos - qpos) <= WINDOW) & (kpos >= 0) & (kpos < n_lat)
            bias = jnp.where(valid, 0.0, NEG_INF)
            bias = jnp.concatenate([bias] * WIN_GROUP, axis=1)
            s = jnp.concatenate([s[:n_win] + bias, s[n_win:]], axis=0)
        heads = [kvh * WIN_GROUP + g for g in range(WIN_GROUP)]
        sink = jnp.concatenate(
            [jnp.tile(sink_ref[hd:hd + 1, :], (1, sub // LANES)) for hd in heads], axis=1)
        m = jnp.maximum(jnp.max(s, axis=0, keepdims=True), sink)
        p = jnp.exp(s - m).astype(BF16)
        vv = keys_or_values(u, vp_ref, vc_ref, vn_ref, vx_ref) if with_window else \
            keys_or_values(u, None, None, None, vx_ref)
        v_t = jnp.concatenate([vv.astype(F32).T.astype(BF16),
                               jnp.ones((WIN_VT_ROWS - LANES, n_keys), BF16)], axis=0)
        pv = jnp.dot(v_t, p, preferred_element_type=F32)
        den = pv[LANES:LANES + 1, :] + jnp.exp(sink - m)
        o = (pv[0:LANES, :] / den).T
        for pr in range(WIN_GROUP // 2):
            blk = kvh * (WIN_GROUP // 2) + pr
            lo = o[(2 * pr) * sub:(2 * pr + 1) * sub]
            hi = o[(2 * pr + 1) * sub:(2 * pr + 2) * sub]
            o_ref[r0:r0 + sub, blk * LANES:(blk + 1) * LANES] = jnp.where(low, lo, hi).astype(BF16)

    scores(0, 0)
    for u in range(len(units)):
        if u + 1 < len(units):
            scores(u + 1, (u + 1) % 2)
        softmax_pv(u, u % 2)


def _win_attn(q, k, v, kx, vx, sink_rows, with_window):
    nq = q.shape[0]
    tq = min(256, nq)
    n_ctx = kx.shape[0]
    kvw = WIN_KV_HEADS * LANES
    in_specs = [pl.BlockSpec((tq, D_MODEL), lambda i: (i, 0))]
    args = [q]
    if with_window:
        r = tq // WINDOW
        last = nq // WINDOW - 1
        prev_spec = pl.BlockSpec((WINDOW, kvw), lambda i: (jnp.maximum(i * r - 1, 0), 0))
        cur_spec = pl.BlockSpec((tq, kvw), lambda i: (i, 0))
        next_spec = pl.BlockSpec((WINDOW, kvw), lambda i: (jnp.minimum((i + 1) * r, last), 0))
        in_specs += [prev_spec, cur_spec, next_spec] * 2
        args += [k, k, k, v, v, v]
    in_specs += [_const_spec((n_ctx, kvw)), _const_spec((n_ctx, kvw)),
                 _const_spec((WIN_Q_HEADS, LANES))]
    args += [kx, vx, sink_rows]
    return pl.pallas_call(
        functools.partial(_win_attn_kernel, tq, nq, with_window),
        grid=(nq // tq,),
        in_specs=in_specs,
        out_specs=pl.BlockSpec((tq, D_MODEL), lambda i: (i, 0)),
        out_shape=jax.ShapeDtypeStruct((nq, D_MODEL), BF16),
        scratch_shapes=[pltpu.VMEM((2, (3 * WINDOW if with_window else 0) + n_ctx,
                                    WIN_GROUP * (WINDOW if with_window else tq)), F32)],
        compiler_params=_cparams(1),
        name="win_attn",
    )(*args)


_FFN_SPLIT = -(-(D_FF // MXU_TILE) // 2) * MXU_TILE
FFN_CHUNKS = ((0, _FFN_SPLIT), (_FFN_SPLIT, D_FF))
FFN_SUB_ROWS = 256


def _ffn_kernel(row, fused, *refs):
    if fused:
        x_ref, wo_ref, h_ref, mod_ref, n2_ref, wgu_ref, wd_ref, o_ref = refs
    else:
        h_ref, mod_ref, n2_ref, wgu_ref, wd_ref, o_ref = refs
    t = h_ref.shape[0]
    sub = min(FFN_SUB_ROWS, t)
    for sb in range(t // sub):
        rs = slice(sb * sub, (sb + 1) * sub)
        h = h_ref[rs, :]
        if fused:
            h = h + _mod(mod_ref, row, G1) * jnp.dot(x_ref[rs, :], wo_ref[...],
                                                     preferred_element_type=F32)
        a = _mod_norm(h, n2_ref[...], _mod(mod_ref, row, SC2), _mod(mod_ref, row, SH2)).astype(BF16)
        acc = None
        for lo, hi in FFN_CHUNKS:
            g = jnp.dot(a, wgu_ref[:, lo:hi], preferred_element_type=F32)
            u = jnp.dot(a, wgu_ref[:, D_FF + lo:D_FF + hi], preferred_element_type=F32)
            act = (_silu(g) * u).astype(BF16)
            part = jnp.dot(act, wd_ref[lo:hi, :], preferred_element_type=F32)
            acc = part if acc is None else acc + part
        o_ref[rs, :] = h + _mod(mod_ref, row, G2) * acc


def _ffn(h, mods, layer, row, n2, wgu, wd, attn=None):
    n = h.shape[0]
    t = _row_tile(n)
    row_spec = pl.BlockSpec((t, D_MODEL), lambda i: (i, 0))
    in_specs = [row_spec,
                pl.BlockSpec((None, 8, 6 * D_MODEL), lambda i: (layer, 0, 0)),
                _const_spec((1, D_MODEL)),
                _layer_spec((D_MODEL, 2 * D_FF), layer),
                _layer_spec((D_FF, D_MODEL), layer)]
    args = [h, mods, n2.reshape(1, D_MODEL), wgu, wd]
    if attn is not None:
        in_specs = [row_spec, _const_spec((D_MODEL, D_MODEL))] + in_specs
        args = list(attn) + args
    return pl.pallas_call(
        functools.partial(_ffn_kernel, row, attn is not None),
        grid=(n // t,),
        in_specs=in_specs,
        out_specs=row_spec,
        out_shape=jax.ShapeDtypeStruct((n, D_MODEL), F32),
        compiler_params=_cparams(1),
        name="ffn",
    )(*args)


def _in_proj_kernel(row, mode, *refs):
    if mode == "short":
        h_ref, mod_ref, n1_ref, w_ref, b_out, cu_out = refs
    else:
        h_ref, mod_ref, n1_ref, w_ref, bias_ref, glu_out = refs
    t = h_ref.shape[0]
    sub = min(QKV_SUB_ROWS, t)
    for sb in range(t // sub):
        rs = slice(sb * sub, (sb + 1) * sub)
        a = _mod_norm(h_ref[rs, :], n1_ref[...], _mod(mod_ref, row, SC1), _mod(mod_ref, row, SH1))
        y = jnp.dot(a.astype(BF16), w_ref[...], preferred_element_type=F32)
        if mode == "short":
            b_out[rs, :] = y[:, :D_MODEL]
            cu_out[rs, :] = y[:, D_MODEL:2 * D_MODEL] * y[:, 2 * D_MODEL:]
        else:
            y = y + bias_ref[...]
            g = y[:, D_MODEL:]
            glu_out[rs, :] = y[:, :D_MODEL] * (1.0 / (1.0 + jnp.exp(-g)))


def _in_proj(h, mods, layer, row, n1, w, mode, bias=None):
    n = h.shape[0]
    t = _row_tile(n)
    wtot = w.shape[1]
    in_specs = [pl.BlockSpec((t, D_MODEL), lambda i: (i, 0)),
                pl.BlockSpec((None, 8, 6 * D_MODEL), lambda i: (layer, 0, 0)),
                _const_spec((1, D_MODEL)),
                _const_spec((D_MODEL, wtot))]
    args = [h, mods, n1.reshape(1, D_MODEL), w]
    row_spec = pl.BlockSpec((t, D_MODEL), lambda i: (i, 0))
    row_shape = jax.ShapeDtypeStruct((n, D_MODEL), F32)
    if mode == "short":
        out_specs, out_shape = [row_spec, row_spec], [row_shape, row_shape]
    else:
        in_specs.append(_const_spec((1, wtot)))
        args.append(bias.reshape(1, wtot))
        out_specs, out_shape = row_spec, row_shape
    return pl.pallas_call(
        functools.partial(_in_proj_kernel, row, mode),
        grid=(n // t,),
        in_specs=in_specs,
        out_specs=out_specs,
        out_shape=out_shape,
        compiler_params=_cparams(1),
        name="in_proj_" + mode,
    )(*args)


CONV_ROW_BLOCK = 64
SUBLANES = 8


def _conv_shifts(taps):
    return [divmod(CONV_HALO - taps // 2 + k, SUBLANES) for k in range(taps)]


def _conv_rems(taps):
    return sorted({r for _, r in _conv_shifts(taps) if r})


def _conv_span(taps, t):
    return t + SUBLANES * max(a for a, _ in _conv_shifts(taps))


def _dwconv_tile(xs_ref, sh_ref, u_ref, prev_ref, cur_ref, next_ref, w_ref, taps, t):
    i = pl.program_id(0)
    last = pl.num_programs(0) - 1
    ncb = D_MODEL // LANES
    for cb in range(ncb):
        cs = slice(cb * LANES, (cb + 1) * LANES)
        xs_ref[cb, 0:CONV_HALO, :] = jnp.where(i > 0, prev_ref[:, cs], 0.0)
        xs_ref[cb, CONV_HALO:CONV_HALO + t, :] = cur_ref[:, cs]
        xs_ref[cb, CONV_HALO + t:, :] = jnp.where(i < last, next_ref[:, cs], 0.0)
    shifts = _conv_shifts(taps)
    rems = _conv_rems(taps)
    span = _conv_span(taps, t)
    for j, r in enumerate(rems):
        sh_ref[j] = xs_ref[:, r:r + span, :]

    def block(idx, carry):
        rb, cb = idx // ncb, idx % ncb
        row0 = pl.multiple_of(rb * CONV_ROW_BLOCK, CONV_ROW_BLOCK)
        accs = [None, None]
        for k, (a, r) in enumerate(shifts):
            rows = pl.ds(row0 + a * SUBLANES, CONV_ROW_BLOCK)
            x = xs_ref[cb, rows, :] if r == 0 else sh_ref[rems.index(r), cb, rows, :]
            term = x * w_ref[cb, k:k + 1, :]
            accs[k % 2] = term if accs[k % 2] is None else accs[k % 2] + term
        u_ref[cb, pl.ds(row0, CONV_ROW_BLOCK), :] = accs[0] + accs[1]
        return carry

    lax.fori_loop(0, (t // CONV_ROW_BLOCK) * ncb, block, 0)
    return jnp.concatenate([u_ref[cb] for cb in range(ncb)], axis=1)


def _conv_out_kernel(row, mode, taps, t, *refs):
    if mode == "short":
        (prev_ref, cur_ref, next_ref, cw_ref, b_ref, h_ref, mod_ref, w_ref, o_ref,
         xs_ref, sh_ref, u_ref) = refs
    else:
        (prev_ref, cur_ref, next_ref, cw_ref, dwb_ref, lng_ref, lnb_ref, h_ref, mod_ref,
         w_ref, pb_ref, o_ref, xs_ref, sh_ref, u_ref) = refs
    u = _dwconv_tile(xs_ref, sh_ref, u_ref, prev_ref, cur_ref, next_ref, cw_ref, taps, t)
    if mode == "short":
        y = jnp.dot((b_ref[...] * u).astype(BF16), w_ref[...], preferred_element_type=F32)
    else:
        u = u + dwb_ref[...]
        mu = jnp.mean(u, axis=-1, keepdims=True)
        uc = u - mu
        var = jnp.mean(uc * uc, axis=-1, keepdims=True)
        z = _silu(uc * lax.rsqrt(var + NORM_EPS) * lng_ref[...] + lnb_ref[...])
        y = jnp.dot(z.astype(BF16), w_ref[...], preferred_element_type=F32) + pb_ref[...]
    o_ref[...] = h_ref[...] + _mod(mod_ref, row, G1) * y


def _conv_out(x, h, mods, layer, row, conv_w, w, mode, extra):
    n = h.shape[0]
    t = _row_tile(n)
    taps = conv_w.shape[0]
    r = t // CONV_HALO
    last = n // CONV_HALO - 1
    taps_pad = -(-taps // 8) * 8
    ncb = D_MODEL // LANES
    cw = jnp.zeros((taps_pad, D_MODEL), F32).at[:taps].set(conv_w)
    cw = cw.reshape(taps_pad, ncb, LANES).transpose(1, 0, 2)
    row_spec = pl.BlockSpec((t, D_MODEL), lambda i: (i, 0))
    vec_spec = _const_spec((1, D_MODEL))
    in_specs = [pl.BlockSpec((CONV_HALO, D_MODEL), lambda i: (jnp.maximum(i * r - 1, 0), 0)),
                row_spec,
                pl.BlockSpec((CONV_HALO, D_MODEL), lambda i: (jnp.minimum((i + 1) * r, last), 0)),
                _const_spec((ncb, taps_pad, LANES))]
    args = [x, x, x, cw]
    mod_spec = pl.BlockSpec((None, 8, 6 * D_MODEL), lambda i: (layer, 0, 0))
    if mode == "short":
        (b_gate,) = extra
        in_specs += [row_spec, row_spec, mod_spec, _const_spec((D_MODEL, D_MODEL))]
        args += [b_gate, h, mods, w]
    else:
        dw_b, ln_g, ln_b, pw_b = extra
        in_specs += [vec_spec, vec_spec, vec_spec, row_spec, mod_spec,
                     _const_spec((D_MODEL, D_MODEL)), vec_spec]
        args += [dw_b.reshape(1, D_MODEL), ln_g.reshape(1, D_MODEL), ln_b.reshape(1, D_MODEL),
                 h, mods, w, pw_b.reshape(1, D_MODEL)]
    return pl.pallas_call(
        functools.partial(_conv_out_kernel, row, mode, taps, t),
        grid=(n // t,),
        in_specs=in_specs,
        out_specs=row_spec,
        out_shape=jax.ShapeDtypeStruct((n, D_MODEL), F32),
        scratch_shapes=[pltpu.VMEM((ncb, t + 2 * CONV_HALO, LANES), F32),
                        pltpu.VMEM((len(_conv_rems(taps)), ncb, _conv_span(taps, t), LANES), F32),
                        pltpu.VMEM((ncb, t, LANES), F32)],
        compiler_params=_cparams(1),
        name="conv_out_" + mode,
    )(*args)


def _diff_qkv_weight(w_qkv):
    w = w_qkv.astype(BF16)

    def head_major(cols):
        d_in = cols.shape[0]
        return cols.reshape(d_in, 2, DIFF_HEADS, HEAD_DIM).transpose(0, 2, 1, 3).reshape(d_in, D_MODEL)

    return jnp.concatenate([head_major(w[:, :D_MODEL]), head_major(w[:, D_MODEL:2 * D_MODEL]),
                            w[:, 2 * D_MODEL:]], axis=1)


def _win_qkv_weight(w_qkv):
    qw = WIN_Q_HEADS * HEAD_DIM
    kvw = WIN_KV_HEADS * HEAD_DIM
    w = w_qkv.astype(BF16)

    def twice(cols):
        d_in = cols.shape[0]
        heads = cols.reshape(d_in, WIN_KV_HEADS, 1, HEAD_DIM)
        return jnp.broadcast_to(heads, (d_in, WIN_KV_HEADS, 2, HEAD_DIM)).reshape(d_in, 2 * kvw)

    return jnp.concatenate([w[:, :qw], twice(w[:, qw:qw + kvw]), twice(w[:, qw + kvw:])], axis=1)


def kernel(x, c, ctx, c_ctx, ada_w, ada_b, norm1, norm2, ffn_w_gate_up, ffn_w_down, diff_w_qkv, diff_w_o, diff_q_norm, diff_k_norm, diff_lam_q1, diff_lam_k1, diff_lam_q2, diff_lam_k2, diff_subln, sc_w_in, sc_conv_w, sc_w_out, win_w_qkv, win_w_o, win_q_norm, win_k_norm, win_sink, cf_w_pw1, cf_b_pw1, cf_dw_w, cf_dw_b, cf_ln_g, cf_ln_b, cf_w_pw2, cf_b_pw2):
    n = x.shape[1]
    h, hc = x.reshape(n, D_MODEL), ctx.reshape(ctx.shape[1], D_MODEL)
    mods = _ada_mod(c, c_ctx, ada_w, ada_b)
    wgu, wd = ffn_w_gate_up.astype(BF16), ffn_w_down.astype(BF16)
    tab = _rope_tables(n, _row_tile(n))
    for i in range(DEPTH):
        kind, j = i % N_MIXERS, i // N_MIXERS
        with_ctx = i < DEPTH - 1
        attn_l = attn_c = None
        if kind == 0:
            lam_init = 0.8 - 0.6 * math.exp(-0.3 * i)
            w = _diff_qkv_weight(diff_w_qkv[j])
            w_o = diff_w_o[j].astype(BF16)
            lamv = jnp.zeros((8, LANES), F32).at[0:4, 0:HEAD_DIM].set(
                jnp.stack([diff_lam_q1[j], diff_lam_k1[j], diff_lam_q2[j], diff_lam_k2[j]]))
            q_scale = HEAD_DIM ** -0.5 * LOG2E
            proj = functools.partial(_qkv_proj, mods=mods, layer=i, n1=norm1[i], w=w,
                                     q_gain=diff_q_norm[j], k_gain=diff_k_norm[j],
                                     kw=D_MODEL, vw=D_MODEL, q_scale=q_scale)
            q_l, k_l, v_l = proj(h, row=0, tab=tab)
            q_c, k_c, v_c = proj(hc, row=1, tab=None)
            attn_l = (_diff_attn(q_l, [(k_l, v_l), (k_c, v_c)], lamv, diff_subln[j], lam_init), w_o)
            if with_ctx:
                attn_c = (_diff_attn(q_c, [(k_c, v_c)], lamv, diff_subln[j], lam_init), w_o)
        elif kind == 1:
            w_in = sc_w_in[j].astype(BF16)
            w_out = sc_w_out[j].astype(BF16)
            b_l, cu_l = _in_proj(h, mods, i, 0, norm1[i], w_in, "short")
            h = _conv_out(cu_l, h, mods, i, 0, sc_conv_w[j], w_out, "short", (b_l,))
            if with_ctx:
                b_c, cu_c = _in_proj(hc, mods, i, 1, norm1[i], w_in, "short")
                hc = _conv_out(cu_c, hc, mods, i, 1, sc_conv_w[j], w_out, "short", (b_c,))
        elif kind == 2:
            w = _win_qkv_weight(win_w_qkv[j])
            w_o = win_w_o[j].astype(BF16)
            kvw = WIN_KV_HEADS * LANES
            sink_rows = jnp.broadcast_to(win_sink[j][:, None], (WIN_Q_HEADS, LANES))
            proj = functools.partial(_qkv_proj, mods=mods, layer=i, n1=norm1[i], w=w,
                                     q_gain=win_q_norm[j], k_gain=win_k_norm[j],
                                     kw=kvw, vw=kvw, q_scale=HEAD_DIM ** -0.5)
            q_l, k_l, v_l = proj(h, row=0, tab=tab)
            q_c, k_c, v_c = proj(hc, row=1, tab=None)
            o_l = _win_attn(q_l, k_l, v_l, k_c, v_c, sink_rows, True)
            attn_l = (o_l, w_o)
            if with_ctx:
                attn_c = (_win_attn(q_c, None, None, k_c, v_c, sink_rows, False), w_o)
        else:
            w1 = cf_w_pw1[j].astype(BF16)
            w2 = cf_w_pw2[j].astype(BF16)
            extra = (cf_dw_b[j], cf_ln_g[j], cf_ln_b[j], cf_b_pw2[j])
            glu_l = _in_proj(h, mods, i, 0, norm1[i], w1, "glu", cf_b_pw1[j])
            h = _conv_out(glu_l, h, mods, i, 0, cf_dw_w[j], w2, "conf", extra)
            if with_ctx:
                glu_c = _in_proj(hc, mods, i, 1, norm1[i], w1, "glu", cf_b_pw1[j])
                hc = _conv_out(glu_c, hc, mods, i, 1, cf_dw_w[j], w2, "conf", extra)
        h = _ffn(h, mods, i, 0, norm2[i], wgu, wd, attn_l)
        if with_ctx:
            hc = _ffn(hc, mods, i, 1, norm2[i], wgu, wd, attn_c)
    return h.reshape(1, n, D_MODEL)
```

```python
import functools
import math

import jax
import jax.numpy as jnp
import numpy as np
from jax import lax
from jax.experimental import pallas as pl
from jax.experimental.pallas import tpu as pltpu

F32 = jnp.float32
BF16 = jnp.bfloat16

D_MODEL = 1024
DEPTH = 4
N_MIXERS = 4
GRID_W = 64
NORM_EPS = 1e-6
ROPE_THETA = 10000.0
HEAD_DIM = 64
NEG_INF = -1e30
DIFF_HEADS = D_MODEL // (2 * HEAD_DIM)
SHORT_CONV_W = 3
WIN_Q_HEADS = D_MODEL // HEAD_DIM
WIN_KV_HEADS = 4
WIN_GROUP = WIN_Q_HEADS // WIN_KV_HEADS
WINDOW = 128
CONF_CONV_W = 31
D_FF = -(-(8 * D_MODEL) // (3 * 256)) * 256

LANES = 128
MXU_TILE = 256
CONV_HALO = 16
VMEM_LIMIT_BYTES = 56 * 1024 * 1024

LOG2E = math.log2(math.e)
SH1, SC1, G1, SH2, SC2, G2 = range(6)


def _cparams(n_axes):
    return pltpu.CompilerParams(dimension_semantics=("parallel",) * n_axes,
                                vmem_limit_bytes=VMEM_LIMIT_BYTES)


def _const_spec(shape):
    nd = len(shape)
    return pl.BlockSpec(shape, lambda *_: (0,) * nd, pipeline_mode=pl.Buffered(1))


def _layer_spec(shape, layer):
    nd = len(shape)
    return pl.BlockSpec((None,) + tuple(shape), lambda *_: (layer,) + (0,) * nd,
                        pipeline_mode=pl.Buffered(1))


def _mod(mod_ref, row, idx):
    return mod_ref[row:row + 1, idx * D_MODEL:(idx + 1) * D_MODEL]


def _mod_norm(h, gain, scale, shift):
    ms = jnp.mean(h * h, axis=-1, keepdims=True)
    return (h * lax.rsqrt(ms + NORM_EPS)) * (gain * (1.0 + scale)) + shift


def _silu(x):
    return x * (1.0 / (1.0 + jnp.exp(-x)))


def _row_tile(n):
    return 512 if n % 512 == 0 else n


def _ada_kernel(act_ref, w_ref, b_ref, o_ref):
    a = _silu(act_ref[...])
    o_ref[...] = jnp.dot(a.astype(BF16), w_ref[...].astype(BF16),
                         preferred_element_type=F32) + b_ref[...]


def _ada_mod(c, c_ctx, ada_w, ada_b):
    act = jnp.zeros((8, D_MODEL), F32).at[0].set(c[0]).at[1].set(c_ctx)
    cols = 1536
    nc = 6 * D_MODEL // cols
    return pl.pallas_call(
        _ada_kernel,
        grid=(DEPTH, nc),
        in_specs=[pl.BlockSpec((8, D_MODEL), lambda i, j: (0, 0)),
                  pl.BlockSpec((None, D_MODEL, cols), lambda i, j: (i, 0, j)),
                  pl.BlockSpec((None, 1, cols), lambda i, j: (i, 0, j))],
        out_specs=pl.BlockSpec((None, 8, cols), lambda i, j: (i, 0, j)),
        out_shape=jax.ShapeDtypeStruct((DEPTH, 8, 6 * D_MODEL), F32),
        compiler_params=_cparams(2),
        name="ada_mod",
    )(act, ada_w, ada_b.reshape(DEPTH, 1, 6 * D_MODEL))


def _group_mean_matrix(width):
    g = np.arange(width) // HEAD_DIM
    return jnp.asarray((g[:, None] == g[None, :]).astype(np.float32) / HEAD_DIM, BF16)


def _rope_tables(n, t):
    n_freq = HEAD_DIM // 4
    inv_freq = ROPE_THETA ** (-jnp.arange(n_freq, dtype=F32) / n_freq)
    reps = LANES // HEAD_DIM

    def tables(count, is_row):
        ang = jnp.arange(count, dtype=F32)[:, None] * inv_freq[None, :]
        cos, sin, zero = jnp.cos(ang), jnp.sin(ang), jnp.zeros_like(ang)
        halves = [[cos, cos], [-sin, zero], [zero, sin]]
        out = []
        for first, second in halves:
            pair = [first, second, zero, zero] if is_row else [zero, zero, first, second]
            out.append(jnp.tile(jnp.concatenate(pair, axis=-1), (1, reps)))
        return jnp.concatenate(out, axis=-1)

    return tables(n // GRID_W, True), jnp.tile(tables(GRID_W, False), (t // GRID_W, 1))


def _rope(x, tab):
    w = x.shape[-1]
    reps = w // LANES
    cos = jnp.tile(tab[:, 0:LANES], (1, reps))
    sin_up = jnp.tile(tab[:, LANES:2 * LANES], (1, reps))
    sin_dn = jnp.tile(tab[:, 2 * LANES:3 * LANES], (1, reps))
    quarter = HEAD_DIM // 4
    return (x * cos + pltpu.roll(x, w - quarter, 1) * sin_up
            + pltpu.roll(x, quarter, 1) * sin_dn)


QKV_SUB_ROWS = 256


def _qkv_kernel(row, rotary, kw, q_scale, *refs):
    if rotary:
        (h_ref, mod_ref, n1_ref, w_ref, gm_ref, qg_ref, kg_ref, rt_ref, ct_ref,
         q_ref, k_ref, v_ref) = refs
    else:
        h_ref, mod_ref, n1_ref, w_ref, gm_ref, qg_ref, kg_ref, q_ref, k_ref, v_ref = refs
    gm = gm_ref[...]

    def head_norm(x, gain, width):
        x2 = (x * x).astype(BF16)
        ms = jnp.concatenate(
            [jnp.dot(x2[:, c:c + MXU_TILE], gm, preferred_element_type=F32)
             for c in range(0, width, MXU_TILE)], axis=1)
        return x * lax.rsqrt(ms + NORM_EPS) * gain

    t = h_ref.shape[0]
    sub = min(QKV_SUB_ROWS, t)
    for sb in range(t // sub):
        rs = slice(sb * sub, (sb + 1) * sub)
        a = _mod_norm(h_ref[rs, :], n1_ref[...], _mod(mod_ref, row, SC1), _mod(mod_ref, row, SH1))
        y = jnp.dot(a.astype(BF16), w_ref[...], preferred_element_type=F32)
        q = head_norm(y[:, :D_MODEL], qg_ref[...], D_MODEL)
        k = head_norm(y[:, D_MODEL:D_MODEL + kw], kg_ref[...], kw)
        if rotary:
            g0 = sb * sub // GRID_W
            row_part = jnp.concatenate(
                [jnp.broadcast_to(rt_ref[g0 + r:g0 + r + 1, :], (GRID_W, 3 * LANES))
                 for r in range(sub // GRID_W)], axis=0)
            tab = row_part + ct_ref[rs, :]
            q, k = _rope(q, tab), _rope(k, tab)
        q_ref[rs, :] = (q * q_scale).astype(BF16)
        k_ref[rs, :] = k.astype(BF16)
        v_ref[rs, :] = y[:, D_MODEL + kw:].astype(BF16)


def _qkv_proj(h, mods, layer, row, n1, w, q_gain, k_gain, kw, vw, q_scale, tab):
    n = h.shape[0]
    t = _row_tile(n)
    rotary = tab is not None
    wtot = D_MODEL + kw + vw
    in_specs = [pl.BlockSpec((t, D_MODEL), lambda i: (i, 0)),
                pl.BlockSpec((None, 8, 6 * D_MODEL), lambda i: (layer, 0, 0)),
                _const_spec((1, D_MODEL)),
                _const_spec((D_MODEL, wtot)),
                _const_spec((MXU_TILE, MXU_TILE)),
                _const_spec((1, D_MODEL)),
                _const_spec((1, kw))]
    args = [h, mods, n1.reshape(1, D_MODEL), w, _group_mean_matrix(MXU_TILE),
            jnp.tile(q_gain, D_MODEL // HEAD_DIM).reshape(1, D_MODEL),
            jnp.tile(k_gain, kw // HEAD_DIM).reshape(1, kw)]
    if rotary:
        row_tab, col_tab = tab
        in_specs += [pl.BlockSpec((t // GRID_W, 3 * LANES), lambda i: (i, 0)),
                     _const_spec((t, 3 * LANES))]
        args += [row_tab, col_tab]
    return pl.pallas_call(
        functools.partial(_qkv_kernel, row, rotary, kw, q_scale),
        grid=(n // t,),
        in_specs=in_specs,
        out_specs=[pl.BlockSpec((t, D_MODEL), lambda i: (i, 0)),
                   pl.BlockSpec((t, kw), lambda i: (i, 0)),
                   pl.BlockSpec((t, vw), lambda i: (i, 0))],
        out_shape=[jax.ShapeDtypeStruct((n, D_MODEL), BF16),
                   jax.ShapeDtypeStruct((n, kw), BF16),
                   jax.ShapeDtypeStruct((n, vw), BF16)],
        compiler_params=_cparams(1),
        name="qkv_proj",
    )(*args)


DIFF_TQ = 256
DIFF_TK = 1280
DIFF_VT_ROWS = LANES + 16


def _diff_attn_kernel(tq, tk, n_tiles, n_chunks, src_rows, lam_init, q_ref, lamv_ref, sub_ref, *refs):
    n_src = len(src_rows)
    kv_refs = refs[:2 * n_src]
    o_ref, qq_ref, k_ref, vt_ref, s_ref, m_ref, acc_ref = refs[2 * n_src:]
    n_keys = n_chunks * tk
    lane = lax.broadcasted_iota(jnp.int32, (tq, LANES), 1)
    lv = lamv_ref[...]
    lam = (jnp.exp(jnp.sum(lv[0:1] * lv[1:2], axis=-1, keepdims=True))
           - jnp.exp(jnp.sum(lv[2:3] * lv[3:4], axis=-1, keepdims=True)) + lam_init)
    out_gain = sub_ref[...] * (1.0 - lam_init)

    off = 0
    for i_src, rows in enumerate(src_rows):
        ks_ref, vs_ref = kv_refs[2 * i_src], kv_refs[2 * i_src + 1]
        for r0 in range(0, rows, tk):
            r1 = min(r0 + tk, rows)
            k_ref[off + r0:off + r1, :] = ks_ref[r0:r1, :]
            vt_ref[0:LANES, off + r0:off + r1] = vs_ref[r0:r1, :].astype(F32).T.astype(BF16)
        off += rows
    vt_ref[LANES:, :] = jnp.ones((DIFF_VT_ROWS - LANES, n_keys), BF16)

    def rows_of(t):
        return pl.ds(t * tq, tq) if isinstance(t, int) else pl.ds(pl.multiple_of(t * tq, tq), tq)

    def stack_queries(t, slot):
        q = q_ref[rows_of(t), :]
        zero = jnp.zeros_like(q)
        qq_ref[slot, 0:tq, :] = jnp.where(lane < HEAD_DIM, q, zero)
        qq_ref[slot, tq:, :] = jnp.where(lane >= HEAD_DIM, q, zero)

    def scores(slot, c, buf):
        s_ref[buf] = lax.dot_general(k_ref[c * tk:(c + 1) * tk, :], qq_ref[slot],
                                     (((1,), (1,)), ((), ())), preferred_element_type=F32)

    def softmax_pv(c, buf):
        s = s_ref[buf]
        m_prev = m_ref[...]
        m_new = jnp.maximum(m_prev, jnp.max(s, axis=0, keepdims=True))
        alpha = jnp.exp2(m_prev - m_new)
        p = jnp.exp2(s - m_new).astype(BF16)
        pv = jnp.dot(vt_ref[:, c * tk:(c + 1) * tk], p, preferred_element_type=F32)
        acc_ref[...] = acc_ref[...] * alpha + pv
        m_ref[...] = m_new

    def tile(t, t_next, slot, first_buf):
        m_ref[...] = jnp.full(m_ref.shape, NEG_INF, F32)
        acc_ref[...] = jnp.zeros(acc_ref.shape, F32)
        for c in range(n_chunks):
            buf = (first_buf + c) % 2
            if c + 1 < n_chunks:
                scores(slot, c + 1, 1 - buf)
            elif t_next is not None:
                stack_queries(t_next, 1 - slot)
                scores(1 - slot, 0, 1 - buf)
            softmax_pv(c, buf)
        acc = acc_ref[...]
        o_t = acc[0:LANES, :] / acc[LANES:LANES + 1, :]
        d = (o_t[:, :tq] - lam * o_t[:, tq:]).T
        ms = jnp.mean(d * d, axis=-1, keepdims=True)
        o_ref[rows_of(t), :] = (d * lax.rsqrt(ms + NORM_EPS) * out_gain).astype(BF16)

    stack_queries(0, 0)
    scores(0, 0, 0)
    flip = n_chunks % 2
    if n_tiles % 2 == 0 and n_tiles > 2:
        def pair(tp, carry):
            t0 = 2 * tp
            tile(t0, t0 + 1, 0, 0)
            tile(t0 + 1, jnp.minimum(t0 + 2, n_tiles - 1), 1, flip)
            return carry
        lax.fori_loop(0, n_tiles // 2, pair, 0)
    else:
        for t in range(n_tiles):
            tile(t, t + 1 if t + 1 < n_tiles else None, t % 2, (t * flip) % 2)


def _diff_attn(q, kv_sources, lamv, subln, lam_init):
    nq = q.shape[0]
    tq = min(DIFF_TQ, nq)
    src_rows = tuple(k.shape[0] for k, _ in kv_sources)
    n_keys = sum(src_rows)
    tk = min(DIFF_TK, n_keys)
    n_chunks = n_keys // tk
    assert nq % tq == 0 and n_keys % tk == 0
    in_specs = [pl.BlockSpec((nq, LANES), lambda h: (0, h)),
                _const_spec((8, LANES)),
                _const_spec((1, LANES))]
    args = [q, lamv, subln.reshape(1, LANES)]
    for k, v in kv_sources:
        in_specs += [pl.BlockSpec((k.shape[0], LANES), lambda h: (0, h))] * 2
        args += [k, v]
    return pl.pallas_call(
        functools.partial(_diff_attn_kernel, tq, tk, nq // tq, n_chunks, src_rows, lam_init),
        grid=(DIFF_HEADS,),
        in_specs=in_specs,
        out_specs=pl.BlockSpec((nq, LANES), lambda h: (0, h)),
        out_shape=jax.ShapeDtypeStruct((nq, D_MODEL), BF16),
        scratch_shapes=[pltpu.VMEM((2, 2 * tq, LANES), BF16),
                        pltpu.VMEM((n_keys, LANES), BF16),
                        pltpu.VMEM((DIFF_VT_ROWS, n_keys), BF16),
                        pltpu.VMEM((2, tk, 2 * tq), F32),
                        pltpu.VMEM((1, 2 * tq), F32),
                        pltpu.VMEM((DIFF_VT_ROWS, 2 * tq), F32)],
        compiler_params=_cparams(1),
        name="diff_attn",
    )(*args)


def _win_attn_kernel(tq, n_lat, with_window, *refs):
    if with_window:
        (q_ref, kp_ref, kc_ref, kn_ref, vp_ref, vc_ref, vn_ref,
         kx_ref, vx_ref, sink_ref, o_ref, s_ref) = refs
    else:
        q_ref, kx_ref, vx_ref, sink_ref, o_ref, s_ref = refs
    i = pl.program_id(0)
    n_ctx = kx_ref.shape[0]
    sub = WINDOW if with_window else tq
    n_win = 3 * WINDOW if with_window else 0
    n_keys = n_win + n_ctx
    low = lax.broadcasted_iota(jnp.int32, (sub, LANES), 1) < HEAD_DIM
    ones = jnp.ones((n_keys, LANES), BF16)
    nt = (((1,), (1,)), ((), ()))
    units = [(sb, kvh) for sb in range(tq // sub) for kvh in range(WIN_KV_HEADS)]

    def keys_or_values(u, p_ref, c_ref, n_ref, x_ref):
        sb, kvh = units[u]
        ks = slice(kvh * LANES, (kvh + 1) * LANES)
        if not with_window:
            return x_ref[:, ks]
        span = jnp.concatenate([p_ref[:, ks], c_ref[:, ks], n_ref[:, ks]], axis=0)
        return jnp.concatenate([span[sb * WINDOW:sb * WINDOW + n_win], x_ref[:, ks]], axis=0)

    def scores(u, buf):
        sb, kvh = units[u]
        r0 = sb * sub
        qs = []
        for pr in range(WIN_GROUP // 2):
            blk = kvh * (WIN_GROUP // 2) + pr
            qp = q_ref[r0:r0 + sub, blk * LANES:(blk + 1) * LANES]
            zero = jnp.zeros_like(qp)
            qs += [jnp.where(low, qp, zero), jnp.where(low, zero, qp)]
        kk = keys_or_values(u, kp_ref, kc_ref, kn_ref, kx_ref) if with_window else \
            keys_or_values(u, None, None, None, kx_ref)
        s_ref[buf] = lax.dot_general(jnp.concatenate(qs, axis=0), kk, nt,
                                     preferred_element_type=F32)

    def softmax_pv(u, buf):
        sb, kvh = units[u]
        r0 = sb * sub
        s = s_ref[buf]
        if with_window:
            qpos = i * tq + r0 + lax.broadcasted_iota(jnp.int32, (sub, n_win), 0)
            kpos = i * tq + r0 - WINDOW + lax.broadcasted_iota(jnp.int32, (sub, n_win), 1)
            valid = (jnp.abs(kpos - qpos) <= WINDOW) & (kpos >= 0) & (kpos < n_lat)
            bias = jnp.where(valid, 0.0, NEG_INF)
            bias = jnp.concatenate([bias] * WIN_GROUP, axis=0)
            s = jnp.concatenate([s[:, :n_win] + bias, s[:, n_win:]], axis=1)
        heads = [kvh * WIN_GROUP + g for g in range(WIN_GROUP)]
        sink = jnp.concatenate(
            [jnp.broadcast_to(sink_ref[hd:hd + 1, :], (sub, LANES)) for hd in heads], axis=0)
        m = jnp.maximum(jnp.max(s, axis=-1, keepdims=True), sink)
        p = jnp.exp(s - jnp.tile(m, (1, n_keys // LANES)))
        vv = keys_or_values(u, vp_ref, vc_ref, vn_ref, vx_ref) if with_window else \
            keys_or_values(u, None, None, None, vx_ref)
        pv = jnp.dot(p.astype(BF16), jnp.concatenate([vv, ones], axis=1),
                     preferred_element_type=F32)
        den = pv[:, LANES:LANES + 1] + jnp.exp(sink[:, 0:1] - m[:, 0:1])
        o = pv[:, :LANES] / den
        for pr in range(WIN_GROUP // 2):
            blk = kvh * (WIN_GROUP // 2) + pr
            lo = o[(2 * pr) * sub:(2 * pr + 1) * sub]
            hi = o[(2 * pr + 1) * sub:(2 * pr + 2) * sub]
            o_ref[r0:r0 + sub, blk * LANES:(blk + 1) * LANES] = jnp.where(low, lo, hi).astype(BF16)

    scores(0, 0)
    for u in range(len(units)):
        if u + 1 < len(units):
            scores(u + 1, (u + 1) % 2)
        softmax_pv(u, u % 2)


def _win_attn(q, k, v, kx, vx, sink_rows, with_window):
    nq = q.shape[0]
    tq = min(256, nq)
    n_ctx = kx.shape[0]
    kvw = WIN_KV_HEADS * LANES
    in_specs = [pl.BlockSpec((tq, D_MODEL), lambda i: (i, 0))]
    args = [q]
    if with_window:
        r = tq // WINDOW
        last = nq // WINDOW - 1
        prev_spec = pl.BlockSpec((WINDOW, kvw), lambda i: (jnp.maximum(i * r - 1, 0), 0))
        cur_spec = pl.BlockSpec((tq, kvw), lambda i: (i, 0))
        next_spec = pl.BlockSpec((WINDOW, kvw), lambda i: (jnp.minimum((i + 1) * r, last), 0))
        in_specs += [prev_spec, cur_spec, next_spec] * 2
        args += [k, k, k, v, v, v]
    in_specs += [_const_spec((n_ctx, kvw)), _const_spec((n_ctx, kvw)),
                 _const_spec((WIN_Q_HEADS, LANES))]
    args += [kx, vx, sink_rows]
    return pl.pallas_call(
        functools.partial(_win_attn_kernel, tq, nq, with_window),
        grid=(nq // tq,),
        in_specs=in_specs,
        out_specs=pl.BlockSpec((tq, D_MODEL), lambda i: (i, 0)),
        out_shape=jax.ShapeDtypeStruct((nq, D_MODEL), BF16),
        scratch_shapes=[pltpu.VMEM((2, WIN_GROUP * (WINDOW if with_window else tq),
                                    (3 * WINDOW if with_window else 0) + n_ctx), F32)],
        compiler_params=_cparams(1),
        name="win_attn",
    )(*args)


_FFN_SPLIT = -(-(D_FF // MXU_TILE) // 2) * MXU_TILE
FFN_CHUNKS = ((0, _FFN_SPLIT), (_FFN_SPLIT, D_FF))
FFN_SUB_ROWS = 256
FFN_ROW_TILE = 1024


def _ffn_kernel(row, fused, *refs):
    if fused:
        x_ref, wo_ref, h_ref, mod_ref, n2_ref, wgu_ref, wd_ref, o_ref = refs
    else:
        h_ref, mod_ref, n2_ref, wgu_ref, wd_ref, o_ref = refs
    t = h_ref.shape[0]
    sub = min(FFN_SUB_ROWS, t)
    for sb in range(t // sub):
        rs = slice(sb * sub, (sb + 1) * sub)
        h = h_ref[rs, :]
        if fused:
            h = h + _mod(mod_ref, row, G1) * jnp.dot(x_ref[rs, :], wo_ref[...],
                                                     preferred_element_type=F32)
        a = _mod_norm(h, n2_ref[...], _mod(mod_ref, row, SC2), _mod(mod_ref, row, SH2)).astype(BF16)
        acc = None
        for lo, hi in FFN_CHUNKS:
            g = jnp.dot(a, wgu_ref[:, lo:hi], preferred_element_type=F32)
            u = jnp.dot(a, wgu_ref[:, D_FF + lo:D_FF + hi], preferred_element_type=F32)
            act = (_silu(g) * u).astype(BF16)
            part = jnp.dot(act, wd_ref[lo:hi, :], preferred_element_type=F32)
            acc = part if acc is None else acc + part
        o_ref[rs, :] = h + _mod(mod_ref, row, G2) * acc


def _ffn(h, mods, layer, row, n2, wgu, wd, attn=None):
    n = h.shape[0]
    t = FFN_ROW_TILE if n % FFN_ROW_TILE == 0 else _row_tile(n)
    row_spec = pl.BlockSpec((t, D_MODEL), lambda i: (i, 0))
    in_specs = [row_spec,
                pl.BlockSpec((None, 8, 6 * D_MODEL), lambda i: (layer, 0, 0)),
                _const_spec((1, D_MODEL)),
                _layer_spec((D_MODEL, 2 * D_FF), layer),
                _layer_spec((D_FF, D_MODEL), layer)]
    args = [h, mods, n2.reshape(1, D_MODEL), wgu, wd]
    if attn is not None:
        in_specs = [row_spec, _const_spec((D_MODEL, D_MODEL))] + in_specs
        args = list(attn) + args
    return pl.pallas_call(
        functools.partial(_ffn_kernel, row, attn is not None),
        grid=(n // t,),
        in_specs=in_specs,
        out_specs=row_spec,
        out_shape=jax.ShapeDtypeStruct((n, D_MODEL), F32),
        compiler_params=_cparams(1),
        name="ffn",
    )(*args)


def _in_proj_kernel(row, mode, *refs):
    if mode == "short":
        h_ref, mod_ref, n1_ref, w_ref, b_out, cu_out = refs
    else:
        h_ref, mod_ref, n1_ref, w_ref, bias_ref, glu_out = refs
    t = h_ref.shape[0]
    sub = min(QKV_SUB_ROWS, t)
    for sb in range(t // sub):
        rs = slice(sb * sub, (sb + 1) * sub)
        a = _mod_norm(h_ref[rs, :], n1_ref[...], _mod(mod_ref, row, SC1), _mod(mod_ref, row, SH1))
        y = jnp.dot(a.astype(BF16), w_ref[...], preferred_element_type=F32)
        if mode == "short":
            b_out[rs, :] = y[:, :D_MODEL]
            cu_out[rs, :] = y[:, D_MODEL:2 * D_MODEL] * y[:, 2 * D_MODEL:]
        else:
            y = y + bias_ref[...]
            g = y[:, D_MODEL:]
            glu_out[rs, :] = y[:, :D_MODEL] * (1.0 / (1.0 + jnp.exp(-g)))


def _in_proj(h, mods, layer, row, n1, w, mode, bias=None):
    n = h.shape[0]
    t = _row_tile(n)
    wtot = w.shape[1]
    in_specs = [pl.BlockSpec((t, D_MODEL), lambda i: (i, 0)),
                pl.BlockSpec((None, 8, 6 * D_MODEL), lambda i: (layer, 0, 0)),
                _const_spec((1, D_MODEL)),
                _const_spec((D_MODEL, wtot))]
    args = [h, mods, n1.reshape(1, D_MODEL), w]
    row_spec = pl.BlockSpec((t, D_MODEL), lambda i: (i, 0))
    row_shape = jax.ShapeDtypeStruct((n, D_MODEL), F32)
    if mode == "short":
        out_specs, out_shape = [row_spec, row_spec], [row_shape, row_shape]
    else:
        in_specs.append(_const_spec((1, wtot)))
        args.append(bias.reshape(1, wtot))
        out_specs, out_shape = row_spec, row_shape
    return pl.pallas_call(
        functools.partial(_in_proj_kernel, row, mode),
        grid=(n // t,),
        in_specs=in_specs,
        out_specs=out_specs,
        out_shape=out_shape,
        compiler_params=_cparams(1),
        name="in_proj_" + mode,
    )(*args)


CONV_ROW_BLOCK = 64
SUBLANES = 8


def _conv_shifts(taps):
    return [divmod(CONV_HALO - taps // 2 + k, SUBLANES) for k in range(taps)]


def _conv_rems(taps):
    return sorted({r for _, r in _conv_shifts(taps) if r})


def _conv_span(taps, t):
    return t + SUBLANES * max(a for a, _ in _conv_shifts(taps))


def _dwconv_tile(xs_ref, sh_ref, u_ref, prev_ref, cur_ref, next_ref, w_ref, taps, t):
    i = pl.program_id(0)
    last = pl.num_programs(0) - 1
    ncb = D_MODEL // LANES
    for cb in range(ncb):
        cs = slice(cb * LANES, (cb + 1) * LANES)
        xs_ref[cb, 0:CONV_HALO, :] = jnp.where(i > 0, prev_ref[:, cs], 0.0)
        xs_ref[cb, CONV_HALO:CONV_HALO + t, :] = cur_ref[:, cs]
        xs_ref[cb, CONV_HALO + t:, :] = jnp.where(i < last, next_ref[:, cs], 0.0)
    shifts = _conv_shifts(taps)
    rems = _conv_rems(taps)
    span = _conv_span(taps, t)
    for j, r in enumerate(rems):
        sh_ref[j] = xs_ref[:, r:r + span, :]

    def block(idx, carry):
        rb, cb = idx // ncb, idx % ncb
        row0 = pl.multiple_of(rb * CONV_ROW_BLOCK, CONV_ROW_BLOCK)
        accs = [None, None]
        for k, (a, r) in enumerate(shifts):
            rows = pl.ds(row0 + a * SUBLANES, CONV_ROW_BLOCK)
            x = xs_ref[cb, rows, :] if r == 0 else sh_ref[rems.index(r), cb, rows, :]
            term = x * w_ref[cb, k:k + 1, :]
            accs[k % 2] = term if accs[k % 2] is None else accs[k % 2] + term
        u_ref[cb, pl.ds(row0, CONV_ROW_BLOCK), :] = accs[0] + accs[1]
        return carry

    lax.fori_loop(0, (t // CONV_ROW_BLOCK) * ncb, block, 0)
    return jnp.concatenate([u_ref[cb] for cb in range(ncb)], axis=1)


def _conv_out_kernel(row, mode, taps, t, *refs):
    if mode == "short":
        (prev_ref, cur_ref, next_ref, cw_ref, b_ref, h_ref, mod_ref, w_ref, o_ref,
         xs_ref, sh_ref, u_ref) = refs
    else:
        (prev_ref, cur_ref, next_ref, cw_ref, dwb_ref, lng_ref, lnb_ref, h_ref, mod_ref,
         w_ref, pb_ref, o_ref, xs_ref, sh_ref, u_ref) = refs
    u = _dwconv_tile(xs_ref, sh_ref, u_ref, prev_ref, cur_ref, next_ref, cw_ref, taps, t)
    if mode == "short":
        y = jnp.dot((b_ref[...] * u).astype(BF16), w_ref[...], preferred_element_type=F32)
    else:
        u = u + dwb_ref[...]
        mu = jnp.mean(u, axis=-1, keepdims=True)
        uc = u - mu
        var = jnp.mean(uc * uc, axis=-1, keepdims=True)
        z = _silu(uc * lax.rsqrt(var + NORM_EPS) * lng_ref[...] + lnb_ref[...])
        y = jnp.dot(z.astype(BF16), w_ref[...], preferred_element_type=F32) + pb_ref[...]
    o_ref[...] = h_ref[...] + _mod(mod_ref, row, G1) * y


def _conv_out(x, h, mods, layer, row, conv_w, w, mode, extra):
    n = h.shape[0]
    t = _row_tile(n)
    taps = conv_w.shape[0]
    r = t // CONV_HALO
    last = n // CONV_HALO - 1
    taps_pad = -(-taps // 8) * 8
    ncb = D_MODEL // LANES
    cw = jnp.zeros((taps_pad, D_MODEL), F32).at[:taps].set(conv_w)
    cw = cw.reshape(taps_pad, ncb, LANES).transpose(1, 0, 2)
    row_spec = pl.BlockSpec((t, D_MODEL), lambda i: (i, 0))
    vec_spec = _const_spec((1, D_MODEL))
    in_specs = [pl.BlockSpec((CONV_HALO, D_MODEL), lambda i: (jnp.maximum(i * r - 1, 0), 0)),
                row_spec,
                pl.BlockSpec((CONV_HALO, D_MODEL), lambda i: (jnp.minimum((i + 1) * r, last), 0)),
                _const_spec((ncb, taps_pad, LANES))]
    args = [x, x, x, cw]
    mod_spec = pl.BlockSpec((None, 8, 6 * D_MODEL), lambda i: (layer, 0, 0))
    if mode == "short":
        (b_gate,) = extra
        in_specs += [row_spec, row_spec, mod_spec, _const_spec((D_MODEL, D_MODEL))]
        args += [b_gate, h, mods, w]
    else:
        dw_b, ln_g, ln_b, pw_b = extra
        in_specs += [vec_spec, vec_spec, vec_spec, row_spec, mod_spec,
                     _const_spec((D_MODEL, D_MODEL)), vec_spec]
        args += [dw_b.reshape(1, D_MODEL), ln_g.reshape(1, D_MODEL), ln_b.reshape(1, D_MODEL),
                 h, mods, w, pw_b.reshape(1, D_MODEL)]
    return pl.pallas_call(
        functools.partial(_conv_out_kernel, row, mode, taps, t),
        grid=(n // t,),
        in_specs=in_specs,
        out_specs=row_spec,
        out_shape=jax.ShapeDtypeStruct((n, D_MODEL), F32),
        scratch_shapes=[pltpu.VMEM((ncb, t + 2 * CONV_HALO, LANES), F32),
                        pltpu.VMEM((len(_conv_rems(taps)), ncb, _conv_span(taps, t), LANES), F32),
                        pltpu.VMEM((ncb, t, LANES), F32)],
        compiler_params=_cparams(1),
        name="conv_out_" + mode,
    )(*args)


def _diff_qkv_weight(w_qkv):
    w = w_qkv.astype(BF16)

    def head_major(cols):
        d_in = cols.shape[0]
        return cols.reshape(d_in, 2, DIFF_HEADS, HEAD_DIM).transpose(0, 2, 1, 3).reshape(d_in, D_MODEL)

    return jnp.concatenate([head_major(w[:, :D_MODEL]), head_major(w[:, D_MODEL:2 * D_MODEL]),
                            w[:, 2 * D_MODEL:]], axis=1)


def _win_qkv_weight(w_qkv):
    qw = WIN_Q_HEADS * HEAD_DIM
    kvw = WIN_KV_HEADS * HEAD_DIM
    w = w_qkv.astype(BF16)

    def twice(cols):
        d_in = cols.shape[0]
        heads = cols.reshape(d_in, WIN_KV_HEADS, 1, HEAD_DIM)
        return jnp.broadcast_to(heads, (d_in, WIN_KV_HEADS, 2, HEAD_DIM)).reshape(d_in, 2 * kvw)

    return jnp.concatenate([w[:, :qw], twice(w[:, qw:qw + kvw]), twice(w[:, qw + kvw:])], axis=1)


def kernel(x, c, ctx, c_ctx, ada_w, ada_b, norm1, norm2, ffn_w_gate_up, ffn_w_down, diff_w_qkv, diff_w_o, diff_q_norm, diff_k_norm, diff_lam_q1, diff_lam_k1, diff_lam_q2, diff_lam_k2, diff_subln, sc_w_in, sc_conv_w, sc_w_out, win_w_qkv, win_w_o, win_q_norm, win_k_norm, win_sink, cf_w_pw1, cf_b_pw1, cf_dw_w, cf_dw_b, cf_ln_g, cf_ln_b, cf_w_pw2, cf_b_pw2):
    n = x.shape[1]
    h, hc = x.reshape(n, D_MODEL), ctx.reshape(ctx.shape[1], D_MODEL)
    mods = _ada_mod(c, c_ctx, ada_w, ada_b)
    wgu, wd = ffn_w_gate_up.astype(BF16), ffn_w_down.astype(BF16)
    tab = _rope_tables(n, _row_tile(n))
    for i in range(DEPTH):
        kind, j = i % N_MIXERS, i // N_MIXERS
        with_ctx = i < DEPTH - 1
        attn_l = attn_c = None
        if kind == 0:
            lam_init = 0.8 - 0.6 * math.exp(-0.3 * i)
            w = _diff_qkv_weight(diff_w_qkv[j])
            w_o = diff_w_o[j].astype(BF16)
            lamv = jnp.zeros((8, LANES), F32).at[0:4, 0:HEAD_DIM].set(
                jnp.stack([diff_lam_q1[j], diff_lam_k1[j], diff_lam_q2[j], diff_lam_k2[j]]))
            q_scale = HEAD_DIM ** -0.5 * LOG2E
            proj = functools.partial(_qkv_proj, mods=mods, layer=i, n1=norm1[i], w=w,
                                     q_gain=diff_q_norm[j], k_gain=diff_k_norm[j],
                                     kw=D_MODEL, vw=D_MODEL, q_scale=q_scale)
            q_l, k_l, v_l = proj(h, row=0, tab=tab)
            q_c, k_c, v_c = proj(hc, row=1, tab=None)
            attn_l = (_diff_attn(q_l, [(k_l, v_l), (k_c, v_c)], lamv, diff_subln[j], lam_init), w_o)
            if with_ctx:
                attn_c = (_diff_attn(q_c, [(k_c, v_c)], lamv, diff_subln[j], lam_init), w_o)
        elif kind == 1:
            w_in = sc_w_in[j].astype(BF16)
            w_out = sc_w_out[j].astype(BF16)
            b_l, cu_l = _in_proj(h, mods, i, 0, norm1[i], w_in, "short")
            h = _conv_out(cu_l, h, mods, i, 0, sc_conv_w[j], w_out, "short", (b_l,))
            if with_ctx:
                b_c, cu_c = _in_proj(hc, mods, i, 1, norm1[i], w_in, "short")
                hc = _conv_out(cu_c, hc, mods, i, 1, sc_conv_w[j], w_out, "short", (b_c,))
        elif kind == 2:
            w = _win_qkv_weight(win_w_qkv[j])
            w_o = win_w_o[j].astype(BF16)
            kvw = WIN_KV_HEADS * LANES
            sink_rows = jnp.broadcast_to(win_sink[j][:, None], (WIN_Q_HEADS, LANES))
            proj = functools.partial(_qkv_proj, mods=mods, layer=i, n1=norm1[i], w=w,
                                     q_gain=win_q_norm[j], k_gain=win_k_norm[j],
                                     kw=kvw, vw=kvw, q_scale=HEAD_DIM ** -0.5)
            q_l, k_l, v_l = proj(h, row=0, tab=tab)
            q_c, k_c, v_c = proj(hc, row=1, tab=None)
            o_l = _win_attn(q_l, k_l, v_l, k_c, v_c, sink_rows, True)
            attn_l = (o_l, w_o)
            if with_ctx:
                attn_c = (_win_attn(q_c, None, None, k_c, v_c, sink_rows, False), w_o)
        else:
            w1 = cf_w_pw1[j].astype(BF16)
            w2 = cf_w_pw2[j].astype(BF16)
            extra = (cf_dw_b[j], cf_ln_g[j], cf_ln_b[j], cf_b_pw2[j])
            glu_l = _in_proj(h, mods, i, 0, norm1[i], w1, "glu", cf_b_pw1[j])
            h = _conv_out(glu_l, h, mods, i, 0, cf_dw_w[j], w2, "conf", extra)
            if with_ctx:
                glu_c = _in_proj(hc, mods, i, 1, norm1[i], w1, "glu", cf_b_pw1[j])
                hc = _conv_out(glu_c, hc, mods, i, 1, cf_dw_w[j], w2, "conf", extra)
        h = _ffn(h, mods, i, 0, norm2[i], wgu, wd, attn_l)
        if with_ctx:
            hc = _ffn(hc, mods, i, 1, norm2[i], wgu, wd, attn_c)
    return h.reshape(1, n, D_MODEL)
```

```python
import functools
import math

import jax
import jax.numpy as jnp
import numpy as np
from jax import lax
from jax.experimental import pallas as pl
from jax.experimental.pallas import tpu as pltpu

F32 = jnp.float32
BF16 = jnp.bfloat16

D_MODEL = 1024
DEPTH = 4
N_MIXERS = 4
GRID_W = 64
NORM_EPS = 1e-6
ROPE_THETA = 10000.0
HEAD_DIM = 64
NEG_INF = -1e30
DIFF_HEADS = D_MODEL // (2 * HEAD_DIM)
SHORT_CONV_W = 3
WIN_Q_HEADS = D_MODEL // HEAD_DIM
WIN_KV_HEADS = 4
WIN_GROUP = WIN_Q_HEADS // WIN_KV_HEADS
WINDOW = 128
CONF_CONV_W = 31
D_FF = -(-(8 * D_MODEL) // (3 * 256)) * 256

LANES = 128
MXU_TILE = 256
CONV_HALO = 16
VMEM_LIMIT_BYTES = 56 * 1024 * 1024

LOG2E = math.log2(math.e)
SH1, SC1, G1, SH2, SC2, G2 = range(6)


def _cparams(n_axes):
    return pltpu.CompilerParams(dimension_semantics=("parallel",) * n_axes,
                                vmem_limit_bytes=VMEM_LIMIT_BYTES)


def _const_spec(shape):
    nd = len(shape)
    return pl.BlockSpec(shape, lambda *_: (0,) * nd, pipeline_mode=pl.Buffered(1))


def _layer_spec(shape, layer):
    nd = len(shape)
    return pl.BlockSpec((None,) + tuple(shape), lambda *_: (layer,) + (0,) * nd,
                        pipeline_mode=pl.Buffered(1))


def _mod(mod_ref, row, idx):
    return mod_ref[row:row + 1, idx * D_MODEL:(idx + 1) * D_MODEL]


def _mod_norm(h, gain, scale, shift):
    ms = jnp.mean(h * h, axis=-1, keepdims=True)
    return (h * lax.rsqrt(ms + NORM_EPS)) * (gain * (1.0 + scale)) + shift


def _silu(x):
    return x * (1.0 / (1.0 + jnp.exp(-x)))


def _row_tile(n):
    return 512 if n % 512 == 0 else n


def _ada_kernel(act_ref, w_ref, b_ref, o_ref):
    a = _silu(act_ref[...])
    o_ref[...] = jnp.dot(a.astype(BF16), w_ref[...].astype(BF16),
                         preferred_element_type=F32) + b_ref[...]


def _ada_mod(c, c_ctx, ada_w, ada_b):
    act = jnp.zeros((8, D_MODEL), F32).at[0].set(c[0]).at[1].set(c_ctx)
    cols = 1536
    nc = 6 * D_MODEL // cols
    return pl.pallas_call(
        _ada_kernel,
        grid=(DEPTH, nc),
        in_specs=[pl.BlockSpec((8, D_MODEL), lambda i, j: (0, 0)),
                  pl.BlockSpec((None, D_MODEL, cols), lambda i, j: (i, 0, j)),
                  pl.BlockSpec((None, 1, cols), lambda i, j: (i, 0, j))],
        out_specs=pl.BlockSpec((None, 8, cols), lambda i, j: (i, 0, j)),
        out_shape=jax.ShapeDtypeStruct((DEPTH, 8, 6 * D_MODEL), F32),
        compiler_params=_cparams(2),
        name="ada_mod",
    )(act, ada_w, ada_b.reshape(DEPTH, 1, 6 * D_MODEL))


def _group_mean_matrix(width):
    g = np.arange(width) // HEAD_DIM
    return jnp.asarray((g[:, None] == g[None, :]).astype(np.float32) / HEAD_DIM, BF16)


def _rope_tables(n, t):
    n_freq = HEAD_DIM // 4
    inv_freq = ROPE_THETA ** (-jnp.arange(n_freq, dtype=F32) / n_freq)
    reps = LANES // HEAD_DIM

    def tables(count, is_row):
        ang = jnp.arange(count, dtype=F32)[:, None] * inv_freq[None, :]
        cos, sin, zero = jnp.cos(ang), jnp.sin(ang), jnp.zeros_like(ang)
        halves = [[cos, cos], [-sin, zero], [zero, sin]]
        out = []
        for first, second in halves:
            pair = [first, second, zero, zero] if is_row else [zero, zero, first, second]
            out.append(jnp.tile(jnp.concatenate(pair, axis=-1), (1, reps)))
        return jnp.concatenate(out, axis=-1)

    return tables(n // GRID_W, True), jnp.tile(tables(GRID_W, False), (t // GRID_W, 1))


def _rope(x, tab):
    w = x.shape[-1]
    reps = w // LANES
    cos = jnp.tile(tab[:, 0:LANES], (1, reps))
    sin_up = jnp.tile(tab[:, LANES:2 * LANES], (1, reps))
    sin_dn = jnp.tile(tab[:, 2 * LANES:3 * LANES], (1, reps))
    quarter = HEAD_DIM // 4
    return (x * cos + pltpu.roll(x, w - quarter, 1) * sin_up
            + pltpu.roll(x, quarter, 1) * sin_dn)


QKV_SUB_ROWS = 256


def _qkv_kernel(row, rotary, kw, q_scale, *refs):
    if rotary:
        (h_ref, mod_ref, n1_ref, w_ref, gm_ref, qg_ref, kg_ref, rt_ref, ct_ref,
         q_ref, k_ref, v_ref) = refs
    else:
        h_ref, mod_ref, n1_ref, w_ref, gm_ref, qg_ref, kg_ref, q_ref, k_ref, v_ref = refs
    gm = gm_ref[...]

    def head_norm(x, gain, width):
        x2 = (x * x).astype(BF16)
        ms = jnp.concatenate(
            [jnp.dot(x2[:, c:c + MXU_TILE], gm, preferred_element_type=F32)
             for c in range(0, width, MXU_TILE)], axis=1)
        return x * lax.rsqrt(ms + NORM_EPS) * gain

    t = h_ref.shape[0]
    sub = min(QKV_SUB_ROWS, t)
    for sb in range(t // sub):
        rs = slice(sb * sub, (sb + 1) * sub)
        a = _mod_norm(h_ref[rs, :], n1_ref[...], _mod(mod_ref, row, SC1), _mod(mod_ref, row, SH1))
        y = jnp.dot(a.astype(BF16), w_ref[...], preferred_element_type=F32)
        q = head_norm(y[:, :D_MODEL], qg_ref[...], D_MODEL)
        k = head_norm(y[:, D_MODEL:D_MODEL + kw], kg_ref[...], kw)
        if rotary:
            g0 = sb * sub // GRID_W
            row_part = jnp.concatenate(
                [jnp.broadcast_to(rt_ref[g0 + r:g0 + r + 1, :], (GRID_W, 3 * LANES))
                 for r in range(sub // GRID_W)], axis=0)
            tab = row_part + ct_ref[rs, :]
            q, k = _rope(q, tab), _rope(k, tab)
        q_ref[rs, :] = (q * q_scale).astype(BF16)
        k_ref[rs, :] = k.astype(BF16)
        v_ref[rs, :] = y[:, D_MODEL + kw:].astype(BF16)


def _qkv_proj(h, mods, layer, row, n1, w, q_gain, k_gain, kw, vw, q_scale, tab):
    n = h.shape[0]
    t = _row_tile(n)
    rotary = tab is not None
    wtot = D_MODEL + kw + vw
    in_specs = [pl.BlockSpec((t, D_MODEL), lambda i: (i, 0)),
                pl.BlockSpec((None, 8, 6 * D_MODEL), lambda i: (layer, 0, 0)),
                _const_spec((1, D_MODEL)),
                _const_spec((D_MODEL, wtot)),
                _const_spec((MXU_TILE, MXU_TILE)),
                _const_spec((1, D_MODEL)),
                _const_spec((1, kw))]
    args = [h, mods, n1.reshape(1, D_MODEL), w, _group_mean_matrix(MXU_TILE),
            jnp.tile(q_gain, D_MODEL // HEAD_DIM).reshape(1, D_MODEL),
            jnp.tile(k_gain, kw // HEAD_DIM).reshape(1, kw)]
    if rotary:
        row_tab, col_tab = tab
        in_specs += [pl.BlockSpec((t // GRID_W, 3 * LANES), lambda i: (i, 0)),
                     _const_spec((t, 3 * LANES))]
        args += [row_tab, col_tab]
    return pl.pallas_call(
        functools.partial(_qkv_kernel, row, rotary, kw, q_scale),
        grid=(n // t,),
        in_specs=in_specs,
        out_specs=[pl.BlockSpec((t, D_MODEL), lambda i: (i, 0)),
                   pl.BlockSpec((t, kw), lambda i: (i, 0)),
                   pl.BlockSpec((t, vw), lambda i: (i, 0))],
        out_shape=[jax.ShapeDtypeStruct((n, D_MODEL), BF16),
                   jax.ShapeDtypeStruct((n, kw), BF16),
                   jax.ShapeDtypeStruct((n, vw), BF16)],
        compiler_params=_cparams(1),
        name="qkv_proj",
    )(*args)


DIFF_TQ = 128
DIFF_TK = 1280
DIFF_VT_ROWS = LANES + 16


def _diff_attn_kernel(tq, tk, n_tiles, n_chunks, src_rows, lam_init, q_ref, lamv_ref, sub_ref, *refs):
    n_src = len(src_rows)
    kv_refs = refs[:2 * n_src]
    o_ref, qq_ref, k_ref, vt_ref, s_ref, m_ref, acc_ref = refs[2 * n_src:]
    n_keys = n_chunks * tk
    lane = lax.broadcasted_iota(jnp.int32, (tq, LANES), 1)
    lv = lamv_ref[...]
    lam = (jnp.exp(jnp.sum(lv[0:1] * lv[1:2], axis=-1, keepdims=True))
           - jnp.exp(jnp.sum(lv[2:3] * lv[3:4], axis=-1, keepdims=True)) + lam_init)
    out_gain = sub_ref[...] * (1.0 - lam_init)

    off = 0
    for i_src, rows in enumerate(src_rows):
        ks_ref, vs_ref = kv_refs[2 * i_src], kv_refs[2 * i_src + 1]
        for r0 in range(0, rows, tk):
            r1 = min(r0 + tk, rows)
            k_ref[off + r0:off + r1, :] = ks_ref[r0:r1, :]
            vt_ref[0:LANES, off + r0:off + r1] = vs_ref[r0:r1, :].astype(F32).T.astype(BF16)
        off += rows
    vt_ref[LANES:, :] = jnp.ones((DIFF_VT_ROWS - LANES, n_keys), BF16)

    def rows_of(t):
        return pl.ds(t * tq, tq) if isinstance(t, int) else pl.ds(pl.multiple_of(t * tq, tq), tq)

    def stack_queries(t, slot):
        q = q_ref[rows_of(t), :]
        zero = jnp.zeros_like(q)
        qq_ref[slot, 0:tq, :] = jnp.where(lane < HEAD_DIM, q, zero)
        qq_ref[slot, tq:, :] = jnp.where(lane >= HEAD_DIM, q, zero)

    def scores(slot, c, buf):
        s_ref[buf] = lax.dot_general(k_ref[c * tk:(c + 1) * tk, :], qq_ref[slot],
                                     (((1,), (1,)), ((), ())), preferred_element_type=F32)

    def softmax_pv(c, buf):
        s = s_ref[buf]
        m_prev = m_ref[...]
        m_new = jnp.maximum(m_prev, jnp.max(s, axis=0, keepdims=True))
        alpha = jnp.exp2(m_prev - m_new)
        p = jnp.exp2(s - m_new).astype(BF16)
        pv = jnp.dot(vt_ref[:, c * tk:(c + 1) * tk], p, preferred_element_type=F32)
        acc_ref[...] = acc_ref[...] * alpha + pv
        m_ref[...] = m_new

    def tile(t, t_next, slot, first_buf):
        m_ref[...] = jnp.full(m_ref.shape, NEG_INF, F32)
        acc_ref[...] = jnp.zeros(acc_ref.shape, F32)
        for c in range(n_chunks):
            buf = (first_buf + c) % 2
            if c + 1 < n_chunks:
                scores(slot, c + 1, 1 - buf)
            elif t_next is not None:
                stack_queries(t_next, 1 - slot)
                scores(1 - slot, 0, 1 - buf)
            softmax_pv(c, buf)
        acc = acc_ref[...]
        o_t = acc[0:LANES, :] / acc[LANES:LANES + 1, :]
        d = (o_t[:, :tq] - lam * o_t[:, tq:]).T
        ms = jnp.mean(d * d, axis=-1, keepdims=True)
        o_ref[rows_of(t), :] = (d * lax.rsqrt(ms + NORM_EPS) * out_gain).astype(BF16)

    stack_queries(0, 0)
    scores(0, 0, 0)
    flip = n_chunks % 2
    if n_tiles % 2 == 0 and n_tiles > 2:
        def pair(tp, carry):
            t0 = 2 * tp
            tile(t0, t0 + 1, 0, 0)
            tile(t0 + 1, jnp.minimum(t0 + 2, n_tiles - 1), 1, flip)
            return carry
        lax.fori_loop(0, n_tiles // 2, pair, 0)
    else:
        for t in range(n_tiles):
            tile(t, t + 1 if t + 1 < n_tiles else None, t % 2, (t * flip) % 2)


def _diff_attn(q, kv_sources, lamv, subln, lam_init):
    nq = q.shape[0]
    tq = min(DIFF_TQ, nq)
    src_rows = tuple(k.shape[0] for k, _ in kv_sources)
    n_keys = sum(src_rows)
    tk = min(DIFF_TK, n_keys)
    n_chunks = n_keys // tk
    assert nq % tq == 0 and n_keys % tk == 0
    in_specs = [pl.BlockSpec((nq, LANES), lambda h: (0, h)),
                _const_spec((8, LANES)),
                _const_spec((1, LANES))]
    args = [q, lamv, subln.reshape(1, LANES)]
    for k, v in kv_sources:
        in_specs += [pl.BlockSpec((k.shape[0], LANES), lambda h: (0, h))] * 2
        args += [k, v]
    return pl.pallas_call(
        functools.partial(_diff_attn_kernel, tq, tk, nq // tq, n_chunks, src_rows, lam_init),
        grid=(DIFF_HEADS,),
        in_specs=in_specs,
        out_specs=pl.BlockSpec((nq, LANES), lambda h: (0, h)),
        out_shape=jax.ShapeDtypeStruct((nq, D_MODEL), BF16),
        scratch_shapes=[pltpu.VMEM((2, 2 * tq, LANES), BF16),
                        pltpu.VMEM((n_keys, LANES), BF16),
                        pltpu.VMEM((DIFF_VT_ROWS, n_keys), BF16),
                        pltpu.VMEM((2, tk, 2 * tq), F32),
                        pltpu.VMEM((1, 2 * tq), F32),
                        pltpu.VMEM((DIFF_VT_ROWS, 2 * tq), F32)],
        compiler_params=_cparams(1),
        name="diff_attn",
    )(*args)


def _win_attn_kernel(tq, n_lat, with_window, *refs):
    if with_window:
        (q_ref, kp_ref, kc_ref, kn_ref, vp_ref, vc_ref, vn_ref,
         kx_ref, vx_ref, sink_ref, o_ref, s_ref) = refs
    else:
        q_ref, kx_ref, vx_ref, sink_ref, o_ref, s_ref = refs
    i = pl.program_id(0)
    n_ctx = kx_ref.shape[0]
    sub = WINDOW if with_window else tq
    n_win = 3 * WINDOW if with_window else 0
    n_keys = n_win + n_ctx
    low = lax.broadcasted_iota(jnp.int32, (sub, LANES), 1) < HEAD_DIM
    ones = jnp.ones((n_keys, LANES), BF16)
    nt = (((1,), (1,)), ((), ()))
    units = [(sb, kvh) for sb in range(tq // sub) for kvh in range(WIN_KV_HEADS)]

    def keys_or_values(u, p_ref, c_ref, n_ref, x_ref):
        sb, kvh = units[u]
        ks = slice(kvh * LANES, (kvh + 1) * LANES)
        if not with_window:
            return x_ref[:, ks]
        span = jnp.concatenate([p_ref[:, ks], c_ref[:, ks], n_ref[:, ks]], axis=0)
        return jnp.concatenate([span[sb * WINDOW:sb * WINDOW + n_win], x_ref[:, ks]], axis=0)

    def scores(u, buf):
        sb, kvh = units[u]
        r0 = sb * sub
        qs = []
        for pr in range(WIN_GROUP // 2):
            blk = kvh * (WIN_GROUP // 2) + pr
            qp = q_ref[r0:r0 + sub, blk * LANES:(blk + 1) * LANES]
            zero = jnp.zeros_like(qp)
            qs += [jnp.where(low, qp, zero), jnp.where(low, zero, qp)]
        kk = keys_or_values(u, kp_ref, kc_ref, kn_ref, kx_ref) if with_window else \
            keys_or_values(u, None, None, None, kx_ref)
        s_ref[buf] = lax.dot_general(jnp.concatenate(qs, axis=0), kk, nt,
                                     preferred_element_type=F32)

    def softmax_pv(u, buf):
        sb, kvh = units[u]
        r0 = sb * sub
        s = s_ref[buf]
        if with_window:
            qpos = i * tq + r0 + lax.broadcasted_iota(jnp.int32, (sub, n_win), 0)
            kpos = i * tq + r0 - WINDOW + lax.broadcasted_iota(jnp.int32, (sub, n_win), 1)
            valid = (jnp.abs(kpos - qpos) <= WINDOW) & (kpos >= 0) & (kpos < n_lat)
            bias = jnp.where(valid, 0.0, NEG_INF)
            bias = jnp.concatenate([bias] * WIN_GROUP, axis=0)
            s = jnp.concatenate([s[:, :n_win] + bias, s[:, n_win:]], axis=1)
        heads = [kvh * WIN_GROUP + g for g in range(WIN_GROUP)]
        sink = jnp.concatenate(
            [jnp.broadcast_to(sink_ref[hd:hd + 1, :], (sub, LANES)) for hd in heads], axis=0)
        m = jnp.maximum(jnp.max(s, axis=-1, keepdims=True), sink)
        p = jnp.exp(s - jnp.tile(m, (1, n_keys // LANES)))
        vv = keys_or_values(u, vp_ref, vc_ref, vn_ref, vx_ref) if with_window else \
            keys_or_values(u, None, None, None, vx_ref)
        pv = jnp.dot(p.astype(BF16), jnp.concatenate([vv, ones], axis=1),
                     preferred_element_type=F32)
        den = pv[:, LANES:LANES + 1] + jnp.exp(sink[:, 0:1] - m[:, 0:1])
        o = pv[:, :LANES] / den
        for pr in range(WIN_GROUP // 2):
            blk = kvh * (WIN_GROUP // 2) + pr
            lo = o[(2 * pr) * sub:(2 * pr + 1) * sub]
            hi = o[(2 * pr + 1) * sub:(2 * pr + 2) * sub]
            o_ref[r0:r0 + sub, blk * LANES:(blk + 1) * LANES] = jnp.where(low, lo, hi).astype(BF16)

    scores(0, 0)
    for u in range(len(units)):
        if u + 1 < len(units):
            scores(u + 1, (u + 1) % 2)
        softmax_pv(u, u % 2)


def _win_attn(q, k, v, kx, vx, sink_rows, with_window):
    nq = q.shape[0]
    tq = min(256, nq)
    n_ctx = kx.shape[0]
    kvw = WIN_KV_HEADS * LANES
    in_specs = [pl.BlockSpec((tq, D_MODEL), lambda i: (i, 0))]
    args = [q]
    if with_window:
        r = tq // WINDOW
        last = nq // WINDOW - 1
        prev_spec = pl.BlockSpec((WINDOW, kvw), lambda i: (jnp.maximum(i * r - 1, 0), 0))
        cur_spec = pl.BlockSpec((tq, kvw), lambda i: (i, 0))
        next_spec = pl.BlockSpec((WINDOW, kvw), lambda i: (jnp.minimum((i + 1) * r, last), 0))
        in_specs += [prev_spec, cur_spec, next_spec] * 2
        args += [k, k, k, v, v, v]
    in_specs += [_const_spec((n_ctx, kvw)), _const_spec((n_ctx, kvw)),
                 _const_spec((WIN_Q_HEADS, LANES))]
    args += [kx, vx, sink_rows]
    return pl.pallas_call(
        functools.partial(_win_attn_kernel, tq, nq, with_window),
        grid=(nq // tq,),
        in_specs=in_specs,
        out_specs=pl.BlockSpec((tq, D_MODEL), lambda i: (i, 0)),
        out_shape=jax.ShapeDtypeStruct((nq, D_MODEL), BF16),
        scratch_shapes=[pltpu.VMEM((2, WIN_GROUP * (WINDOW if with_window else tq),
                                    (3 * WINDOW if with_window else 0) + n_ctx), F32)],
        compiler_params=_cparams(1),
        name="win_attn",
    )(*args)


_FFN_SPLIT = -(-(D_FF // MXU_TILE) // 2) * MXU_TILE
FFN_CHUNKS = ((0, _FFN_SPLIT), (_FFN_SPLIT, D_FF))
FFN_SUB_ROWS = 256
FFN_ROW_TILE = 1024


def _ffn_kernel(row, fused, *refs):
    if fused:
        x_ref, wo_ref, h_ref, mod_ref, n2_ref, wgu_ref, wd_ref, o_ref = refs
    else:
        h_ref, mod_ref, n2_ref, wgu_ref, wd_ref, o_ref = refs
    t = h_ref.shape[0]
    sub = min(FFN_SUB_ROWS, t)
    for sb in range(t // sub):
        rs = slice(sb * sub, (sb + 1) * sub)
        h = h_ref[rs, :]
        if fused:
            h = h + _mod(mod_ref, row, G1) * jnp.dot(x_ref[rs, :], wo_ref[...],
                                                     preferred_element_type=F32)
        a = _mod_norm(h, n2_ref[...], _mod(mod_ref, row, SC2), _mod(mod_ref, row, SH2)).astype(BF16)
        acc = None
        for lo, hi in FFN_CHUNKS:
            g = jnp.dot(a, wgu_ref[:, lo:hi], preferred_element_type=F32)
            u = jnp.dot(a, wgu_ref[:, D_FF + lo:D_FF + hi], preferred_element_type=F32)
            act = (_silu(g) * u).astype(BF16)
            part = jnp.dot(act, wd_ref[lo:hi, :], preferred_element_type=F32)
            acc = part if acc is None else acc + part
        o_ref[rs, :] = h + _mod(mod_ref, row, G2) * acc


def _ffn(h, mods, layer, row, n2, wgu, wd, attn=None):
    n = h.shape[0]
    t = FFN_ROW_TILE if n % FFN_ROW_TILE == 0 else _row_tile(n)
    row_spec = pl.BlockSpec((t, D_MODEL), lambda i: (i, 0))
    in_specs = [row_spec,
                pl.BlockSpec((None, 8, 6 * D_MODEL), lambda i: (layer, 0, 0)),
                _const_spec((1, D_MODEL)),
                _layer_spec((D_MODEL, 2 * D_FF), layer),
                _layer_spec((D_FF, D_MODEL), layer)]
    args = [h, mods, n2.reshape(1, D_MODEL), wgu, wd]
    if attn is not None:
        in_specs = [row_spec, _const_spec((D_MODEL, D_MODEL))] + in_specs
        args = list(attn) + args
    return pl.pallas_call(
        functools.partial(_ffn_kernel, row, attn is not None),
        grid=(n // t,),
        in_specs=in_specs,
        out_specs=row_spec,
        out_shape=jax.ShapeDtypeStruct((n, D_MODEL), F32),
        compiler_params=_cparams(1),
        name="ffn",
    )(*args)


def _in_proj_kernel(row, mode, *refs):
    if mode == "short":
        h_ref, mod_ref, n1_ref, w_ref, b_out, cu_out = refs
    else:
        h_ref, mod_ref, n1_ref, w_ref, bias_ref, glu_out = refs
    t = h_ref.shape[0]
    sub = min(QKV_SUB_ROWS, t)
    for sb in range(t // sub):
        rs = slice(sb * sub, (sb + 1) * sub)
        a = _mod_norm(h_ref[rs, :], n1_ref[...], _mod(mod_ref, row, SC1), _mod(mod_ref, row, SH1))
        y = jnp.dot(a.astype(BF16), w_ref[...], preferred_element_type=F32)
        if mode == "short":
            b_out[rs, :] = y[:, :D_MODEL]
            cu_out[rs, :] = y[:, D_MODEL:2 * D_MODEL] * y[:, 2 * D_MODEL:]
        else:
            y = y + bias_ref[...]
            g = y[:, D_MODEL:]
            glu_out[rs, :] = y[:, :D_MODEL] * (1.0 / (1.0 + jnp.exp(-g)))


def _in_proj(h, mods, layer, row, n1, w, mode, bias=None):
    n = h.shape[0]
    t = _row_tile(n)
    wtot = w.shape[1]
    in_specs = [pl.BlockSpec((t, D_MODEL), lambda i: (i, 0)),
                pl.BlockSpec((None, 8, 6 * D_MODEL), lambda i: (layer, 0, 0)),
                _const_spec((1, D_MODEL)),
                _const_spec((D_MODEL, wtot))]
    args = [h, mods, n1.reshape(1, D_MODEL), w]
    row_spec = pl.BlockSpec((t, D_MODEL), lambda i: (i, 0))
    row_shape = jax.ShapeDtypeStruct((n, D_MODEL), F32)
    if mode == "short":
        out_specs, out_shape = [row_spec, row_spec], [row_shape, row_shape]
    else:
        in_specs.append(_const_spec((1, wtot)))
        args.append(bias.reshape(1, wtot))
        out_specs, out_shape = row_spec, row_shape
    return pl.pallas_call(
        functools.partial(_in_proj_kernel, row, mode),
        grid=(n // t,),
        in_specs=in_specs,
        out_specs=out_specs,
        out_shape=out_shape,
        compiler_params=_cparams(1),
        name="in_proj_" + mode,
    )(*args)


CONV_ROW_BLOCK = 64
SUBLANES = 8


def _conv_shifts(taps):
    return [divmod(CONV_HALO - taps // 2 + k, SUBLANES) for k in range(taps)]


def _conv_rems(taps):
    return sorted({r for _, r in _conv_shifts(taps) if r})


def _conv_span(taps, t):
    return t + SUBLANES * max(a for a, _ in _conv_shifts(taps))


def _dwconv_tile(xs_ref, sh_ref, u_ref, prev_ref, cur_ref, next_ref, w_ref, taps, t):
    i = pl.program_id(0)
    last = pl.num_programs(0) - 1
    ncb = D_MODEL // LANES
    for cb in range(ncb):
        cs = slice(cb * LANES, (cb + 1) * LANES)
        xs_ref[cb, 0:CONV_HALO, :] = jnp.where(i > 0, prev_ref[:, cs], 0.0)
        xs_ref[cb, CONV_HALO:CONV_HALO + t, :] = cur_ref[:, cs]
        xs_ref[cb, CONV_HALO + t:, :] = jnp.where(i < last, next_ref[:, cs], 0.0)
    shifts = _conv_shifts(taps)
    rems = _conv_rems(taps)
    span = _conv_span(taps, t)
    for j, r in enumerate(rems):
        sh_ref[j] = xs_ref[:, r:r + span, :]

    def block(idx, carry):
        rb, cb = idx // ncb, idx % ncb
        row0 = pl.multiple_of(rb * CONV_ROW_BLOCK, CONV_ROW_BLOCK)
        accs = [None, None]
        for k, (a, r) in enumerate(shifts):
            rows = pl.ds(row0 + a * SUBLANES, CONV_ROW_BLOCK)
            x = xs_ref[cb, rows, :] if r == 0 else sh_ref[rems.index(r), cb, rows, :]
            term = x * w_ref[cb, k:k + 1, :]
            accs[k % 2] = term if accs[k % 2] is None else accs[k % 2] + term
        u_ref[cb, pl.ds(row0, CONV_ROW_BLOCK), :] = accs[0] + accs[1]
        return carry

    lax.fori_loop(0, (t // CONV_ROW_BLOCK) * ncb, block, 0)
    return jnp.concatenate([u_ref[cb] for cb in range(ncb)], axis=1)


def _conv_out_kernel(row, mode, taps, t, *refs):
    if mode == "short":
        (prev_ref, cur_ref, next_ref, cw_ref, b_ref, h_ref, mod_ref, w_ref, o_ref,
         xs_ref, sh_ref, u_ref) = refs
    else:
        (prev_ref, cur_ref, next_ref, cw_ref, dwb_ref, lng_ref, lnb_ref, h_ref, mod_ref,
         w_ref, pb_ref, o_ref, xs_ref, sh_ref, u_ref) = refs
    u = _dwconv_tile(xs_ref, sh_ref, u_ref, prev_ref, cur_ref, next_ref, cw_ref, taps, t)
    if mode == "short":
        y = jnp.dot((b_ref[...] * u).astype(BF16), w_ref[...], preferred_element_type=F32)
    else:
        u = u + dwb_ref[...]
        mu = jnp.mean(u, axis=-1, keepdims=True)
        uc = u - mu
        var = jnp.mean(uc * uc, axis=-1, keepdims=True)
        z = _silu(uc * lax.rsqrt(var + NORM_EPS) * lng_ref[...] + lnb_ref[...])
        y = jnp.dot(z.astype(BF16), w_ref[...], preferred_element_type=F32) + pb_ref[...]
    o_ref[...] = h_ref[...] + _mod(mod_ref, row, G1) * y


def _conv_out(x, h, mods, layer, row, conv_w, w, mode, extra):
    n = h.shape[0]
    t = _row_tile(n)
    taps = conv_w.shape[0]
    r = t // CONV_HALO
    last = n // CONV_HALO - 1
    taps_pad = -(-taps // 8) * 8
    ncb = D_MODEL // LANES
    cw = jnp.zeros((taps_pad, D_MODEL), F32).at[:taps].set(conv_w)
    cw = cw.reshape(taps_pad, ncb, LANES).transpose(1, 0, 2)
    row_spec = pl.BlockSpec((t, D_MODEL), lambda i: (i, 0))
    vec_spec = _const_spec((1, D_MODEL))
    in_specs = [pl.BlockSpec((CONV_HALO, D_MODEL), lambda i: (jnp.maximum(i * r - 1, 0), 0)),
                row_spec,
                pl.BlockSpec((CONV_HALO, D_MODEL), lambda i: (jnp.minimum((i + 1) * r, last), 0)),
                _const_spec((ncb, taps_pad, LANES))]
    args = [x, x, x, cw]
    mod_spec = pl.BlockSpec((None, 8, 6 * D_MODEL), lambda i: (layer, 0, 0))
    if mode == "short":
        (b_gate,) = extra
        in_specs += [row_spec, row_spec, mod_spec, _const_spec((D_MODEL, D_MODEL))]
        args += [b_gate, h, mods, w]
    else:
        dw_b, ln_g, ln_b, pw_b = extra
        in_specs += [vec_spec, vec_spec, vec_spec, row_spec, mod_spec,
                     _const_spec((D_MODEL, D_MODEL)), vec_spec]
        args += [dw_b.reshape(1, D_MODEL), ln_g.reshape(1, D_MODEL), ln_b.reshape(1, D_MODEL),
                 h, mods, w, pw_b.reshape(1, D_MODEL)]
    return pl.pallas_call(
        functools.partial(_conv_out_kernel, row, mode, taps, t),
        grid=(n // t,),
        in_specs=in_specs,
        out_specs=row_spec,
        out_shape=jax.ShapeDtypeStruct((n, D_MODEL), F32),
        scratch_shapes=[pltpu.VMEM((ncb, t + 2 * CONV_HALO, LANES), F32),
                        pltpu.VMEM((len(_conv_rems(taps)), ncb, _conv_span(taps, t), LANES), F32),
                        pltpu.VMEM((ncb, t, LANES), F32)],
        compiler_params=_cparams(1),
        name="conv_out_" + mode,
    )(*args)


def _diff_qkv_weight(w_qkv):
    w = w_qkv.astype(BF16)

    def head_major(cols):
        d_in = cols.shape[0]
        return cols.reshape(d_in, 2, DIFF_HEADS, HEAD_DIM).transpose(0, 2, 1, 3).reshape(d_in, D_MODEL)

    return jnp.concatenate([head_major(w[:, :D_MODEL]), head_major(w[:, D_MODEL:2 * D_MODEL]),
                            w[:, 2 * D_MODEL:]], axis=1)


def _win_qkv_weight(w_qkv):
    qw = WIN_Q_HEADS * HEAD_DIM
    kvw = WIN_KV_HEADS * HEAD_DIM
    w = w_qkv.astype(BF16)

    def twice(cols):
        d_in = cols.shape[0]
        heads = cols.reshape(d_in, WIN_KV_HEADS, 1, HEAD_DIM)
        return jnp.broadcast_to(heads, (d_in, WIN_KV_HEADS, 2, HEAD_DIM)).reshape(d_in, 2 * kvw)

    return jnp.concatenate([w[:, :qw], twice(w[:, qw:qw + kvw]), twice(w[:, qw + kvw:])], axis=1)


def kernel(x, c, ctx, c_ctx, ada_w, ada_b, norm1, norm2, ffn_w_gate_up, ffn_w_down, diff_w_qkv, diff_w_o, diff_q_norm, diff_k_norm, diff_lam_q1, diff_lam_k1, diff_lam_q2, diff_lam_k2, diff_subln, sc_w_in, sc_conv_w, sc_w_out, win_w_qkv, win_w_o, win_q_norm, win_k_norm, win_sink, cf_w_pw1, cf_b_pw1, cf_dw_w, cf_dw_b, cf_ln_g, cf_ln_b, cf_w_pw2, cf_b_pw2):
    n = x.shape[1]
    h, hc = x.reshape(n, D_MODEL), ctx.reshape(ctx.shape[1], D_MODEL)
    mods = _ada_mod(c, c_ctx, ada_w, ada_b)
    wgu, wd = ffn_w_gate_up.astype(BF16), ffn_w_down.astype(BF16)
    tab = _rope_tables(n, _row_tile(n))
    for i in range(DEPTH):
        kind, j = i % N_MIXERS, i // N_MIXERS
        with_ctx = i < DEPTH - 1
        attn_l = attn_c = None
        if kind == 0:
            lam_init = 0.8 - 0.6 * math.exp(-0.3 * i)
            w = _diff_qkv_weight(diff_w_qkv[j])
            w_o = diff_w_o[j].astype(BF16)
            lamv = jnp.zeros((8, LANES), F32).at[0:4, 0:HEAD_DIM].set(
                jnp.stack([diff_lam_q1[j], diff_lam_k1[j], diff_lam_q2[j], diff_lam_k2[j]]))
            q_scale = HEAD_DIM ** -0.5 * LOG2E
            proj = functools.partial(_qkv_proj, mods=mods, layer=i, n1=norm1[i], w=w,
                                     q_gain=diff_q_norm[j], k_gain=diff_k_norm[j],
                                     kw=D_MODEL, vw=D_MODEL, q_scale=q_scale)
            q_l, k_l, v_l = proj(h, row=0, tab=tab)
            q_c, k_c, v_c = proj(hc, row=1, tab=None)
            attn_l = (_diff_attn(q_l, [(k_l, v_l), (k_c, v_c)], lamv, diff_subln[j], lam_init), w_o)
            if with_ctx:
                attn_c = (_diff_attn(q_c, [(k_c, v_c)], lamv, diff_subln[j], lam_init), w_o)
        elif kind == 1:
            w_in = sc_w_in[j].astype(BF16)
            w_out = sc_w_out[j].astype(BF16)
            b_l, cu_l = _in_proj(h, mods, i, 0, norm1[i], w_in, "short")
            h = _conv_out(cu_l, h, mods, i, 0, sc_conv_w[j], w_out, "short", (b_l,))
            if with_ctx:
                b_c, cu_c = _in_proj(hc, mods, i, 1, norm1[i], w_in, "short")
                hc = _conv_out(cu_c, hc, mods, i, 1, sc_conv_w[j], w_out, "short", (b_c,))
        elif kind == 2:
            w = _win_qkv_weight(win_w_qkv[j])
            w_o = win_w_o[j].astype(BF16)
            kvw = WIN_KV_HEADS * LANES
            sink_rows = jnp.broadcast_to(win_sink[j][:, None], (WIN_Q_HEADS, LANES))
            proj = functools.partial(_qkv_proj, mods=mods, layer=i, n1=norm1[i], w=w,
                                     q_gain=win_q_norm[j], k_gain=win_k_norm[j],
                                     kw=kvw, vw=kvw, q_scale=HEAD_DIM ** -0.5)
            q_l, k_l, v_l = proj(h, row=0, tab=tab)
            q_c, k_c, v_c = proj(hc, row=1, tab=None)
            o_l = _win_attn(q_l, k_l, v_l, k_c, v_c, sink_rows, True)
            attn_l = (o_l, w_o)
            if with_ctx:
                attn_c = (_win_attn(q_c, None, None, k_c, v_c, sink_rows, False), w_o)
        else:
            w1 = cf_w_pw1[j].astype(BF16)
            w2 = cf_w_pw2[j].astype(BF16)
            extra = (cf_dw_b[j], cf_ln_g[j], cf_ln_b[j], cf_b_pw2[j])
            glu_l = _in_proj(h, mods, i, 0, norm1[i], w1, "glu", cf_b_pw1[j])
            h = _conv_out(glu_l, h, mods, i, 0, cf_dw_w[j], w2, "conf", extra)
            if with_ctx:
                glu_c = _in_proj(hc, mods, i, 1, norm1[i], w1, "glu", cf_b_pw1[j])
                hc = _conv_out(glu_c, hc, mods, i, 1, cf_dw_w[j], w2, "conf", extra)
        h = _ffn(h, mods, i, 0, norm2[i], wgu, wd, attn_l)
        if with_ctx:
            hc = _ffn(hc, mods, i, 1, norm2[i], wgu, wd, attn_c)
    return h.reshape(1, n, D_MODEL)
```
